```python
import jax, jax.numpy as jnp
from jax import lax
import numpy as np

D_MODEL = 2048
BATCH = 1
SEQ = 8192
DEPTH = 4

N_MIXERS = 3
HEAD_DIM = 128
ROPE_THETA = 500000.0
ROPE_DIM = HEAD_DIM // 4
EPS = 1e-6
Q_CHUNK = 128
NEG = -1e30
POS_BIG = 1e30

MOBA_HEADS = D_MODEL // HEAD_DIM
MOBA_BLOCK = 256
MOBA_TOPK = 3
MOBA_IN = 4 * MOBA_HEADS * HEAD_DIM

NSA_HEADS = D_MODEL // HEAD_DIM
NSA_KV_GROUPS = NSA_HEADS // 4
NSA_CMP_LEN = 32
NSA_CMP_STRIDE = 16
NSA_CMP_HIDDEN = 256
NSA_SLC_BLOCK = 64
NSA_SLC_TOPK = 16
NSA_WINDOW = 512
NSA_IN = 2 * NSA_HEADS * HEAD_DIM + 6 * NSA_KV_GROUPS * HEAD_DIM + 3 * NSA_HEADS

RET_HEADS = 8
RET_QK_DIM = D_MODEL // RET_HEADS
RET_V_DIM = 2 * D_MODEL // RET_HEADS
RET_CHUNK = 128
RET_ROT_THETA = 10000.0
RET_IN = 2 * RET_HEADS * RET_QK_DIM + 2 * RET_HEADS * RET_V_DIM

kernel_name = 'hybrid_moba_nsa_retention_trunk'

f32 = jnp.float32


def rms_norm(x, g):
    xf = x.astype(f32)
    y = xf * lax.rsqrt(jnp.mean(xf * xf, axis=-1, keepdims=True) + EPS)
    return (y * g.astype(f32)).astype(x.dtype)


def rotate(x, pos, rot_dim, theta):
    half = rot_dim // 2
    inv = theta ** (-jnp.arange(half, dtype=f32) / half)
    ang = pos.astype(f32)[..., None] * inv
    cos = jnp.cos(ang)[:, :, None, :]
    sin = jnp.sin(ang)[:, :, None, :]
    xr = x[..., :rot_dim].astype(f32)
    x1, x2 = xr[..., :half], xr[..., half:]
    rot = jnp.concatenate([x1 * cos - x2 * sin, x1 * sin + x2 * cos], axis=-1).astype(x.dtype)
    return jnp.concatenate([rot, x[..., rot_dim:]], axis=-1)


def masked_softmax(s, mask):
    s = jnp.where(mask, s.astype(f32), NEG)
    m = jnp.max(s, axis=-1, keepdims=True)
    p = jnp.where(mask, jnp.exp(s - m), 0.0)
    return p / jnp.maximum(jnp.sum(p, axis=-1, keepdims=True), 1e-30)


def moba_mixer(h, pos, w_in, w_out):
    B, S, _ = h.shape
    H, Dh, L = MOBA_HEADS, HEAD_DIM, MOBA_BLOCK
    q, k, v, z = jnp.split(h @ w_in, 4, axis=-1)
    q = rotate(q.reshape(B, S, H, Dh), pos, ROPE_DIM, ROPE_THETA) * (Dh ** -0.5)
    k = rotate(k.reshape(B, S, H, Dh), pos, ROPE_DIM, ROPE_THETA)
    v = v.reshape(B, S, H, Dh)
    nb = -(-S // L)
    pad = ((0, 0), (0, nb * L - S), (0, 0), (0, 0))
    kb = jnp.pad(k, pad).reshape(B, nb, L, H, Dh).transpose(0, 3, 1, 2, 4)
    vb = jnp.pad(v, pad).reshape(B, nb, L, H, Dh).transpose(0, 3, 1, 2, 4)
    k_mean = jnp.mean(kb.astype(f32), axis=3)
    topk = min(MOBA_TOPK, nb)
    nq = S // Q_CHUNK
    qc = q.reshape(B, nq, Q_CHUNK, H, Dh).transpose(1, 0, 3, 2, 4)
    bi = jnp.arange(B)[:, None, None]
    hi = jnp.arange(H)[None, :, None]
    bshape = (B, H, Q_CHUNK, L)

    def chunk(args):
        qi, ci = args
        q0 = ci * Q_CHUNK
        blk = q0 // L
        qpos = q0 + jnp.arange(Q_CHUNK)
        gate_s = jnp.einsum('bhqd,bhnd->bhqn', qi.astype(f32), k_mean)
        gate_s = jnp.where(jnp.arange(nb) < blk, gate_s, NEG)
        _, idx = lax.top_k(gate_s, topk)
        slot_ok = jnp.arange(topk) < blk
        k_own = lax.dynamic_index_in_dim(kb, blk, axis=2, keepdims=False)
        v_own = lax.dynamic_index_in_dim(vb, blk, axis=2, keepdims=False)
        own_mask = (blk * L + jnp.arange(L))[None, :] <= qpos[:, None]
        scores = [jnp.einsum('bhqd,bhld->bhql', qi, k_own)]
        masks = [jnp.broadcast_to(own_mask, bshape)]
        for s in range(topk):
            k_sel = kb[bi, hi, idx[..., s]]
            scores.append(jnp.einsum('bhqd,bhqld->bhql', qi, k_sel))
            masks.append(jnp.broadcast_to(slot_ok[s], bshape))
        p = masked_softmax(jnp.concatenate(scores, -1), jnp.concatenate(masks, -1))
        p = p.reshape(B, H, Q_CHUNK, topk + 1, L)
        o = jnp.einsum('bhql,bhld->bhqd', p[:, :, :, 0], v_own.astype(f32))
        for s in range(topk):
            v_sel = vb[bi, hi, idx[..., s]]
            o = o + jnp.einsum('bhql,bhqld->bhqd', p[:, :, :, s + 1], v_sel.astype(f32))
        return o.astype(h.dtype)

    o = lax.map(chunk, (qc, jnp.arange(nq)))
    o = o.transpose(1, 0, 3, 2, 4).reshape(B, S, H * Dh)
    return ((o * jax.nn.silu(z)) @ w_out).astype(h.dtype)


def nsa_compress(t, pe, w1, w2):
    B, S, G, Dh = t.shape
    n_sub = NSA_CMP_LEN // NSA_CMP_STRIDE
    sub = t.reshape(B, S // NSA_CMP_STRIDE, NSA_CMP_STRIDE, G, Dh)
    nc = S // NSA_CMP_STRIDE - n_sub + 1
    blocks = jnp.concatenate([sub[:, j:j + nc] for j in range(n_sub)], axis=2)
    blocks = blocks + pe[None, None, :, None, :]
    flat = blocks.transpose(0, 1, 3, 2, 4).reshape(B, nc, G, NSA_CMP_LEN * Dh)
    return jax.nn.silu(flat @ w1) @ w2


def nsa_overlap(nc, ns):
    cs = np.arange(nc)[:, None] * NSA_CMP_STRIDE
    ss = np.arange(ns)[None, :] * NSA_SLC_BLOCK
    ov = np.clip(np.minimum(cs + NSA_CMP_LEN, ss + NSA_SLC_BLOCK) - np.maximum(cs, ss), 0, None)
    return (ov / NSA_CMP_LEN).astype(np.float32)


def nsa_mixer(h, pos, w_in, w_out, cmp_pe, wk1, wk2, wv1, wv2):
    B, S, _ = h.shape
    H, G, Dh, W, SL = NSA_HEADS, NSA_KV_GROUPS, HEAD_DIM, NSA_WINDOW, NSA_SLC_BLOCK
    R = H // G
    sizes = (H * Dh,) + (G * Dh,) * 6 + (3 * H, H * Dh)
    q, k_c, v_c, k_s, v_s, k_w, v_w, gate_logits, z = jnp.split(
        h @ w_in, np.cumsum(sizes)[:-1].tolist(), axis=-1)
    q = rotate(q.reshape(B, S, H, Dh), pos, ROPE_DIM, ROPE_THETA) * (Dh ** -0.5)
    k_c, k_s, k_w = [rotate(t.reshape(B, S, G, Dh), pos, ROPE_DIM, ROPE_THETA) for t in (k_c, k_s, k_w)]
    v_c, v_s, v_w = [t.reshape(B, S, G, Dh) for t in (v_c, v_s, v_w)]
    gates = jax.nn.sigmoid(gate_logits.astype(f32)).reshape(B, S, G, R, 3)
    k_cmp = nsa_compress(k_c, cmp_pe, wk1, wk2)
    v_cmp = nsa_compress(v_c, cmp_pe, wv1, wv2).astype(f32)
    nc = k_cmp.shape[1]
    cmp_end = jnp.arange(nc) * NSA_CMP_STRIDE + NSA_CMP_LEN - 1
    ns = S // SL
    ks_b = k_s.reshape(B, ns, SL, G, Dh).transpose(0, 3, 1, 2, 4)
    vs_b = v_s.reshape(B, ns, SL, G, Dh).transpose(0, 3, 1, 2, 4)
    overlap = jnp.asarray(nsa_overlap(nc, ns))
    n_sel = min(NSA_SLC_TOPK, ns)
    blk_id = jnp.arange(ns)
    k_wp = jnp.pad(k_w, ((0, 0), (W, 0), (0, 0), (0, 0)))
    v_wp = jnp.pad(v_w, ((0, 0), (W, 0), (0, 0), (0, 0)))
    nq = S // Q_CHUNK
    qc = q.reshape(B, nq, Q_CHUNK, G, R, Dh).transpose(1, 0, 2, 3, 4, 5)
    gc = gates.reshape(B, nq, Q_CHUNK, G, R, 3).transpose(1, 0, 2, 3, 4, 5)
    bi = jnp.arange(B)[:, None, None, None]
    gi = jnp.arange(G)[None, :, None, None]

    def chunk(args):
        qi, g_i, ci = args
        q0 = ci * Q_CHUNK
        qpos = q0 + jnp.arange(Q_CHUNK)
        s_c = jnp.einsum('bqgrd,bngd->bgrqn', qi, k_cmp)
        p_c = masked_softmax(s_c, cmp_end[None, :] <= qpos[:, None])
        o_c = jnp.einsum('bgrqn,bngd->bqgrd', p_c, v_cmp)
        imp = jnp.einsum('bgrqn,ns->bgqs', p_c, overlap)
        cur = qpos // SL
        valid = (blk_id * SL)[None, :] <= qpos[:, None]
        forced = (blk_id[None, :] == 0) | (blk_id[None, :] == cur[:, None]) | (blk_id[None, :] == cur[:, None] - 1)
        sel = jnp.where(valid, jnp.where(forced, POS_BIG, imp), NEG)
        top_val, idx = lax.top_k(sel, n_sel)
        slot_ok = top_val > NEG / 2
        k_sel = ks_b[bi, gi, idx]
        v_sel = vs_b[bi, gi, idx].astype(f32)
        tok = idx[..., None] * SL + jnp.arange(SL)
        m_s = slot_ok[..., None] & (tok <= qpos[None, None, :, None, None])
        s_s = jnp.einsum('bqgrd,bgqnld->bgrqnl', qi, k_sel).reshape(B, G, R, Q_CHUNK, n_sel * SL)
        p_s = masked_softmax(s_s, m_s.reshape(B, G, 1, Q_CHUNK, n_sel * SL))
        o_s = jnp.einsum('bgrqnl,bgqnld->bqgrd', p_s.reshape(B, G, R, Q_CHUNK, n_sel, SL), v_sel)
        kw = lax.dynamic_slice_in_dim(k_wp, q0, W + Q_CHUNK, axis=1)
        vw = lax.dynamic_slice_in_dim(v_wp, q0, W + Q_CHUNK, axis=1).astype(f32)
        kpos = q0 - W + jnp.arange(W + Q_CHUNK)
        m_w = (kpos[None, :] <= qpos[:, None]) & (kpos[None, :] > qpos[:, None] - W) & (kpos[None, :] >= 0)
        p_w = masked_softmax(jnp.einsum('bqgrd,bkgd->bgrqk', qi, kw), m_w)
        o_w = jnp.einsum('bgrqk,bkgd->bqgrd', p_w, vw)
        o = g_i[..., 0:1] * o_c + g_i[..., 1:2] * o_s + g_i[..., 2:3] * o_w
        return o.astype(h.dtype)

    o = lax.map(chunk, (qc, gc, jnp.arange(nq)))
    o = o.transpose(1, 0, 2, 3, 4, 5).reshape(B, S, H * Dh)
    return ((o * jax.nn.silu(z)) @ w_out).astype(h.dtype)


def retention_mixer(h, pos, w_in, w_out):
    B, S, _ = h.shape
    H, Dk, Dv, C = RET_HEADS, RET_QK_DIM, RET_V_DIM, RET_CHUNK
    q, k, v, g = jnp.split(h @ w_in, [H * Dk, 2 * H * Dk, 2 * H * Dk + H * Dv], axis=-1)
    q = rotate(q.reshape(B, S, H, Dk), pos, Dk, RET_ROT_THETA).astype(f32)
    k = rotate(k.reshape(B, S, H, Dk), pos, Dk, RET_ROT_THETA).astype(f32) * (Dk ** -0.5)
    v = v.reshape(B, S, H, Dv).astype(f32)
    n = S // C
    to_chunks = lambda t: t.reshape(B, n, C, H, t.shape[-1]).transpose(1, 0, 3, 2, 4)
    log_gamma = jnp.log(1.0 - 2.0 ** (-5.0 - jnp.arange(H, dtype=f32)))
    i = jnp.arange(C, dtype=f32)
    diff = i[:, None] - i[None, :]
    intra = jnp.where(diff >= 0, jnp.exp(log_gamma[:, None, None] * jnp.maximum(diff, 0.0)), 0.0)
    read_decay = jnp.exp(log_gamma[:, None] * (i + 1.0))[..., None]
    write_decay = jnp.exp(log_gamma[:, None] * (C - 1.0 - i))[..., None]
    carry_decay = jnp.exp(log_gamma * C)[:, None, None]

    def step(state, inp):
        qi, ki, vi = inp
        inner = jnp.einsum('bhid,bhjd->bhij', qi, ki) * intra
        o = jnp.einsum('bhij,bhjv->bhiv', inner, vi) + jnp.einsum('bhid,bhdv->bhiv', qi, state) * read_decay
        state = carry_decay * state + jnp.einsum('bhjd,bhjv->bhdv', ki * write_decay, vi)
        return state, o

    state0 = jnp.zeros((B, H, Dk, Dv), f32)
    _, o = lax.scan(step, state0, (to_chunks(q), to_chunks(k), to_chunks(v)))
    o = o.transpose(1, 0, 3, 2, 4).reshape(B, S, H, Dv)
    mu = jnp.mean(o, axis=-1, keepdims=True)
    var = jnp.mean(jnp.square(o - mu), axis=-1, keepdims=True)
    o = ((o - mu) * lax.rsqrt(var + EPS)).reshape(B, S, H * Dv).astype(h.dtype)
    return ((jax.nn.silu(g) * o) @ w_out).astype(h.dtype)


def setup_inputs(seed: int = 0) -> dict:
    key = jax.random.key(seed)
    ks = jax.random.split(key, 20)
    n_moba = len(range(0, DEPTH, N_MIXERS))
    n_nsa = len(range(1, DEPTH, N_MIXERS))
    n_ret = len(range(2, DEPTH, N_MIXERS))

    def w(k, shape, fan_in):
        return jax.random.normal(k, shape, f32) * (fan_in ** -0.5)

    D = D_MODEL
    offset = jax.random.randint(ks[2], (BATCH, 1), 0, 4096, dtype=jnp.int32)
    return {
        'x': jax.random.normal(ks[0], (BATCH, SEQ, D), f32),
        'c': jax.random.normal(ks[1], (BATCH, D), f32),
        'positions': offset + jnp.arange(SEQ, dtype=jnp.int32)[None, :],
        'ada_w': w(ks[3], (DEPTH, D, 3 * D), D),
        'ada_b': 0.01 * jax.random.normal(ks[4], (DEPTH, 3 * D), f32),
        'norm_pre': 1.0 + 0.05 * jax.random.normal(ks[5], (DEPTH, D), f32),
        'norm_post': 1.0 + 0.05 * jax.random.normal(ks[6], (DEPTH, D), f32),
        'moba_w_in': w(ks[7], (n_moba, D, MOBA_IN), D),
        'moba_w_out': w(ks[8], (n_moba, MOBA_HEADS * HEAD_DIM, D), MOBA_HEADS * HEAD_DIM),
        'nsa_w_in': w(ks[9], (n_nsa, D, NSA_IN), D),
        'nsa_w_out': w(ks[10], (n_nsa, NSA_HEADS * HEAD_DIM, D), NSA_HEADS * HEAD_DIM),
        'nsa_cmp_pe': 0.1 * jax.random.normal(ks[11], (n_nsa, NSA_CMP_LEN, HEAD_DIM), f32),
        'nsa_cmp_wk1': w(ks[12], (n_nsa, NSA_CMP_LEN * HEAD_DIM, NSA_CMP_HIDDEN), NSA_CMP_LEN * HEAD_DIM),
        'nsa_cmp_wk2': w(ks[13], (n_nsa, NSA_CMP_HIDDEN, HEAD_DIM), NSA_CMP_HIDDEN),
        'nsa_cmp_wv1': w(ks[14], (n_nsa, NSA_CMP_LEN * HEAD_DIM, NSA_CMP_HIDDEN), NSA_CMP_LEN * HEAD_DIM),
        'nsa_cmp_wv2': w(ks[15], (n_nsa, NSA_CMP_HIDDEN, HEAD_DIM), NSA_CMP_HIDDEN),
        'ret_w_in': w(ks[16], (n_ret, D, RET_IN), D),
        'ret_w_out': w(ks[17], (n_ret, RET_HEADS * RET_V_DIM, D), RET_HEADS * RET_V_DIM),
    }


def reference(x, c, positions, ada_w, ada_b, norm_pre, norm_post, moba_w_in, moba_w_out,
              nsa_w_in, nsa_w_out, nsa_cmp_pe, nsa_cmp_wk1, nsa_cmp_wk2, nsa_cmp_wv1, nsa_cmp_wv2,
              ret_w_in, ret_w_out):
    for i in range(DEPTH):
        kind, j = i % N_MIXERS, i // N_MIXERS
        mod = jax.nn.silu(c) @ ada_w[i] + ada_b[i]
        shift, scale, gate = jnp.split(mod, 3, axis=-1)
        h = rms_norm(x, norm_pre[i]) * (1.0 + scale[:, None, :]) + shift[:, None, :]
        if kind == 0:
            y = moba_mixer(h, positions, moba_w_in[j], moba_w_out[j])
        elif kind == 1:
            y = nsa_mixer(h, positions, nsa_w_in[j], nsa_w_out[j], nsa_cmp_pe[j],
                          nsa_cmp_wk1[j], nsa_cmp_wk2[j], nsa_cmp_wv1[j], nsa_cmp_wv2[j])
        else:
            y = retention_mixer(h, positions, ret_w_in[j], ret_w_out[j])
        x = x + gate[:, None, :] * rms_norm(y, norm_post[i])
    return x
```

```python
import functools

import numpy as np
import jax
import jax.numpy as jnp
from jax import lax
from jax.experimental import pallas as pl
from jax.experimental.pallas import tpu as pltpu

f32 = jnp.float32
bf16 = jnp.bfloat16
i32 = jnp.int32

D_MODEL = 2048
SEQ = 8192
DEPTH = 4
N_MIXERS = 3
HEAD_DIM = 128
ROPE_THETA = 500000.0
ROPE_DIM = HEAD_DIM // 4
EPS = 1e-6
NEG = -1e30
POS_BIG = 1e30

MOBA_HEADS = D_MODEL // HEAD_DIM
MOBA_BLOCK = 256
MOBA_TOPK = 3
MOBA_NB = SEQ // MOBA_BLOCK

NSA_HEADS = D_MODEL // HEAD_DIM
NSA_KV_GROUPS = NSA_HEADS // 4
NSA_REP = NSA_HEADS // NSA_KV_GROUPS
NSA_CMP_LEN = 32
NSA_CMP_STRIDE = 16
NSA_CMP_HIDDEN = 256
NSA_SLC_BLOCK = 64
NSA_SLC_TOPK = 16
NSA_WINDOW = 512
NSA_NC_PAD = SEQ // NSA_CMP_STRIDE
NSA_NS = SEQ // NSA_SLC_BLOCK
NSA_GATE_ROWS = 16

RET_HEADS = 8
RET_QK_DIM = D_MODEL // RET_HEADS
RET_V_DIM = 2 * D_MODEL // RET_HEADS
RET_CHUNK = 256
RET_ROT_THETA = 10000.0

VMEM_LIMIT = 56 * 1024 * 1024

TN_DIMS = (((0,), (0,)), ((), ()))
NT_DIMS = (((1,), (1,)), ((), ()))


def _params(*sem):
    return pltpu.CompilerParams(dimension_semantics=sem, vmem_limit_bytes=VMEM_LIMIT)


def _rope_table_kernel(pos_ref, inv_ref, cos_ref, sin_ref):
    ang = pos_ref[...].astype(f32) * inv_ref[...]
    cos_ref[...] = jnp.cos(ang)
    sin_ref[...] = jnp.sin(ang)


def rope_tables(positions, half, theta):
    inv = (theta ** (-jnp.arange(half, dtype=f32) / half)).reshape(half, 1)
    ts = 1024
    return pl.pallas_call(
        _rope_table_kernel,
        grid=(SEQ // ts,),
        in_specs=[pl.BlockSpec((1, ts), lambda i: (0, i)),
                  pl.BlockSpec((half, 1), lambda i: (0, 0))],
        out_specs=[pl.BlockSpec((half, ts), lambda i: (0, i))] * 2,
        out_shape=[jax.ShapeDtypeStruct((half, SEQ), f32)] * 2,
        name=f"rope_tables_{half}",
    )(positions, inv)


def _ada_kernel(c_ref, w_ref, b_ref, o_ref):
    sc = jax.nn.silu(c_ref[...])
    sc8 = jnp.broadcast_to(sc, (8, D_MODEL)).astype(bf16)
    acc = jnp.dot(sc8, w_ref[0].astype(bf16), preferred_element_type=f32)
    o_ref[0] = acc[0:1] + b_ref[0]


def ada_modulation(c, ada_w, ada_b):
    tn = 1024
    n3 = 3 * D_MODEL
    out = pl.pallas_call(
        _ada_kernel,
        grid=(DEPTH, n3 // tn),
        in_specs=[pl.BlockSpec((1, D_MODEL), lambda l, j: (0, 0)),
                  pl.BlockSpec((1, D_MODEL, tn), lambda l, j: (l, 0, j)),
                  pl.BlockSpec((1, 1, tn), lambda l, j: (l, 0, j))],
        out_specs=pl.BlockSpec((1, 1, tn), lambda l, j: (l, 0, j)),
        out_shape=jax.ShapeDtypeStruct((DEPTH, 1, n3), f32),
        compiler_params=_params("arbitrary", "arbitrary"),
        name="ada_modulation",
    )(c, ada_w, ada_b.reshape(DEPTH, 1, n3))
    return out.reshape(DEPTH, n3)


def _prenorm_kernel(x_ref, g_ref, sc_ref, sh_ref, o_ref):
    x = x_ref[...]
    ms = jnp.mean(x * x, axis=0, keepdims=True)
    y = (x * lax.rsqrt(ms + EPS)) * g_ref[...]
    o_ref[...] = (y * (1.0 + sc_ref[...]) + sh_ref[...]).astype(bf16)


def prenorm(xT, g_col, scale_col, shift_col):
    tm = 512
    col = pl.BlockSpec((D_MODEL, 1), lambda i: (0, 0))
    return pl.pallas_call(
        _prenorm_kernel,
        grid=(SEQ // tm,),
        in_specs=[pl.BlockSpec((D_MODEL, tm), lambda i: (0, i)), col, col, col],
        out_specs=pl.BlockSpec((D_MODEL, tm), lambda i: (0, i)),
        out_shape=jax.ShapeDtypeStruct((D_MODEL, SEQ), bf16),
        compiler_params=_params("arbitrary"),
        name="prenorm",
    )(xT, g_col, scale_col, shift_col)


def _proj_kernel(*refs, kind, tn, slab, half, scale, scale_lo, scale_hi, token_major):
    if kind == "rope":
        h_ref, w_ref, cos_ref, sin_ref, o_ref = refs
    else:
        h_ref, w_ref, o_ref = refs
    acc = jnp.dot(w_ref[...], h_ref[...], preferred_element_type=f32)
    if scale is not None:
        j = pl.program_id(1)
        acc = acc * jnp.where((j >= scale_lo) & (j < scale_hi), scale, 1.0).astype(f32)
    if kind == "silu":
        acc = acc * jax.nn.sigmoid(acc)
    elif kind == "sigmoid":
        acc = jax.nn.sigmoid(acc)

    if kind == "rope":
        cos = cos_ref[...]
        sin = sin_ref[...]
        pieces = []
        for s in range(tn // slab):
            b = s * slab
            x1 = acc[b:b + half]
            x2 = acc[b + half:b + 2 * half]
            pieces.append(x1 * cos - x2 * sin)
            pieces.append(x1 * sin + x2 * cos)
            if 2 * half < slab:
                pieces.append(acc[b + 2 * half:b + slab])
        acc = jnp.concatenate(pieces, axis=0)
    if token_major:
        for s in range(tn // HEAD_DIM):
            o_ref[s] = acc[s * HEAD_DIM:(s + 1) * HEAD_DIM].T.astype(o_ref.dtype)
    else:
        o_ref[...] = acc.astype(o_ref.dtype)


def project(hT, wT, kind, *, tables=None, slab=HEAD_DIM, half=ROPE_DIM // 2, scale=None,
            scale_rows=(0, 0), token_major=False, out_dtype=bf16, tn=512, name="proj"):
    n = wT.shape[0]
    tm = 1024
    tn = min(tn, n)
    assert n % tn == 0 and scale_rows[0] % tn == 0 and scale_rows[1] % tn == 0
    assert kind != "rope" or tn % slab == 0
    in_specs = [pl.BlockSpec((D_MODEL, tm), lambda i, j: (0, i)),
                pl.BlockSpec((tn, D_MODEL), lambda i, j: (j, 0))]
    args = [hT, wT]
    if kind == "rope":
        in_specs += [pl.BlockSpec((half, tm), lambda i, j: (0, i))] * 2
        args += list(tables)
    if token_major:
        out_spec = pl.BlockSpec((tn // HEAD_DIM, tm, HEAD_DIM), lambda i, j: (j, i, 0))
        out_shape = jax.ShapeDtypeStruct((n // HEAD_DIM, SEQ, HEAD_DIM), out_dtype)
    else:
        out_spec = pl.BlockSpec((tn, tm), lambda i, j: (j, i))
        out_shape = jax.ShapeDtypeStruct((n, SEQ), out_dtype)
    body = functools.partial(
        _proj_kernel, kind=kind, tn=tn, slab=slab, half=half, scale=scale,
        scale_lo=scale_rows[0] // tn, scale_hi=scale_rows[1] // tn, token_major=token_major)
    return pl.pallas_call(
        body, grid=(SEQ // tm, n // tn), in_specs=in_specs, out_specs=out_spec,
        out_shape=out_shape, compiler_params=_params("arbitrary", "arbitrary"), name=name,
    )(*args)


def _outproj_kernel(a_ref, w_ref, x_ref, gate_ref, g_ref, o_ref):
    y = jnp.dot(w_ref[...], a_ref[...], preferred_element_type=f32)
    ms = jnp.mean(y * y, axis=0, keepdims=True)
    yn = (y * lax.rsqrt(ms + EPS)) * g_ref[...]
    o_ref[...] = x_ref[...] + gate_ref[...] * yn


def outproj_residual(aT, w_outT, xT, gate_col, g_col):
    k = aT.shape[0]
    tm = 256
    col = pl.BlockSpec((D_MODEL, 1), lambda i: (0, 0))
    return pl.pallas_call(
        _outproj_kernel,
        grid=(SEQ // tm,),
        in_specs=[pl.BlockSpec((k, tm), lambda i: (0, i)),
                  pl.BlockSpec((D_MODEL, k), lambda i: (0, 0)),
                  pl.BlockSpec((D_MODEL, tm), lambda i: (0, i)), col, col],
        out_specs=pl.BlockSpec((D_MODEL, tm), lambda i: (0, i)),
        out_shape=jax.ShapeDtypeStruct((D_MODEL, SEQ), f32),
        compiler_params=_params("arbitrary"),
        name="outproj_residual",
    )(aT, w_outT, xT, gate_col, g_col)


def _online_step(sT, vT, m, l, acc):
    mn = jnp.maximum(m, jnp.max(sT, axis=0, keepdims=True))
    a = jnp.exp(m - mn)
    p = jnp.exp(sT - mn)
    l = a * l + jnp.sum(p, axis=0, keepdims=True)
    acc = a * acc + jnp.dot(vT, p.astype(bf16), preferred_element_type=f32)
    return mn, l, acc


def _first_k_select(val, idx, n_rows, k, slot_ok):
    sel = jnp.zeros(val.shape, f32)
    for t in range(k):
        m = jnp.max(val, axis=0, keepdims=True)
        first = jnp.min(jnp.where(val == m, idx, n_rows), axis=0, keepdims=True)
        pick = idx == first
        sel = jnp.where(pick, slot_ok(t, m), sel)
        val = jnp.where(pick, -jnp.inf, val)
    return sel


def _moba_attn_kernel(q_ref, k_ref, v_ref, z_ref, avg_ref, o_ref, kmean_ref, sel_ref):
    tq = MOBA_BLOCK
    i = pl.program_id(1)

    @pl.when(i == 0)
    def _():
        kmean_ref[...] = lax.dot_general(avg_ref[...], k_ref[...], NT_DIMS,
                                         preferred_element_type=f32)

    qT = q_ref[...]
    gate = jnp.dot(kmean_ref[...], qT.astype(f32), precision=lax.Precision.HIGHEST,
                   preferred_element_type=f32)
    nidx = lax.broadcasted_iota(i32, (MOBA_NB, tq), 0)
    gate = jnp.where(nidx < i, gate, NEG)
    sel_ref[...] = _first_k_select(gate, nidx, MOBA_NB, MOBA_TOPK, lambda t, m: jnp.where(t < i, 1.0, 0.0))

    def scores(n):
        off = pl.multiple_of(n * MOBA_BLOCK, MOBA_BLOCK)
        kT = k_ref[:, pl.ds(off, MOBA_BLOCK)]
        sT = lax.dot_general(kT, qT, TN_DIMS, preferred_element_type=f32)
        return sT, v_ref[:, pl.ds(off, MOBA_BLOCK)]

    sT, vT = scores(i)
    kpos = lax.broadcasted_iota(i32, (MOBA_BLOCK, tq), 0)
    qpos = lax.broadcasted_iota(i32, (MOBA_BLOCK, tq), 1)
    sT = jnp.where(kpos <= qpos, sT, NEG)
    m0 = jnp.full((1, tq), NEG, f32)
    carry = _online_step(sT, vT, m0, jnp.zeros((1, tq), f32), jnp.zeros((HEAD_DIM, tq), f32))

    def body(n, carry):
        sT, vT = scores(n)
        sT = jnp.where(sel_ref[pl.ds(n, 1), :] > 0.5, sT, NEG)
        return _online_step(sT, vT, *carry)

    m, l, acc = lax.fori_loop(0, i, body, carry)
    o = acc / jnp.maximum(l, 1e-30)
    o_ref[...] = (o * z_ref[...].astype(f32)).astype(bf16)


def moba_attention(qkT, vT, zT):
    tq = MOBA_BLOCK
    h = MOBA_HEADS
    avg = ((jnp.arange(SEQ)[None, :] // MOBA_BLOCK) == jnp.arange(MOBA_NB)[:, None])
    avg = (avg.astype(f32) / MOBA_BLOCK).astype(bf16)
    return pl.pallas_call(
        _moba_attn_kernel,
        grid=(h, SEQ // tq),
        in_specs=[pl.BlockSpec((HEAD_DIM, tq), lambda hh, i: (hh, i)),
                  pl.BlockSpec((HEAD_DIM, SEQ), lambda hh, i: (h + hh, 0)),
                  pl.BlockSpec((HEAD_DIM, SEQ), lambda hh, i: (hh, 0)),
                  pl.BlockSpec((HEAD_DIM, tq), lambda hh, i: (hh, i)),
                  pl.BlockSpec((MOBA_NB, SEQ), lambda hh, i: (0, 0))],
        out_specs=pl.BlockSpec((HEAD_DIM, tq), lambda hh, i: (hh, i)),
        out_shape=jax.ShapeDtypeStruct((h * HEAD_DIM, SEQ), bf16),
        scratch_shapes=[pltpu.VMEM((MOBA_NB, HEAD_DIM), f32), pltpu.VMEM((MOBA_NB, tq), f32)],
        compiler_params=_params("arbitrary", "arbitrary"),
        name="moba_attention",
    )(qkT, qkT, vT, zT, avg)


def _nsa_cmp_kernel(x_ref, wab_ref, pe_ref, w1_ref, w2_ref, o_ref, *, transpose_out):
    hid = NSA_CMP_HIDDEN
    ab = jnp.dot(x_ref[0], wab_ref[...], preferred_element_type=f32)
    first = ab[:, :hid]
    second = pltpu.roll(ab[:, hid:], NSA_NC_PAD - 1, 0)
    pe8 = jnp.broadcast_to(pe_ref[...], (8, NSA_CMP_LEN * HEAD_DIM)).astype(bf16)
    bias = jnp.dot(pe8, w1_ref[...], preferred_element_type=f32)[0:1]
    hpre = first + second + bias
    out = jnp.dot((hpre * jax.nn.sigmoid(hpre)).astype(bf16), w2_ref[...],
                  preferred_element_type=f32)
    o_ref[0] = (out.T if transpose_out else out).astype(o_ref.dtype)


def nsa_compress(t_tok, pe, w1, w2, transpose_out):
    g = NSA_KV_GROUPS
    half_k = NSA_CMP_STRIDE * HEAD_DIM
    x = t_tok.reshape(g, NSA_NC_PAD, half_k)
    w1b = w1.astype(bf16)
    wab = jnp.concatenate([w1b[:half_k], w1b[half_k:]], axis=1)
    oshape = (g, HEAD_DIM, NSA_NC_PAD) if transpose_out else (g, NSA_NC_PAD, HEAD_DIM)
    return pl.pallas_call(
        functools.partial(_nsa_cmp_kernel, transpose_out=transpose_out),
        grid=(g,),
        in_specs=[pl.BlockSpec((1, NSA_NC_PAD, half_k), lambda gg: (gg, 0, 0)),
                  pl.BlockSpec((half_k, 2 * NSA_CMP_HIDDEN), lambda gg: (0, 0)),
                  pl.BlockSpec((1, NSA_CMP_LEN * HEAD_DIM), lambda gg: (0, 0)),
                  pl.BlockSpec((NSA_CMP_LEN * HEAD_DIM, NSA_CMP_HIDDEN), lambda gg: (0, 0)),
                  pl.BlockSpec((NSA_CMP_HIDDEN, HEAD_DIM), lambda gg: (0, 0))],
        out_specs=pl.BlockSpec((1,) + oshape[1:], lambda gg: (gg, 0, 0)),
        out_shape=jax.ShapeDtypeStruct(oshape, bf16),
        compiler_params=_params("arbitrary"),
        name="nsa_compress",
    )(x, wab, pe.reshape(1, NSA_CMP_LEN * HEAD_DIM), w1b, w2.astype(bf16))


def _nsa_attn_kernel(q_ref, kc_ref, vc_ref, ov_ref, ks_ref, vs_ref, kw_ref, vw_ref,
                     gt_ref, z_ref, o_ref, sel_ref):
    tq = HEAD_DIM
    rep = NSA_REP
    nq = rep * tq
    i = pl.program_id(1)
    q0 = i * tq
    qT = jnp.concatenate([q_ref[r * HEAD_DIM:(r + 1) * HEAD_DIM, :] for r in range(rep)],
                         axis=1)
    lane = lax.broadcasted_iota(i32, (1, nq), 1)
    qpos = q0 + (lane & (tq - 1))
    qpos1 = q0 + lax.broadcasted_iota(i32, (1, tq), 1)

    sc = jnp.dot(kc_ref[0], qT, preferred_element_type=f32)
    cend = lax.broadcasted_iota(i32, (NSA_NC_PAD, nq), 0) * NSA_CMP_STRIDE + (NSA_CMP_LEN - 1)
    ok_c = cend <= qpos
    sc = jnp.where(ok_c, sc, NEG)
    pc = jnp.where(ok_c, jnp.exp(sc - jnp.max(sc, axis=0, keepdims=True)), 0.0)
    pc = pc / jnp.maximum(jnp.sum(pc, axis=0, keepdims=True), 1e-30)
    ocT = jnp.dot(vc_ref[0], pc.astype(bf16), preferred_element_type=f32)

    ps = pc[:, 0:tq]
    for r in range(1, rep):
        ps = ps + pc[:, r * tq:(r + 1) * tq]
    ps_hi = ps.astype(bf16)
    ps_lo = (ps - ps_hi.astype(f32)).astype(bf16)
    imp = (jnp.dot(ov_ref[...], ps_hi, preferred_element_type=f32)
           + jnp.dot(ov_ref[...], ps_lo, preferred_element_type=f32))
    sidx = lax.broadcasted_iota(i32, (NSA_NS, tq), 0)
    cur = jnp.right_shift(qpos1, 6)
    valid = sidx * NSA_SLC_BLOCK <= qpos1
    forced = (sidx == 0) | (sidx == cur) | (sidx == cur - 1)
    val = jnp.where(valid, jnp.where(forced, POS_BIG, imp), NEG)
    sel_ref[...] = _first_k_select(val, sidx, NSA_NS, NSA_SLC_TOPK, lambda t, m: jnp.where(m > NEG / 2, 1.0, 0.0))

    kt = 256
    per = kt // NSA_SLC_BLOCK
    krow = lax.broadcasted_iota(i32, (kt, nq), 0)

    def sel_body(n, carry):
        off = pl.multiple_of(n * kt, kt)
        sT = lax.dot_general(ks_ref[:, pl.ds(off, kt)], qT, TN_DIMS, preferred_element_type=f32)
        rows = [jnp.broadcast_to(sel_ref[pl.ds(per * n + b, 1), :], (NSA_SLC_BLOCK, tq))
                for b in range(per)]
        msk = jnp.concatenate(rows, axis=0)
        msk = jnp.concatenate([msk] * rep, axis=1)
        ok = (msk > 0.5) & (off + krow <= qpos)
        return _online_step(jnp.where(ok, sT, NEG), vs_ref[:, pl.ds(off, kt)], *carry)

    init = (jnp.full((1, nq), NEG, f32), jnp.zeros((1, nq), f32), jnp.zeros((HEAD_DIM, nq), f32))
    _, l_s, acc_s = lax.fori_loop(0, jnp.right_shift(i, 1) + 1, sel_body, init)
    osT = acc_s / jnp.maximum(l_s, 1e-30)

    span = NSA_WINDOW + tq
    start = pl.multiple_of(jnp.maximum(q0 - NSA_WINDOW, 0), tq)
    sw = lax.dot_general(kw_ref[:, pl.ds(start, span)], qT, TN_DIMS, preferred_element_type=f32)
    kpos = start + lax.broadcasted_iota(i32, (span, nq), 0)
    ok_w = (kpos <= qpos) & (kpos > qpos - NSA_WINDOW)
    sw = jnp.where(ok_w, sw, NEG)
    pw = jnp.where(ok_w, jnp.exp(sw - jnp.max(sw, axis=0, keepdims=True)), 0.0)
    pw = pw / jnp.maximum(jnp.sum(pw, axis=0, keepdims=True), 1e-30)
    owT = jnp.dot(vw_ref[:, pl.ds(start, span)], pw.astype(bf16), preferred_element_type=f32)

    gt = gt_ref[0]
    for r in range(rep):
        cs = slice(r * tq, (r + 1) * tq)
        rs = slice(r * HEAD_DIM, (r + 1) * HEAD_DIM)
        o = (gt[3 * r:3 * r + 1] * ocT[:, cs] + gt[3 * r + 1:3 * r + 2] * osT[:, cs]
             + gt[3 * r + 2:3 * r + 3] * owT[:, cs])
        o_ref[rs, :] = (o * z_ref[rs, :].astype(f32)).astype(bf16)


def _nsa_overlap_t():
    nc = NSA_NC_PAD - 1
    cs = np.arange(nc)[:, None] * NSA_CMP_STRIDE
    ss = np.arange(NSA_NS)[None, :] * NSA_SLC_BLOCK
    ov = np.clip(np.minimum(cs + NSA_CMP_LEN, ss + NSA_SLC_BLOCK) - np.maximum(cs, ss), 0, None)
    ov = np.concatenate([ov / NSA_CMP_LEN, np.zeros((1, NSA_NS))], axis=0)
    return jnp.asarray(ov.T, dtype=bf16)


def nsa_attention(ropeT, plainT, k_cmp, v_cmpT, gatesT, zT):
    tq = HEAD_DIM
    g = NSA_KV_GROUPS
    qrows = NSA_REP * HEAD_DIM
    hq = NSA_HEADS
    full = lambda blk: pl.BlockSpec((HEAD_DIM, SEQ), lambda gg, i: (blk + gg, 0))
    return pl.pallas_call(
        _nsa_attn_kernel,
        grid=(g, SEQ // tq),
        in_specs=[pl.BlockSpec((qrows, tq), lambda gg, i: (gg, i)),
                  pl.BlockSpec((1, NSA_NC_PAD, HEAD_DIM), lambda gg, i: (gg, 0, 0)),
                  pl.BlockSpec((1, HEAD_DIM, NSA_NC_PAD), lambda gg, i: (gg, 0, 0)),
                  pl.BlockSpec((NSA_NS, NSA_NC_PAD), lambda gg, i: (0, 0)),
                  full(hq), full(0), full(hq + g), full(g),
                  pl.BlockSpec((1, NSA_GATE_ROWS, tq), lambda gg, i: (gg, 0, i)),
                  pl.BlockSpec((qrows, tq), lambda gg, i: (gg, i))],
        out_specs=pl.BlockSpec((qrows, tq), lambda gg, i: (gg, i)),
        out_shape=jax.ShapeDtypeStruct((NSA_HEADS * HEAD_DIM, SEQ), bf16),
        scratch_shapes=[pltpu.VMEM((NSA_NS, tq), f32)],
        compiler_params=_params("arbitrary", "arbitrary"),
        name="nsa_attention",
    )(ropeT, k_cmp, v_cmpT, _nsa_overlap_t(), ropeT, plainT, ropeT, plainT, gatesT, zT)


def _ret_kernel(q_ref, k_ref, v_ref, g_ref, dt_ref, rd_ref, wd_ref, cd_ref, o_ref, st_ref):
    @pl.when(pl.program_id(1) == 0)
    def _():
        st_ref[...] = jnp.zeros(st_ref.shape, f32)

    qT = q_ref[...]
    kT = k_ref[...]
    vT = v_ref[...]
    innerT = lax.dot_general(kT, qT, TN_DIMS, preferred_element_type=f32) * dt_ref[0]
    st = st_ref[...]
    o = (jnp.dot(vT, innerT.astype(bf16), preferred_element_type=f32)
         + jnp.dot(st.astype(bf16), qT, preferred_element_type=f32) * rd_ref[0])
    kw = (kT.astype(f32) * wd_ref[0]).astype(bf16)
    st_ref[...] = cd_ref[0] * st + lax.dot_general(vT, kw, NT_DIMS, preferred_element_type=f32)
    mu = jnp.mean(o, axis=0, keepdims=True)
    d = o - mu
    var = jnp.mean(d * d, axis=0, keepdims=True)
    o_ref[...] = (g_ref[...].astype(f32) * (d * lax.rsqrt(var + EPS))).astype(bf16)


def retention(qkT, vT, gT):
    c = RET_CHUNK
    h = RET_HEADS
    log_gamma = jnp.log(1.0 - 2.0 ** (-5.0 - jnp.arange(h, dtype=f32)))
    i = jnp.arange(c, dtype=f32)
    diff = i[None, :] - i[:, None]
    dt = jnp.where(diff >= 0, jnp.exp(log_gamma[:, None, None] * jnp.maximum(diff, 0.0)), 0.0)
    rd = jnp.exp(log_gamma[:, None] * (i + 1.0))[:, None, :]
    wd = jnp.exp(log_gamma[:, None] * (c - 1.0 - i))[:, None, :]
    cd = jnp.broadcast_to(jnp.exp(log_gamma * c)[:, None, None], (h, 1, RET_QK_DIM))
    vec = lambda n: pl.BlockSpec((1, 1, n), lambda hh, t: (hh, 0, 0))
    return pl.pallas_call(
        _ret_kernel,
        grid=(h, SEQ // c),
        in_specs=[pl.BlockSpec((RET_QK_DIM, c), lambda hh, t: (hh, t)),
                  pl.BlockSpec((RET_QK_DIM, c), lambda hh, t: (h + hh, t)),
                  pl.BlockSpec((RET_V_DIM, c), lambda hh, t: (hh, t)),
                  pl.BlockSpec((RET_V_DIM, c), lambda hh, t: (hh, t)),
                  pl.BlockSpec((1, c, c), lambda hh, t: (hh, 0, 0)),
                  vec(c), vec(c), vec(RET_QK_DIM)],
        out_specs=pl.BlockSpec((RET_V_DIM, c), lambda hh, t: (hh, t)),
        out_shape=jax.ShapeDtypeStruct((h * RET_V_DIM, SEQ), bf16),
        scratch_shapes=[pltpu.VMEM((RET_V_DIM, RET_QK_DIM), f32)],
        compiler_params=_params("arbitrary", "arbitrary"),
        name="retention",
    )(qkT, qkT, vT, gT, dt, rd, wd, cd)


def _t(w):
    return w.T.astype(bf16)


def moba_mixer(hT, tabs16, w_in, w_out):
    hd = MOBA_HEADS * HEAD_DIM
    qkT = project(hT, _t(w_in[:, :2 * hd]), "rope", tables=tabs16, scale=HEAD_DIM ** -0.5,
                  scale_rows=(0, hd), name="moba_proj_qk")
    vT = project(hT, _t(w_in[:, 2 * hd:3 * hd]), "plain", name="moba_proj_v")
    zT = project(hT, _t(w_in[:, 3 * hd:]), "silu", name="moba_proj_z")
    return moba_attention(qkT, vT, zT), _t(w_out)


def nsa_mixer(hT, tabs16, w_in, w_out, pe, wk1, wk2, wv1, wv2):
    hd = NSA_HEADS * HEAD_DIM
    gd = NSA_KV_GROUPS * HEAD_DIM
    seg = lambda k: w_in[:, hd + k * gd:hd + (k + 1) * gd]
    w_rope = jnp.concatenate([w_in[:, :hd], seg(2), seg(4)], axis=1)
    ropeT = project(hT, _t(w_rope), "rope", tables=tabs16, scale=HEAD_DIM ** -0.5,
                    scale_rows=(0, hd), name="nsa_proj_rope")
    plainT = project(hT, _t(jnp.concatenate([seg(3), seg(5)], axis=1)), "plain", name="nsa_proj_v")
    kc_tok = project(hT, _t(seg(0)), "rope", tables=tabs16, token_major=True, name="nsa_proj_kc")
    vc_tok = project(hT, _t(seg(1)), "plain", token_major=True, name="nsa_proj_vc")
    g0 = hd + 6 * gd
    per = NSA_REP * 3
    wg = w_in[:, g0:g0 + NSA_KV_GROUPS * per].reshape(D_MODEL, NSA_KV_GROUPS, per)
    wg = jnp.pad(wg, ((0, 0), (0, 0), (0, NSA_GATE_ROWS - per))).reshape(D_MODEL, -1)
    gatesT = project(hT, _t(wg), "sigmoid", out_dtype=f32, name="nsa_proj_gate")
    gatesT = gatesT.reshape(NSA_KV_GROUPS, NSA_GATE_ROWS, SEQ)
    zT = project(hT, _t(w_in[:, g0 + NSA_HEADS * 3:]), "silu", name="nsa_proj_z")
    k_cmp = nsa_compress(kc_tok, pe, wk1, wk2, transpose_out=False)
    v_cmpT = nsa_compress(vc_tok, pe, wv1, wv2, transpose_out=True)
    return nsa_attention(ropeT, plainT, k_cmp, v_cmpT, gatesT, zT), _t(w_out)


def retention_mixer(hT, tabs128, w_in, w_out):
    qk = 2 * RET_HEADS * RET_QK_DIM
    vd = RET_HEADS * RET_V_DIM
    qkT = project(hT, _t(w_in[:, :qk]), "rope", tables=tabs128, slab=RET_QK_DIM,
                  half=RET_QK_DIM // 2, scale=RET_QK_DIM ** -0.5, scale_rows=(qk // 2, qk),
                  name="ret_proj_qk")
    vT = project(hT, _t(w_in[:, qk:qk + vd]), "plain", name="ret_proj_v")
    gT = project(hT, _t(w_in[:, qk + vd:]), "silu", name="ret_proj_g")
    return retention(qkT, vT, gT), _t(w_out)


def kernel(x, c, positions, ada_w, ada_b, norm_pre, norm_post, moba_w_in, moba_w_out, nsa_w_in, nsa_w_out, nsa_cmp_pe, nsa_cmp_wk1, nsa_cmp_wk2, nsa_cmp_wv1, nsa_cmp_wv2, ret_w_in, ret_w_out):
    assert x.shape == (1, SEQ, D_MODEL)
    mod = ada_modulation(c, ada_w, ada_b)
    tabs16 = rope_tables(positions, ROPE_DIM // 2, ROPE_THETA)
    tabs128 = rope_tables(positions, RET_QK_DIM // 2, RET_ROT_THETA)
    col = lambda v: v.reshape(D_MODEL, 1)
    xT = x[0].T
    for i in range(DEPTH):
        kind, j = i % N_MIXERS, i // N_MIXERS
        shift, scale, gate = mod[i, :D_MODEL], mod[i, D_MODEL:2 * D_MODEL], mod[i, 2 * D_MODEL:]
        hT = prenorm(xT, col(norm_pre[i]), col(scale), col(shift))
        if kind == 0:
            aT, w_outT = moba_mixer(hT, tabs16, moba_w_in[j], moba_w_out[j])
        elif kind == 1:
            aT, w_outT = nsa_mixer(hT, tabs16, nsa_w_in[j], nsa_w_out[j], nsa_cmp_pe[j],
                                   nsa_cmp_wk1[j], nsa_cmp_wk2[j], nsa_cmp_wv1[j], nsa_cmp_wv2[j])
        else:
            aT, w_outT = retention_mixer(hT, tabs128, ret_w_in[j], ret_w_out[j])
        xT = outproj_residual(aT, w_outT, xT, col(gate), col(norm_post[i]))
    return xT.T[None]
```

```python
import functools

import numpy as np
import jax
import jax.numpy as jnp
from jax import lax
from jax.experimental import pallas as pl
from jax.experimental.pallas import tpu as pltpu

f32 = jnp.float32
bf16 = jnp.bfloat16
i32 = jnp.int32

D_MODEL = 2048
SEQ = 8192
DEPTH = 4
N_MIXERS = 3
HEAD_DIM = 128
ROPE_THETA = 500000.0
ROPE_DIM = HEAD_DIM // 4
EPS = 1e-6
NEG = -1e30
POS_BIG = 1e30

MOBA_HEADS = D_MODEL // HEAD_DIM
MOBA_BLOCK = 256
MOBA_TOPK = 3
MOBA_NB = SEQ // MOBA_BLOCK
MOBA_TQ = 1024
ONES_ROWS = 16

NSA_HEADS = D_MODEL // HEAD_DIM
NSA_KV_GROUPS = NSA_HEADS // 4
NSA_REP = NSA_HEADS // NSA_KV_GROUPS
NSA_CMP_LEN = 32
NSA_CMP_STRIDE = 16
NSA_CMP_HIDDEN = 256
NSA_SLC_BLOCK = 64
NSA_SLC_TOPK = 16
NSA_WINDOW = 512
NSA_NC_PAD = SEQ // NSA_CMP_STRIDE
NSA_NS = SEQ // NSA_SLC_BLOCK
NSA_GATE_ROWS = 16
NSA_TQ = 256
NSA_SPAN = 1024

RET_HEADS = 8
RET_QK_DIM = D_MODEL // RET_HEADS
RET_V_DIM = 2 * D_MODEL // RET_HEADS
RET_CHUNK = 256
RET_ROT_THETA = 10000.0

VMEM_LIMIT = 56 * 1024 * 1024

TN_DIMS = (((0,), (0,)), ((), ()))
NT_DIMS = (((1,), (1,)), ((), ()))


def _params(*sem):
    return pltpu.CompilerParams(dimension_semantics=sem, vmem_limit_bytes=VMEM_LIMIT)


def _rope_table_kernel(pos_ref, inv_ref, cos_ref, sin_ref):
    ang = pos_ref[...].astype(f32) * inv_ref[...]
    cos_ref[...] = jnp.cos(ang)
    sin_ref[...] = jnp.sin(ang)


def rope_tables(positions, half, theta):
    inv = (theta ** (-jnp.arange(half, dtype=f32) / half)).reshape(half, 1)
    ts = 1024
    return pl.pallas_call(
        _rope_table_kernel,
        grid=(SEQ // ts,),
        in_specs=[pl.BlockSpec((1, ts), lambda i: (0, i)),
                  pl.BlockSpec((half, 1), lambda i: (0, 0))],
        out_specs=[pl.BlockSpec((half, ts), lambda i: (0, i))] * 2,
        out_shape=[jax.ShapeDtypeStruct((half, SEQ), f32)] * 2,
        name=f"rope_tables_{half}",
    )(positions, inv)


def _ada_kernel(c_ref, w_ref, b_ref, o_ref):
    sc = jax.nn.silu(c_ref[...])
    sc8 = jnp.broadcast_to(sc, (8, D_MODEL)).astype(bf16)
    acc = jnp.dot(sc8, w_ref[0].astype(bf16), preferred_element_type=f32)
    o_ref[0] = acc[0:1] + b_ref[0]


def ada_modulation(c, ada_w, ada_b):
    tn = 1024
    n3 = 3 * D_MODEL
    out = pl.pallas_call(
        _ada_kernel,
        grid=(DEPTH, n3 // tn),
        in_specs=[pl.BlockSpec((1, D_MODEL), lambda l, j: (0, 0)),
                  pl.BlockSpec((1, D_MODEL, tn), lambda l, j: (l, 0, j)),
                  pl.BlockSpec((1, 1, tn), lambda l, j: (l, 0, j))],
        out_specs=pl.BlockSpec((1, 1, tn), lambda l, j: (l, 0, j)),
        out_shape=jax.ShapeDtypeStruct((DEPTH, 1, n3), f32),
        compiler_params=_params("arbitrary", "arbitrary"),
        name="ada_modulation",
    )(c, ada_w, ada_b.reshape(DEPTH, 1, n3))
    return out.reshape(DEPTH, n3)


def _prenorm_kernel(x_ref, g_ref, sc_ref, sh_ref, o_ref):
    x = x_ref[...]
    ms = jnp.mean(x * x, axis=0, keepdims=True)
    y = (x * lax.rsqrt(ms + EPS)) * g_ref[...]
    o_ref[...] = (y * (1.0 + sc_ref[...]) + sh_ref[...]).astype(bf16)


def prenorm(xT, g_col, scale_col, shift_col):
    tm = 512
    col = pl.BlockSpec((D_MODEL, 1), lambda i: (0, 0))
    return pl.pallas_call(
        _prenorm_kernel,
        grid=(SEQ // tm,),
        in_specs=[pl.BlockSpec((D_MODEL, tm), lambda i: (0, i)), col, col, col],
        out_specs=pl.BlockSpec((D_MODEL, tm), lambda i: (0, i)),
        out_shape=jax.ShapeDtypeStruct((D_MODEL, SEQ), bf16),
        compiler_params=_params("arbitrary"),
        name="prenorm",
    )(xT, g_col, scale_col, shift_col)


def _proj_kernel(*refs, kind, tn, slab, half, scale, scale_lo, scale_hi, token_major):
    if kind == "rope":
        h_ref, w_ref, cos_ref, sin_ref, o_ref = refs
    else:
        h_ref, w_ref, o_ref = refs
    acc = jnp.dot(w_ref[...], h_ref[...], preferred_element_type=f32)
    if scale is not None:
        j = pl.program_id(1)
        acc = acc * jnp.where((j >= scale_lo) & (j < scale_hi), scale, 1.0).astype(f32)
    if kind == "silu":
        acc = acc * jax.nn.sigmoid(acc)
    elif kind == "sigmoid":
        acc = jax.nn.sigmoid(acc)

    if kind == "rope":
        cos = cos_ref[...]
        sin = sin_ref[...]
        pieces = []
        for s in range(tn // slab):
            b = s * slab
            x1 = acc[b:b + half]
            x2 = acc[b + half:b + 2 * half]
            pieces.append(x1 * cos - x2 * sin)
            pieces.append(x1 * sin + x2 * cos)
            if 2 * half < slab:
                pieces.append(acc[b + 2 * half:b + slab])
        acc = jnp.concatenate(pieces, axis=0)
    if token_major:
        for s in range(tn // HEAD_DIM):
            o_ref[s] = acc[s * HEAD_DIM:(s + 1) * HEAD_DIM].T.astype(o_ref.dtype)
    else:
        o_ref[...] = acc.astype(o_ref.dtype)


def project(hT, wT, kind, *, tables=None, slab=HEAD_DIM, half=ROPE_DIM // 2, scale=None,
            scale_rows=(0, 0), token_major=False, out_dtype=bf16, tn=512, name="proj"):
    n = wT.shape[0]
    tm = 1024
    tn = min(tn, n)
    assert n % tn == 0 and scale_rows[0] % tn == 0 and scale_rows[1] % tn == 0
    assert kind != "rope" or tn % slab == 0
    in_specs = [pl.BlockSpec((D_MODEL, tm), lambda i, j: (0, i)),
                pl.BlockSpec((tn, D_MODEL), lambda i, j: (j, 0))]
    args = [hT, wT]
    if kind == "rope":
        in_specs += [pl.BlockSpec((half, tm), lambda i, j: (0, i))] * 2
        args += list(tables)
    if token_major:
        out_spec = pl.BlockSpec((tn // HEAD_DIM, tm, HEAD_DIM), lambda i, j: (j, i, 0))
        out_shape = jax.ShapeDtypeStruct((n // HEAD_DIM, SEQ, HEAD_DIM), out_dtype)
    else:
        out_spec = pl.BlockSpec((tn, tm), lambda i, j: (j, i))
        out_shape = jax.ShapeDtypeStruct((n, SEQ), out_dtype)
    body = functools.partial(
        _proj_kernel, kind=kind, tn=tn, slab=slab, half=half, scale=scale,
        scale_lo=scale_rows[0] // tn, scale_hi=scale_rows[1] // tn, token_major=token_major)
    return pl.pallas_call(
        body, grid=(SEQ // tm, n // tn), in_specs=in_specs, out_specs=out_spec,
        out_shape=out_shape, compiler_params=_params("arbitrary", "arbitrary"), name=name,
    )(*args)


def _outproj_kernel(a_ref, w_ref, x_ref, gate_ref, g_ref, o_ref):
    y = jnp.dot(w_ref[...], a_ref[...], preferred_element_type=f32)
    ms = jnp.mean(y * y, axis=0, keepdims=True)
    yn = (y * lax.rsqrt(ms + EPS)) * g_ref[...]
    o_ref[...] = x_ref[...] + gate_ref[...] * yn


def outproj_residual(aT, w_outT, xT, gate_col, g_col):
    k = aT.shape[0]
    tm = 256
    col = pl.BlockSpec((D_MODEL, 1), lambda i: (0, 0))
    return pl.pallas_call(
        _outproj_kernel,
        grid=(SEQ // tm,),
        in_specs=[pl.BlockSpec((k, tm), lambda i: (0, i)),
                  pl.BlockSpec((D_MODEL, k), lambda i: (0, 0)),
                  pl.BlockSpec((D_MODEL, tm), lambda i: (0, i)), col, col],
        out_specs=pl.BlockSpec((D_MODEL, tm), lambda i: (0, i)),
        out_shape=jax.ShapeDtypeStruct((D_MODEL, SEQ), f32),
        compiler_params=_params("arbitrary"),
        name="outproj_residual",
    )(aT, w_outT, xT, gate_col, g_col)


def _first_k_select(val, idx, n_rows, k, slot_ok):
    sel = jnp.zeros(val.shape, f32)
    for t in range(k):
        m = jnp.max(val, axis=0, keepdims=True)
        first = jnp.min(jnp.where(val == m, idx, n_rows), axis=0, keepdims=True)
        pick = idx == first
        sel = jnp.where(pick, slot_ok(t, m), sel)
        val = jnp.where(pick, -jnp.inf, val)
    return sel


def _flash_step(s, va, m, acc):
    mn = jnp.maximum(m, jnp.max(s, axis=0, keepdims=True))
    p = jnp.exp(s - mn).astype(bf16)
    return mn, jnp.exp(m - mn) * acc + jnp.dot(va, p, preferred_element_type=f32)


def _moba_attn_kernel(q_ref, k_ref, v_ref, z_ref, avg_ref, e_ref, o_ref,
                      kmean_ref, vaug_ref, qaug_ref):
    blk = MOBA_BLOCK
    tq = MOBA_TQ
    sub = tq // blk
    i = pl.program_id(1)
    b0 = i * sub

    @pl.when(i == 0)
    def _():
        kmean_ref[...] = jnp.dot(avg_ref[...], k_ref[0], preferred_element_type=f32)
        vaug_ref[0:HEAD_DIM, :] = v_ref[...]
        vaug_ref[HEAD_DIM:, :] = jnp.ones((ONES_ROWS, SEQ), bf16)

    qT = q_ref[...]
    gate = jnp.dot(kmean_ref[...], qT.astype(f32), precision=lax.Precision.HIGHEST,
                   preferred_element_type=f32)
    nidx = lax.broadcasted_iota(i32, (MOBA_NB, tq), 0)
    own = b0 + jnp.right_shift(lax.broadcasted_iota(i32, (1, tq), 1), blk.bit_length() - 1)
    gate = jnp.where(nidx < own, gate, NEG)
    sel = _first_k_select(gate, nidx, MOBA_NB, MOBA_TOPK, lambda t, m: jnp.where(t < own, 1.0, 0.0))
    qaug_ref[0:HEAD_DIM, :] = qT
    qaug_ref[HEAD_DIM:HEAD_DIM + MOBA_NB, :] = jnp.where(sel > 0.5, 0.0, NEG).astype(bf16)
    qaug_ref[HEAD_DIM + MOBA_NB:, :] = jnp.zeros((HEAD_DIM - MOBA_NB, tq), bf16)

    offs = [pl.multiple_of((b0 + c) * blk, blk) for c in range(sub)]
    s_own = jnp.concatenate(
        [jnp.dot(k_ref[0, pl.ds(offs[c], blk), :], qT[:, c * blk:(c + 1) * blk],
                 preferred_element_type=f32) for c in range(sub)], axis=1)
    krow = lax.broadcasted_iota(i32, (blk, tq), 0)
    qcol = lax.broadcasted_iota(i32, (blk, tq), 1) & (blk - 1)
    s_own = jnp.where(krow <= qcol, s_own, NEG)
    m = jnp.max(s_own, axis=0, keepdims=True)
    p = jnp.exp(s_own - m).astype(bf16)
    acc = jnp.concatenate(
        [jnp.dot(vaug_ref[:, pl.ds(offs[c], blk)], p[:, c * blk:(c + 1) * blk],
                 preferred_element_type=f32) for c in range(sub)], axis=1)

    def body(n, carry):
        off = pl.multiple_of(n * tq, tq)
        kaug = jnp.concatenate([k_ref[0, pl.ds(off, tq), :], e_ref[pl.ds(off, tq), :]], axis=1)
        s = jnp.dot(kaug, qaug_ref[...], preferred_element_type=f32)
        return _flash_step(s, vaug_ref[:, pl.ds(off, tq)], *carry)

    _, acc = lax.fori_loop(0, i + 1, body, (m, acc))
    o = acc[0:HEAD_DIM] / jnp.maximum(acc[HEAD_DIM:HEAD_DIM + 1], 1e-30)
    o_ref[...] = (o * z_ref[...].astype(f32)).astype(bf16)


def moba_attention(qT, k_tok, vT, zT):
    tq = MOBA_TQ
    h = MOBA_HEADS
    blk_of = jnp.arange(SEQ) // MOBA_BLOCK
    avg = ((blk_of[None, :] == jnp.arange(MOBA_NB)[:, None]).astype(f32) / MOBA_BLOCK).astype(bf16)
    onehot = (blk_of[:, None] == jnp.arange(HEAD_DIM)[None, :]).astype(bf16)
    return pl.pallas_call(
        _moba_attn_kernel,
        grid=(h, SEQ // tq),
        in_specs=[pl.BlockSpec((HEAD_DIM, tq), lambda hh, i: (hh, i)),
                  pl.BlockSpec((1, SEQ, HEAD_DIM), lambda hh, i: (hh, 0, 0)),
                  pl.BlockSpec((HEAD_DIM, SEQ), lambda hh, i: (hh, 0)),
                  pl.BlockSpec((HEAD_DIM, tq), lambda hh, i: (hh, i)),
                  pl.BlockSpec((MOBA_NB, SEQ), lambda hh, i: (0, 0)),
                  pl.BlockSpec((SEQ, HEAD_DIM), lambda hh, i: (0, 0))],
        out_specs=pl.BlockSpec((HEAD_DIM, tq), lambda hh, i: (hh, i)),
        out_shape=jax.ShapeDtypeStruct((h * HEAD_DIM, SEQ), bf16),
        scratch_shapes=[pltpu.VMEM((MOBA_NB, HEAD_DIM), f32),
                        pltpu.VMEM((HEAD_DIM + ONES_ROWS, SEQ), bf16),
                        pltpu.VMEM((2 * HEAD_DIM, tq), bf16)],
        compiler_params=_params("arbitrary", "arbitrary"),
        name="moba_attention",
    )(qT, k_tok, vT, zT, avg, onehot)


def _nsa_cmp_kernel(x_ref, wab_ref, pe_ref, w1_ref, w2_ref, o_ref, *, transpose_out):
    hid = NSA_CMP_HIDDEN
    ab = jnp.dot(x_ref[0], wab_ref[...], preferred_element_type=f32)
    first = ab[:, :hid]
    second = pltpu.roll(ab[:, hid:], NSA_NC_PAD - 1, 0)
    pe8 = jnp.broadcast_to(pe_ref[...], (8, NSA_CMP_LEN * HEAD_DIM)).astype(bf16)
    bias = jnp.dot(pe8, w1_ref[...], preferred_element_type=f32)[0:1]
    hpre = first + second + bias
    out = jnp.dot((hpre * jax.nn.sigmoid(hpre)).astype(bf16), w2_ref[...],
                  preferred_element_type=f32)
    o_ref[0] = (out.T if transpose_out else out).astype(o_ref.dtype)


def nsa_compress(t_tok, pe, w1, w2, transpose_out):
    g = NSA_KV_GROUPS
    half_k = NSA_CMP_STRIDE * HEAD_DIM
    x = t_tok.reshape(g, NSA_NC_PAD, half_k)
    w1b = w1.astype(bf16)
    wab = jnp.concatenate([w1b[:half_k], w1b[half_k:]], axis=1)
    oshape = (g, HEAD_DIM, NSA_NC_PAD) if transpose_out else (g, NSA_NC_PAD, HEAD_DIM)
    return pl.pallas_call(
        functools.partial(_nsa_cmp_kernel, transpose_out=transpose_out),
        grid=(g,),
        in_specs=[pl.BlockSpec((1, NSA_NC_PAD, half_k), lambda gg: (gg, 0, 0)),
                  pl.BlockSpec((half_k, 2 * NSA_CMP_HIDDEN), lambda gg: (0, 0)),
                  pl.BlockSpec((1, NSA_CMP_LEN * HEAD_DIM), lambda gg: (0, 0)),
                  pl.BlockSpec((NSA_CMP_LEN * HEAD_DIM, NSA_CMP_HIDDEN), lambda gg: (0, 0)),
                  pl.BlockSpec((NSA_CMP_HIDDEN, HEAD_DIM), lambda gg: (0, 0))],
        out_specs=pl.BlockSpec((1,) + oshape[1:], lambda gg: (gg, 0, 0)),
        out_shape=jax.ShapeDtypeStruct(oshape, bf16),
        compiler_params=_params("arbitrary"),
        name="nsa_compress",
    )(x, wab, pe.reshape(1, NSA_CMP_LEN * HEAD_DIM), w1b, w2.astype(bf16))


def _nsa_attn_kernel(q_ref, kc_ref, vc_ref, ov_ref, ks_ref, vs_ref, kw_ref, vw_ref, e_ref,
                     gt_ref, z_ref, o_ref, vsaug_ref, vwaug_ref, qaug_ref):
    tq = NSA_TQ
    rep = NSA_REP
    nq = rep * tq
    i = pl.program_id(1)
    q0 = i * tq

    @pl.when(i == 0)
    def _():
        for src, dst in ((vs_ref, vsaug_ref), (vw_ref, vwaug_ref)):
            dst[0:HEAD_DIM, :] = src[...]
            dst[HEAD_DIM:, :] = jnp.ones((ONES_ROWS, SEQ), bf16)

    qT = jnp.concatenate([q_ref[r * HEAD_DIM:(r + 1) * HEAD_DIM, :] for r in range(rep)],
                         axis=1)
    lane = lax.broadcasted_iota(i32, (1, nq), 1)
    qpos = q0 + (lane & (tq - 1))
    qpos1 = q0 + lax.broadcasted_iota(i32, (1, tq), 1)

    sc = jnp.dot(kc_ref[0], qT, preferred_element_type=f32)
    cend = lax.broadcasted_iota(i32, (NSA_NC_PAD, nq), 0) * NSA_CMP_STRIDE + (NSA_CMP_LEN - 1)
    ok_c = cend <= qpos
    sc = jnp.where(ok_c, sc, NEG)
    pc = jnp.where(ok_c, jnp.exp(sc - jnp.max(sc, axis=0, keepdims=True)), 0.0)
    pc = pc / jnp.maximum(jnp.sum(pc, axis=0, keepdims=True), 1e-30)
    ocT = jnp.dot(vc_ref[0], pc.astype(bf16), preferred_element_type=f32)

    ps = pc[:, 0:tq]
    for r in range(1, rep):
        ps = ps + pc[:, r * tq:(r + 1) * tq]
    ps_hi = ps.astype(bf16)
    ps_lo = (ps - ps_hi.astype(f32)).astype(bf16)
    imp = (jnp.dot(ov_ref[...], ps_hi, preferred_element_type=f32)
           + jnp.dot(ov_ref[...], ps_lo, preferred_element_type=f32))
    sidx = lax.broadcasted_iota(i32, (NSA_NS, tq), 0)
    cur = jnp.right_shift(qpos1, 6)
    valid = sidx * NSA_SLC_BLOCK <= qpos1
    forced = (sidx == 0) | (sidx == cur) | (sidx == cur - 1)
    val = jnp.where(valid, jnp.where(forced, POS_BIG, imp), NEG)
    sel = _first_k_select(val, sidx, NSA_NS, NSA_SLC_TOPK, lambda t, m: jnp.where(m > NEG / 2, 1.0, 0.0))

    bias = jnp.where(sel > 0.5, 0.0, NEG)
    qaug_ref[0:HEAD_DIM, :] = qT
    qaug_ref[HEAD_DIM:, :] = jnp.concatenate([bias] * rep, axis=1).astype(bf16)
    offd = pl.multiple_of(q0, tq)
    kaug = jnp.concatenate([ks_ref[0, pl.ds(offd, tq), :], e_ref[pl.ds(offd, tq), :]], axis=1)
    s = jnp.dot(kaug, qaug_ref[...], preferred_element_type=f32)
    s = jnp.where(q0 + lax.broadcasted_iota(i32, (tq, nq), 0) <= qpos, s, NEG)
    m = jnp.max(s, axis=0, keepdims=True)
    acc = jnp.dot(vsaug_ref[:, pl.ds(offd, tq)], jnp.exp(s - m).astype(bf16),
                  preferred_element_type=f32)

    own_first = i * (tq // NSA_SLC_BLOCK)
    bias_past = jnp.where(sidx >= own_first, NEG, bias)
    qaug_ref[HEAD_DIM:, :] = jnp.concatenate([bias_past] * rep, axis=1).astype(bf16)

    def sel_body(n, carry):
        off = pl.multiple_of(n * NSA_SPAN, NSA_SPAN)
        kaug = jnp.concatenate([ks_ref[0, pl.ds(off, NSA_SPAN), :], e_ref[pl.ds(off, NSA_SPAN), :]],
                               axis=1)
        s = jnp.dot(kaug, qaug_ref[...], preferred_element_type=f32)
        return _flash_step(s, vsaug_ref[:, pl.ds(off, NSA_SPAN)], *carry)

    per_span = NSA_SPAN // tq
    trips = jnp.right_shift(i + per_span - 1, per_span.bit_length() - 1)
    _, acc = lax.fori_loop(0, trips, sel_body, (m, acc))
    osT = acc[0:HEAD_DIM] / jnp.maximum(acc[HEAD_DIM:HEAD_DIM + 1], 1e-30)

    wspan = NSA_WINDOW + tq
    start = pl.multiple_of(jnp.maximum(q0 - NSA_WINDOW, 0), tq)
    sw = jnp.dot(kw_ref[0, pl.ds(start, wspan), :], qT, preferred_element_type=f32)
    kpos = start + lax.broadcasted_iota(i32, (wspan, nq), 0)
    sw = jnp.where((kpos <= qpos) & (kpos > qpos - NSA_WINDOW), sw, NEG)
    pw = jnp.exp(sw - jnp.max(sw, axis=0, keepdims=True)).astype(bf16)
    accw = jnp.dot(vwaug_ref[:, pl.ds(start, wspan)], pw, preferred_element_type=f32)
    owT = accw[0:HEAD_DIM] / jnp.maximum(accw[HEAD_DIM:HEAD_DIM + 1], 1e-30)

    gt = gt_ref[0]
    for r in range(rep):
        cs = slice(r * tq, (r + 1) * tq)
        rs = slice(r * HEAD_DIM, (r + 1) * HEAD_DIM)
        o = (gt[3 * r:3 * r + 1] * ocT[:, cs] + gt[3 * r + 1:3 * r + 2] * osT[:, cs]
             + gt[3 * r + 2:3 * r + 3] * owT[:, cs])
        o_ref[rs, :] = (o * z_ref[rs, :].astype(f32)).astype(bf16)


def _nsa_overlap_t():
    nc = NSA_NC_PAD - 1
    cs = np.arange(nc)[:, None] * NSA_CMP_STRIDE
    ss = np.arange(NSA_NS)[None, :] * NSA_SLC_BLOCK
    ov = np.clip(np.minimum(cs + NSA_CMP_LEN, ss + NSA_SLC_BLOCK) - np.maximum(cs, ss), 0, None)
    ov = np.concatenate([ov / NSA_CMP_LEN, np.zeros((1, NSA_NS))], axis=0)
    return jnp.asarray(ov.T, dtype=bf16)


def nsa_attention(qT, k_tok, plainT, k_cmp, v_cmpT, gatesT, zT):
    tq = NSA_TQ
    g = NSA_KV_GROUPS
    qrows = NSA_REP * HEAD_DIM
    onehot = ((jnp.arange(SEQ) // NSA_SLC_BLOCK)[:, None] == jnp.arange(NSA_NS)[None, :]).astype(bf16)
    keys = lambda first: pl.BlockSpec((1, SEQ, HEAD_DIM), lambda gg, i: (first + gg, 0, 0))
    vals = lambda first: pl.BlockSpec((HEAD_DIM, SEQ), lambda gg, i: (first + gg, 0))
    return pl.pallas_call(
        _nsa_attn_kernel,
        grid=(g, SEQ // tq),
        in_specs=[pl.BlockSpec((qrows, tq), lambda gg, i: (gg, i)),
                  pl.BlockSpec((1, NSA_NC_PAD, HEAD_DIM), lambda gg, i: (gg, 0, 0)),
                  pl.BlockSpec((1, HEAD_DIM, NSA_NC_PAD), lambda gg, i: (gg, 0, 0)),
                  pl.BlockSpec((NSA_NS, NSA_NC_PAD), lambda gg, i: (0, 0)),
                  keys(0), vals(0), keys(g), vals(g),
                  pl.BlockSpec((SEQ, NSA_NS), lambda gg, i: (0, 0)),
                  pl.BlockSpec((1, NSA_GATE_ROWS, tq), lambda gg, i: (gg, 0, i)),
                  pl.BlockSpec((qrows, tq), lambda gg, i: (gg, i))],
        out_specs=pl.BlockSpec((qrows, tq), lambda gg, i: (gg, i)),
        out_shape=jax.ShapeDtypeStruct((NSA_HEADS * HEAD_DIM, SEQ), bf16),
        scratch_shapes=[pltpu.VMEM((HEAD_DIM + ONES_ROWS, SEQ), bf16),
                        pltpu.VMEM((HEAD_DIM + ONES_ROWS, SEQ), bf16),
                        pltpu.VMEM((HEAD_DIM + NSA_NS, NSA_REP * tq), bf16)],
        compiler_params=_params("arbitrary", "arbitrary"),
        name="nsa_attention",
    )(qT, k_cmp, v_cmpT, _nsa_overlap_t(), k_tok, plainT, k_tok, plainT, onehot, gatesT, zT)


def _ret_kernel(q_ref, k_ref, v_ref, g_ref, dt_ref, rd_ref, wd_ref, cd_ref, o_ref, st_ref):
    @pl.when(pl.program_id(1) == 0)
    def _():
        st_ref[...] = jnp.zeros(st_ref.shape, f32)

    qT = q_ref[...]
    kT = k_ref[...]
    vT = v_ref[...]
    innerT = lax.dot_general(kT, qT, TN_DIMS, preferred_element_type=f32) * dt_ref[0]
    st = st_ref[...]
    o = (jnp.dot(vT, innerT.astype(bf16), preferred_element_type=f32)
         + jnp.dot(st.astype(bf16), qT, preferred_element_type=f32) * rd_ref[0])
    kw = (kT.astype(f32) * wd_ref[0]).astype(bf16)
    st_ref[...] = cd_ref[0] * st + lax.dot_general(vT, kw, NT_DIMS, preferred_element_type=f32)
    mu = jnp.mean(o, axis=0, keepdims=True)
    d = o - mu
    var = jnp.mean(d * d, axis=0, keepdims=True)
    o_ref[...] = (g_ref[...].astype(f32) * (d * lax.rsqrt(var + EPS))).astype(bf16)


def retention(qkT, vT, gT):
    c = RET_CHUNK
    h = RET_HEADS
    log_gamma = jnp.log(1.0 - 2.0 ** (-5.0 - jnp.arange(h, dtype=f32)))
    i = jnp.arange(c, dtype=f32)
    diff = i[None, :] - i[:, None]
    dt = jnp.where(diff >= 0, jnp.exp(log_gamma[:, None, None] * jnp.maximum(diff, 0.0)), 0.0)
    rd = jnp.exp(log_gamma[:, None] * (i + 1.0))[:, None, :]
    wd = jnp.exp(log_gamma[:, None] * (c - 1.0 - i))[:, None, :]
    cd = jnp.broadcast_to(jnp.exp(log_gamma * c)[:, None, None], (h, 1, RET_QK_DIM))
    vec = lambda n: pl.BlockSpec((1, 1, n), lambda hh, t: (hh, 0, 0))
    return pl.pallas_call(
        _ret_kernel,
        grid=(h, SEQ // c),
        in_specs=[pl.BlockSpec((RET_QK_DIM, c), lambda hh, t: (hh, t)),
                  pl.BlockSpec((RET_QK_DIM, c), lambda hh, t: (h + hh, t)),
                  pl.BlockSpec((RET_V_DIM, c), lambda hh, t: (hh, t)),
                  pl.BlockSpec((RET_V_DIM, c), lambda hh, t: (hh, t)),
                  pl.BlockSpec((1, c, c), lambda hh, t: (hh, 0, 0)),
                  vec(c), vec(c), vec(RET_QK_DIM)],
        out_specs=pl.BlockSpec((RET_V_DIM, c), lambda hh, t: (hh, t)),
        out_shape=jax.ShapeDtypeStruct((h * RET_V_DIM, SEQ), bf16),
        scratch_shapes=[pltpu.VMEM((RET_V_DIM, RET_QK_DIM), f32)],
        compiler_params=_params("arbitrary", "arbitrary"),
        name="retention",
    )(qkT, qkT, vT, gT, dt, rd, wd, cd)


def _t(w):
    return w.T.astype(bf16)


def moba_mixer(hT, tabs16, w_in, w_out):
    hd = MOBA_HEADS * HEAD_DIM
    qT = project(hT, _t(w_in[:, :hd]), "rope", tables=tabs16, scale=HEAD_DIM ** -0.5,
                 scale_rows=(0, hd), name="moba_proj_q")
    k_tok = project(hT, _t(w_in[:, hd:2 * hd]), "rope", tables=tabs16, token_major=True,
                    name="moba_proj_k")
    vT = project(hT, _t(w_in[:, 2 * hd:3 * hd]), "plain", name="moba_proj_v")
    zT = project(hT, _t(w_in[:, 3 * hd:]), "silu", name="moba_proj_z")
    return moba_attention(qT, k_tok, vT, zT), _t(w_out)


def nsa_mixer(hT, tabs16, w_in, w_out, pe, wk1, wk2, wv1, wv2):
    hd = NSA_HEADS * HEAD_DIM
    gd = NSA_KV_GROUPS * HEAD_DIM
    seg = lambda k: w_in[:, hd + k * gd:hd + (k + 1) * gd]
    qT = project(hT, _t(w_in[:, :hd]), "rope", tables=tabs16, scale=HEAD_DIM ** -0.5,
                 scale_rows=(0, hd), name="nsa_proj_q")
    k_tok = project(hT, _t(jnp.concatenate([seg(2), seg(4)], axis=1)), "rope", tables=tabs16,
                    token_major=True, name="nsa_proj_k")
    plainT = project(hT, _t(jnp.concatenate([seg(3), seg(5)], axis=1)), "plain", name="nsa_proj_v")
    kc_tok = project(hT, _t(seg(0)), "rope", tables=tabs16, token_major=True, name="nsa_proj_kc")
    vc_tok = project(hT, _t(seg(1)), "plain", token_major=True, name="nsa_proj_vc")
    g0 = hd + 6 * gd
    per = NSA_REP * 3
    wg = w_in[:, g0:g0 + NSA_KV_GROUPS * per].reshape(D_MODEL, NSA_KV_GROUPS, per)
    wg = jnp.pad(wg, ((0, 0), (0, 0), (0, NSA_GATE_ROWS - per))).reshape(D_MODEL, -1)
    gatesT = project(hT, _t(wg), "sigmoid", out_dtype=f32, name="nsa_proj_gate")
    gatesT = gatesT.reshape(NSA_KV_GROUPS, NSA_GATE_ROWS, SEQ)
    zT = project(hT, _t(w_in[:, g0 + NSA_HEADS * 3:]), "silu", name="nsa_proj_z")
    k_cmp = nsa_compress(kc_tok, pe, wk1, wk2, transpose_out=False)
    v_cmpT = nsa_compress(vc_tok, pe, wv1, wv2, transpose_out=True)
    return nsa_attention(qT, k_tok, plainT, k_cmp, v_cmpT, gatesT, zT), _t(w_out)


def retention_mixer(hT, tabs128, w_in, w_out):
    qk = 2 * RET_HEADS * RET_QK_DIM
    vd = RET_HEADS * RET_V_DIM
    qkT = project(hT, _t(w_in[:, :qk]), "rope", tables=tabs128, slab=RET_QK_DIM,
                  half=RET_QK_DIM // 2, scale=RET_QK_DIM ** -0.5, scale_rows=(qk // 2, qk),
                  name="ret_proj_qk")
    vT = project(hT, _t(w_in[:, qk:qk + vd]), "plain", name="ret_proj_v")
    gT = project(hT, _t(w_in[:, qk + vd:]), "silu", name="ret_proj_g")
    return retention(qkT, vT, gT), _t(w_out)


def kernel(x, c, positions, ada_w, ada_b, norm_pre, norm_post, moba_w_in, moba_w_out, nsa_w_in, nsa_w_out, nsa_cmp_pe, nsa_cmp_wk1, nsa_cmp_wk2, nsa_cmp_wv1, nsa_cmp_wv2, ret_w_in, ret_w_out):
    assert x.shape == (1, SEQ, D_MODEL)
    mod = ada_modulation(c, ada_w, ada_b)
    tabs16 = rope_tables(positions, ROPE_DIM // 2, ROPE_THETA)
    tabs128 = rope_tables(positions, RET_QK_DIM // 2, RET_ROT_THETA)
    col = lambda v: v.reshape(D_MODEL, 1)
    xT = x[0].T
    for i in range(DEPTH):
        kind, j = i % N_MIXERS, i // N_MIXERS
        shift, scale, gate = mod[i, :D_MODEL], mod[i, D_MODEL:2 * D_MODEL], mod[i, 2 * D_MODEL:]
        hT = prenorm(xT, col(norm_pre[i]), col(scale), col(shift))
        if kind == 0:
            aT, w_outT = moba_mixer(hT, tabs16, moba_w_in[j], moba_w_out[j])
        elif kind == 1:
            aT, w_outT = nsa_mixer(hT, tabs16, nsa_w_in[j], nsa_w_out[j], nsa_cmp_pe[j],
                                   nsa_cmp_wk1[j], nsa_cmp_wk2[j], nsa_cmp_wv1[j], nsa_cmp_wv2[j])
        else:
            aT, w_outT = retention_mixer(hT, tabs128, ret_w_in[j], ret_w_out[j])
        xT = outproj_residual(aT, w_outT, xT, col(gate), col(norm_post[i]))
    return xT.T[None]
```

```python
import functools

import numpy as np
import jax
import jax.numpy as jnp
from jax import lax
from jax.experimental import pallas as pl
from jax.experimental.pallas import tpu as pltpu

f32 = jnp.float32
bf16 = jnp.bfloat16
i32 = jnp.int32

D_MODEL = 2048
SEQ = 8192
DEPTH = 4
N_MIXERS = 3
HEAD_DIM = 128
ROPE_THETA = 500000.0
ROPE_DIM = HEAD_DIM // 4
EPS = 1e-6
NEG = -1e30
POS_BIG = 1e30

MOBA_HEADS = D_MODEL // HEAD_DIM
MOBA_BLOCK = 256
MOBA_TOPK = 3
MOBA_NB = SEQ // MOBA_BLOCK
MOBA_TQ = 1024
ONES_ROWS = 16
PROJ_TN = 512
COL_GROUP = 256

NSA_HEADS = D_MODEL // HEAD_DIM
NSA_KV_GROUPS = NSA_HEADS // 4
NSA_REP = NSA_HEADS // NSA_KV_GROUPS
NSA_CMP_LEN = 32
NSA_CMP_STRIDE = 16
NSA_CMP_HIDDEN = 256
NSA_SLC_BLOCK = 64
NSA_SLC_TOPK = 16
NSA_WINDOW = 512
NSA_NC_PAD = SEQ // NSA_CMP_STRIDE
NSA_NS = SEQ // NSA_SLC_BLOCK
NSA_GATE_ROWS = 16
NSA_TQ = 256
NSA_SPAN = 1024

RET_HEADS = 8
RET_QK_DIM = D_MODEL // RET_HEADS
RET_V_DIM = 2 * D_MODEL // RET_HEADS
RET_CHUNK = 256
RET_ROT_THETA = 10000.0

VMEM_LIMIT = 56 * 1024 * 1024

TN_DIMS = (((0,), (0,)), ((), ()))
NT_DIMS = (((1,), (1,)), ((), ()))


def _params(*sem):
    return pltpu.CompilerParams(dimension_semantics=sem, vmem_limit_bytes=VMEM_LIMIT)


def _rope_table_kernel(pos_ref, inv_ref, cos_ref, sin_ref):
    ang = pos_ref[...].astype(f32) * inv_ref[...]
    cos_ref[...] = jnp.cos(ang)
    sin_ref[...] = jnp.sin(ang)


def rope_tables(positions, half, theta):
    inv = (theta ** (-jnp.arange(half, dtype=f32) / half)).reshape(half, 1)
    ts = 1024
    return pl.pallas_call(
        _rope_table_kernel,
        grid=(SEQ // ts,),
        in_specs=[pl.BlockSpec((1, ts), lambda i: (0, i)),
                  pl.BlockSpec((half, 1), lambda i: (0, 0))],
        out_specs=[pl.BlockSpec((half, ts), lambda i: (0, i))] * 2,
        out_shape=[jax.ShapeDtypeStruct((half, SEQ), f32)] * 2,
        name=f"rope_tables_{half}",
    )(positions, inv)


def _ada_kernel(c_ref, w_ref, b_ref, o_ref):
    sc = jax.nn.silu(c_ref[...])
    sc8 = jnp.broadcast_to(sc, (8, D_MODEL)).astype(bf16)
    acc = jnp.dot(sc8, w_ref[0].astype(bf16), preferred_element_type=f32)
    o_ref[0] = acc[0:1] + b_ref[0]


def ada_modulation(c, ada_w, ada_b):
    tn = 1024
    n3 = 3 * D_MODEL
    out = pl.pallas_call(
        _ada_kernel,
        grid=(DEPTH, n3 // tn),
        in_specs=[pl.BlockSpec((1, D_MODEL), lambda l, j: (0, 0)),
                  pl.BlockSpec((1, D_MODEL, tn), lambda l, j: (l, 0, j)),
                  pl.BlockSpec((1, 1, tn), lambda l, j: (l, 0, j))],
        out_specs=pl.BlockSpec((1, 1, tn), lambda l, j: (l, 0, j)),
        out_shape=jax.ShapeDtypeStruct((DEPTH, 1, n3), f32),
        compiler_params=_params("arbitrary", "arbitrary"),
        name="ada_modulation",
    )(c, ada_w, ada_b.reshape(DEPTH, 1, n3))
    return out.reshape(DEPTH, n3)


def _prenorm_kernel(x_ref, g_ref, sc_ref, sh_ref, o_ref):
    x = x_ref[...]
    ms = jnp.mean(x * x, axis=0, keepdims=True)
    y = (x * lax.rsqrt(ms + EPS)) * g_ref[...]
    o_ref[...] = (y * (1.0 + sc_ref[...]) + sh_ref[...]).astype(bf16)


def prenorm(xT, g_col, scale_col, shift_col):
    tm = 512
    col = pl.BlockSpec((D_MODEL, 1), lambda i: (0, 0))
    return pl.pallas_call(
        _prenorm_kernel,
        grid=(SEQ // tm,),
        in_specs=[pl.BlockSpec((D_MODEL, tm), lambda i: (0, i)), col, col, col],
        out_specs=pl.BlockSpec((D_MODEL, tm), lambda i: (0, i)),
        out_shape=jax.ShapeDtypeStruct((D_MODEL, SEQ), bf16),
        compiler_params=_params("arbitrary"),
        name="prenorm",
    )(xT, g_col, scale_col, shift_col)


def _proj_kernel(*refs, kind, tn, slab, half, scale, scale_lo, scale_hi, token_major):
    if kind == "rope":
        h_ref, w_ref, cos_ref, sin_ref, o_ref = refs
    else:
        h_ref, w_ref, o_ref = refs
    acc = lax.dot_general(w_ref[...].astype(bf16), h_ref[...], TN_DIMS,
                          preferred_element_type=f32)
    if scale is not None:
        j = pl.program_id(1)
        acc = acc * jnp.where((j >= scale_lo) & (j < scale_hi), scale, 1.0).astype(f32)
    if kind == "silu":
        acc = acc * jax.nn.sigmoid(acc)
    elif kind == "sigmoid":
        acc = jax.nn.sigmoid(acc)

    if kind == "rope":
        cos = cos_ref[...]
        sin = sin_ref[...]
        pieces = []
        for s in range(tn // slab):
            b = s * slab
            x1 = acc[b:b + half]
            x2 = acc[b + half:b + 2 * half]
            pieces.append(x1 * cos - x2 * sin)
            pieces.append(x1 * sin + x2 * cos)
            if 2 * half < slab:
                pieces.append(acc[b + 2 * half:b + slab])
        acc = jnp.concatenate(pieces, axis=0)
    if token_major:
        for s in range(tn // HEAD_DIM):
            o_ref[s] = acc[s * HEAD_DIM:(s + 1) * HEAD_DIM].T.astype(o_ref.dtype)
    else:
        o_ref[...] = acc.astype(o_ref.dtype)


def project(hT, w, cols, kind, *, layer=None, col_step=1, tables=None, slab=HEAD_DIM,
            half=ROPE_DIM // 2, scale=None, scale_rows=(0, 0), token_major=False,
            out_dtype=bf16, tn=PROJ_TN, name="proj"):
    start, n = cols
    tm = 1024
    tn = min(tn, n)
    first = start // tn
    assert n % tn == 0 and start % tn == 0
    assert (first + col_step * (n // tn - 1) + 1) * tn <= w.shape[-1]
    assert scale_rows[0] % tn == 0 and scale_rows[1] % tn == 0
    assert kind != "rope" or tn % slab == 0
    if layer is None:
        w_spec = pl.BlockSpec((D_MODEL, tn), lambda i, j: (0, first + col_step * j))
    else:
        w_spec = pl.BlockSpec((None, D_MODEL, tn), lambda i, j: (layer, 0, first + col_step * j))
    in_specs = [pl.BlockSpec((D_MODEL, tm), lambda i, j: (0, i)), w_spec]
    args = [hT, w]
    if kind == "rope":
        in_specs += [pl.BlockSpec((half, tm), lambda i, j: (0, i))] * 2
        args += list(tables)
    if token_major:
        out_spec = pl.BlockSpec((tn // HEAD_DIM, tm, HEAD_DIM), lambda i, j: (j, i, 0))
        out_shape = jax.ShapeDtypeStruct((n // HEAD_DIM, SEQ, HEAD_DIM), out_dtype)
    else:
        out_spec = pl.BlockSpec((tn, tm), lambda i, j: (j, i))
        out_shape = jax.ShapeDtypeStruct((n, SEQ), out_dtype)
    body = functools.partial(
        _proj_kernel, kind=kind, tn=tn, slab=slab, half=half, scale=scale,
        scale_lo=scale_rows[0] // tn, scale_hi=scale_rows[1] // tn, token_major=token_major)
    return pl.pallas_call(
        body, grid=(SEQ // tm, n // tn), in_specs=in_specs, out_specs=out_spec,
        out_shape=out_shape, compiler_params=_params("arbitrary", "arbitrary"), name=name,
    )(*args)


def _outproj_kernel(a_ref, w_ref, x_ref, gate_ref, g_ref, o_ref):
    y = jnp.dot(w_ref[...], a_ref[...], preferred_element_type=f32)
    ms = jnp.mean(y * y, axis=0, keepdims=True)
    yn = (y * lax.rsqrt(ms + EPS)) * g_ref[...]
    o_ref[...] = x_ref[...] + gate_ref[...] * yn


def outproj_residual(aT, w_outT, xT, gate_col, g_col):
    k = aT.shape[0]
    tm = 256
    col = pl.BlockSpec((D_MODEL, 1), lambda i: (0, 0))
    return pl.pallas_call(
        _outproj_kernel,
        grid=(SEQ // tm,),
        in_specs=[pl.BlockSpec((k, tm), lambda i: (0, i)),
                  pl.BlockSpec((D_MODEL, k), lambda i: (0, 0)),
                  pl.BlockSpec((D_MODEL, tm), lambda i: (0, i)), col, col],
        out_specs=pl.BlockSpec((D_MODEL, tm), lambda i: (0, i)),
        out_shape=jax.ShapeDtypeStruct((D_MODEL, SEQ), f32),
        compiler_params=_params("arbitrary"),
        name="outproj_residual",
    )(aT, w_outT, xT, gate_col, g_col)


def _first_k_select(val, idx, n_rows, k, slot_ok):
    sel = jnp.zeros(val.shape, f32)
    for t in range(k):
        m = jnp.max(val, axis=0, keepdims=True)
        first = jnp.min(jnp.where(val == m, idx, n_rows), axis=0, keepdims=True)
        pick = idx == first
        sel = jnp.where(pick, slot_ok(t, m), sel)
        val = jnp.where(pick, -jnp.inf, val)
    return sel


def _probs(s, m):
    return jnp.exp((s - m).astype(bf16))


def _flash_step(s, va, m, acc):
    mn = jnp.maximum(m, jnp.max(s, axis=0, keepdims=True))
    return mn, jnp.exp(m - mn) * acc + jnp.dot(va, _probs(s, mn), preferred_element_type=f32)


def _flash_step_ref(s_ref, va, m, acc, group=COL_GROUP):
    mns, ps = [], []
    for c in range(s_ref.shape[1] // group):
        cs = slice(c * group, (c + 1) * group)
        s = s_ref[:, cs]
        mn = jnp.maximum(m[:, cs], jnp.max(s, axis=0, keepdims=True))
        mns.append(mn)
        ps.append(_probs(s, mn))
    mn = jnp.concatenate(mns, axis=1)
    p = jnp.concatenate(ps, axis=1)
    return mn, jnp.exp(m - mn) * acc + jnp.dot(va, p, preferred_element_type=f32)


def _flash_pairs(k_ref, e_ref, vaug_ref, qaug_ref, sa_ref, sb_ref, pairs, carry):
    span = sa_ref.shape[0]
    last = 2 * pairs - 1

    def scores(n):
        off = pl.multiple_of(n * span, span)
        kaug = jnp.concatenate([k_ref[0, pl.ds(off, span), :], e_ref[pl.ds(off, span), :]], axis=1)
        return jnp.dot(kaug, qaug_ref[...], preferred_element_type=f32)

    def values(n):
        return vaug_ref[:, pl.ds(pl.multiple_of(n * span, span), span)]

    sa_ref[...] = scores(0)

    def body(j, carry):
        n = 2 * j
        sb_ref[...] = scores(n + 1)
        carry = _flash_step_ref(sa_ref, values(n), *carry)
        sa_ref[...] = scores(jnp.minimum(n + 2, last))
        return _flash_step_ref(sb_ref, values(n + 1), *carry)

    return lax.fori_loop(0, pairs, body, carry)


def _moba_attn_kernel(q_ref, k_ref, v_ref, z_ref, avg_ref, e_ref, o_ref,
                      kmean_ref, vaug_ref, qaug_ref, sa_ref, sb_ref):
    blk = MOBA_BLOCK
    tq = MOBA_TQ
    sub = tq // blk
    i = pl.program_id(1)
    b0 = i * sub

    @pl.when(i == 0)
    def _():
        kmean_ref[...] = jnp.dot(avg_ref[...], k_ref[0], preferred_element_type=f32)
        vaug_ref[0:HEAD_DIM, :] = v_ref[...]
        vaug_ref[HEAD_DIM:, :] = jnp.ones((ONES_ROWS, SEQ), bf16)

    qT = q_ref[...]
    gate = jnp.dot(kmean_ref[...], qT.astype(f32), precision=lax.Precision.HIGHEST,
                   preferred_element_type=f32)
    nidx = lax.broadcasted_iota(i32, (MOBA_NB, tq), 0)
    own = b0 + jnp.right_shift(lax.broadcasted_iota(i32, (1, tq), 1), blk.bit_length() - 1)
    gate = jnp.where(nidx < own, gate, NEG)
    sel = _first_k_select(gate, nidx, MOBA_NB, MOBA_TOPK, lambda t, m: jnp.where(t < own, 1.0, 0.0))
    qaug_ref[0:HEAD_DIM, :] = qT
    qaug_ref[HEAD_DIM:HEAD_DIM + MOBA_NB, :] = jnp.where(sel > 0.5, 0.0, NEG).astype(bf16)
    qaug_ref[HEAD_DIM + MOBA_NB:, :] = jnp.zeros((HEAD_DIM - MOBA_NB, tq), bf16)

    offs = [pl.multiple_of((b0 + c) * blk, blk) for c in range(sub)]
    s_own = jnp.concatenate(
        [jnp.dot(k_ref[0, pl.ds(offs[c], blk), :], qT[:, c * blk:(c + 1) * blk],
                 preferred_element_type=f32) for c in range(sub)], axis=1)
    krow = lax.broadcasted_iota(i32, (blk, tq), 0)
    qcol = lax.broadcasted_iota(i32, (blk, tq), 1) & (blk - 1)
    s_own = jnp.where(krow <= qcol, s_own, NEG)
    m = jnp.max(s_own, axis=0, keepdims=True)
    p = _probs(s_own, m)
    acc = jnp.concatenate(
        [jnp.dot(vaug_ref[:, pl.ds(offs[c], blk)], p[:, c * blk:(c + 1) * blk],
                 preferred_element_type=f32) for c in range(sub)], axis=1)

    _, acc = _flash_pairs(k_ref, e_ref, vaug_ref, qaug_ref, sa_ref, sb_ref, i + 1, (m, acc))
    o = acc[0:HEAD_DIM] / jnp.maximum(acc[HEAD_DIM:HEAD_DIM + 1], 1e-30)
    o_ref[...] = (o * z_ref[...].astype(f32)).astype(bf16)


def moba_attention(qT, k_tok, vT, zT):
    tq = MOBA_TQ
    h = MOBA_HEADS
    blk_of = jnp.arange(SEQ) // MOBA_BLOCK
    avg = ((blk_of[None, :] == jnp.arange(MOBA_NB)[:, None]).astype(f32) / MOBA_BLOCK).astype(bf16)
    onehot = (blk_of[:, None] == jnp.arange(HEAD_DIM)[None, :]).astype(bf16)
    return pl.pallas_call(
        _moba_attn_kernel,
        grid=(h, SEQ // tq),
        in_specs=[pl.BlockSpec((HEAD_DIM, tq), lambda hh, i: (hh, i)),
                  pl.BlockSpec((1, SEQ, HEAD_DIM), lambda hh, i: (hh, 0, 0)),
                  pl.BlockSpec((HEAD_DIM, SEQ), lambda hh, i: (hh, 0)),
                  pl.BlockSpec((HEAD_DIM, tq), lambda hh, i: (hh, i)),
                  pl.BlockSpec((MOBA_NB, SEQ), lambda hh, i: (0, 0)),
                  pl.BlockSpec((SEQ, HEAD_DIM), lambda hh, i: (0, 0))],
        out_specs=pl.BlockSpec((HEAD_DIM, tq), lambda hh, i: (hh, i)),
        out_shape=jax.ShapeDtypeStruct((h * HEAD_DIM, SEQ), bf16),
        scratch_shapes=[pltpu.VMEM((MOBA_NB, HEAD_DIM), f32),
                        pltpu.VMEM((HEAD_DIM + ONES_ROWS, SEQ), bf16),
                        pltpu.VMEM((2 * HEAD_DIM, tq), bf16),
                        pltpu.VMEM((tq // 2, tq), f32),
                        pltpu.VMEM((tq // 2, tq), f32)],
        compiler_params=_params("arbitrary", "arbitrary"),
        name="moba_attention",
    )(qT, k_tok, vT, zT, avg, onehot)


def _nsa_cmp_kernel(x_ref, wab_ref, pe_ref, w1_ref, w2_ref, o_ref, *, transpose_out):
    hid = NSA_CMP_HIDDEN
    ab = jnp.dot(x_ref[0], wab_ref[...], preferred_element_type=f32)
    first = ab[:, :hid]
    second = pltpu.roll(ab[:, hid:], NSA_NC_PAD - 1, 0)
    pe8 = jnp.broadcast_to(pe_ref[...], (8, NSA_CMP_LEN * HEAD_DIM)).astype(bf16)
    bias = jnp.dot(pe8, w1_ref[...], preferred_element_type=f32)[0:1]
    hpre = first + second + bias
    out = jnp.dot((hpre * jax.nn.sigmoid(hpre)).astype(bf16), w2_ref[...],
                  preferred_element_type=f32)
    o_ref[0] = (out.T if transpose_out else out).astype(o_ref.dtype)


def nsa_compress(t_tok, pe, w1, w2, transpose_out):
    g = NSA_KV_GROUPS
    half_k = NSA_CMP_STRIDE * HEAD_DIM
    x = t_tok.reshape(g, NSA_NC_PAD, half_k)
    w1b = w1.astype(bf16)
    wab = jnp.concatenate([w1b[:half_k], w1b[half_k:]], axis=1)
    oshape = (g, HEAD_DIM, NSA_NC_PAD) if transpose_out else (g, NSA_NC_PAD, HEAD_DIM)
    return pl.pallas_call(
        functools.partial(_nsa_cmp_kernel, transpose_out=transpose_out),
        grid=(g,),
        in_specs=[pl.BlockSpec((1, NSA_NC_PAD, half_k), lambda gg: (gg, 0, 0)),
                  pl.BlockSpec((half_k, 2 * NSA_CMP_HIDDEN), lambda gg: (0, 0)),
                  pl.BlockSpec((1, NSA_CMP_LEN * HEAD_DIM), lambda gg: (0, 0)),
                  pl.BlockSpec((NSA_CMP_LEN * HEAD_DIM, NSA_CMP_HIDDEN), lambda gg: (0, 0)),
                  pl.BlockSpec((NSA_CMP_HIDDEN, HEAD_DIM), lambda gg: (0, 0))],
        out_specs=pl.BlockSpec((1,) + oshape[1:], lambda gg: (gg, 0, 0)),
        out_shape=jax.ShapeDtypeStruct(oshape, bf16),
        compiler_params=_params("arbitrary"),
        name="nsa_compress",
    )(x, wab, pe.reshape(1, NSA_CMP_LEN * HEAD_DIM), w1b, w2.astype(bf16))


def _nsa_attn_kernel(q_ref, kc_ref, vc_ref, ov_ref, ks_ref, vs_ref, kw_ref, vw_ref, e_ref,
                     gt_ref, z_ref, o_ref, vsaug_ref, vwaug_ref, qaug_ref, sa_ref, sb_ref):
    tq = NSA_TQ
    rep = NSA_REP
    nq = rep * tq
    i = pl.program_id(1)
    q0 = i * tq

    @pl.when(i == 0)
    def _():
        for src, dst in ((vs_ref, vsaug_ref), (vw_ref, vwaug_ref)):
            dst[0:HEAD_DIM, :] = src[...]
            dst[HEAD_DIM:, :] = jnp.ones((ONES_ROWS, SEQ), bf16)

    qT = jnp.concatenate([q_ref[r * HEAD_DIM:(r + 1) * HEAD_DIM, :] for r in range(rep)],
                         axis=1)
    lane = lax.broadcasted_iota(i32, (1, nq), 1)
    qpos = q0 + (lane & (tq - 1))
    qpos1 = q0 + lax.broadcasted_iota(i32, (1, tq), 1)

    sc = jnp.dot(kc_ref[0], qT, preferred_element_type=f32)
    cend = lax.broadcasted_iota(i32, (NSA_NC_PAD, nq), 0) * NSA_CMP_STRIDE + (NSA_CMP_LEN - 1)
    ok_c = cend <= qpos
    sc = jnp.where(ok_c, sc, NEG)
    pc = jnp.where(ok_c, jnp.exp(sc - jnp.max(sc, axis=0, keepdims=True)), 0.0)
    pc = pc / jnp.maximum(jnp.sum(pc, axis=0, keepdims=True), 1e-30)
    ocT = jnp.dot(vc_ref[0], pc.astype(bf16), preferred_element_type=f32)

    ps = pc[:, 0:tq]
    for r in range(1, rep):
        ps = ps + pc[:, r * tq:(r + 1) * tq]
    ps_hi = ps.astype(bf16)
    ps_lo = (ps - ps_hi.astype(f32)).astype(bf16)
    imp = (jnp.dot(ov_ref[...], ps_hi, preferred_element_type=f32)
           + jnp.dot(ov_ref[...], ps_lo, preferred_element_type=f32))
    sidx = lax.broadcasted_iota(i32, (NSA_NS, tq), 0)
    cur = jnp.right_shift(qpos1, 6)
    valid = sidx * NSA_SLC_BLOCK <= qpos1
    forced = (sidx == 0) | (sidx == cur) | (sidx == cur - 1)
    val = jnp.where(valid, jnp.where(forced, POS_BIG, imp), NEG)
    sel = _first_k_select(val, sidx, NSA_NS, NSA_SLC_TOPK, lambda t, m: jnp.where(m > NEG / 2, 1.0, 0.0))

    bias = jnp.where(sel > 0.5, 0.0, NEG)
    qaug_ref[0:HEAD_DIM, :] = qT
    qaug_ref[HEAD_DIM:, :] = jnp.concatenate([bias] * rep, axis=1).astype(bf16)
    offd = pl.multiple_of(q0, tq)
    kaug = jnp.concatenate([ks_ref[0, pl.ds(offd, tq), :], e_ref[pl.ds(offd, tq), :]], axis=1)
    s = jnp.dot(kaug, qaug_ref[...], preferred_element_type=f32)
    s = jnp.where(q0 + lax.broadcasted_iota(i32, (tq, nq), 0) <= qpos, s, NEG)
    m = jnp.max(s, axis=0, keepdims=True)
    acc = jnp.dot(vsaug_ref[:, pl.ds(offd, tq)], _probs(s, m),
                  preferred_element_type=f32)

    own_first = i * (tq // NSA_SLC_BLOCK)
    bias_past = jnp.where(sidx >= own_first, NEG, bias)
    qaug_ref[HEAD_DIM:, :] = jnp.concatenate([bias_past] * rep, axis=1).astype(bf16)

    per_span = NSA_SPAN // tq
    pairs = jnp.right_shift(i + per_span - 1, per_span.bit_length() - 1)
    _, acc = _flash_pairs(ks_ref, e_ref, vsaug_ref, qaug_ref, sa_ref, sb_ref, pairs, (m, acc))
    osT = acc[0:HEAD_DIM] / jnp.maximum(acc[HEAD_DIM:HEAD_DIM + 1], 1e-30)

    wspan = NSA_WINDOW + tq
    start = pl.multiple_of(jnp.maximum(q0 - NSA_WINDOW, 0), tq)
    sw = jnp.dot(kw_ref[0, pl.ds(start, wspan), :], qT, preferred_element_type=f32)
    kpos = start + lax.broadcasted_iota(i32, (wspan, nq), 0)
    sw = jnp.where((kpos <= qpos) & (kpos > qpos - NSA_WINDOW), sw, NEG)
    pw = _probs(sw, jnp.max(sw, axis=0, keepdims=True))
    accw = jnp.dot(vwaug_ref[:, pl.ds(start, wspan)], pw, preferred_element_type=f32)
    owT = accw[0:HEAD_DIM] / jnp.maximum(accw[HEAD_DIM:HEAD_DIM + 1], 1e-30)

    gt = gt_ref[0]
    for r in range(rep):
        cs = slice(r * tq, (r + 1) * tq)
        rs = slice(r * HEAD_DIM, (r + 1) * HEAD_DIM)
        o = (gt[3 * r:3 * r + 1] * ocT[:, cs] + gt[3 * r + 1:3 * r + 2] * osT[:, cs]
             + gt[3 * r + 2:3 * r + 3] * owT[:, cs])
        o_ref[rs, :] = (o * z_ref[rs, :].astype(f32)).astype(bf16)


def _nsa_overlap_t():
    nc = NSA_NC_PAD - 1
    cs = np.arange(nc)[:, None] * NSA_CMP_STRIDE
    ss = np.arange(NSA_NS)[None, :] * NSA_SLC_BLOCK
    ov = np.clip(np.minimum(cs + NSA_CMP_LEN, ss + NSA_SLC_BLOCK) - np.maximum(cs, ss), 0, None)
    ov = np.concatenate([ov / NSA_CMP_LEN, np.zeros((1, NSA_NS))], axis=0)
    return jnp.asarray(ov.T, dtype=bf16)


def nsa_attention(qT, k_tok, plainT, k_cmp, v_cmpT, gatesT, zT):
    tq = NSA_TQ
    g = NSA_KV_GROUPS
    qrows = NSA_REP * HEAD_DIM
    onehot = ((jnp.arange(SEQ) // NSA_SLC_BLOCK)[:, None] == jnp.arange(NSA_NS)[None, :]).astype(bf16)
    keys = lambda first: pl.BlockSpec((1, SEQ, HEAD_DIM), lambda gg, i: (first + gg, 0, 0))
    vals = lambda first: pl.BlockSpec((HEAD_DIM, SEQ), lambda gg, i: (first + gg, 0))
    return pl.pallas_call(
        _nsa_attn_kernel,
        grid=(g, SEQ // tq),
        in_specs=[pl.BlockSpec((qrows, tq), lambda gg, i: (gg, i)),
                  pl.BlockSpec((1, NSA_NC_PAD, HEAD_DIM), lambda gg, i: (gg, 0, 0)),
                  pl.BlockSpec((1, HEAD_DIM, NSA_NC_PAD), lambda gg, i: (gg, 0, 0)),
                  pl.BlockSpec((NSA_NS, NSA_NC_PAD), lambda gg, i: (0, 0)),
                  keys(0), vals(0), keys(g), vals(g),
                  pl.BlockSpec((SEQ, NSA_NS), lambda gg, i: (0, 0)),
                  pl.BlockSpec((1, NSA_GATE_ROWS, tq), lambda gg, i: (gg, 0, i)),
                  pl.BlockSpec((qrows, tq), lambda gg, i: (gg, i))],
        out_specs=pl.BlockSpec((qrows, tq), lambda gg, i: (gg, i)),
        out_shape=jax.ShapeDtypeStruct((NSA_HEADS * HEAD_DIM, SEQ), bf16),
        scratch_shapes=[pltpu.VMEM((HEAD_DIM + ONES_ROWS, SEQ), bf16),
                        pltpu.VMEM((HEAD_DIM + ONES_ROWS, SEQ), bf16),
                        pltpu.VMEM((HEAD_DIM + NSA_NS, NSA_REP * tq), bf16),
                        pltpu.VMEM((NSA_SPAN // 2, NSA_REP * tq), f32),
                        pltpu.VMEM((NSA_SPAN // 2, NSA_REP * tq), f32)],
        compiler_params=_params("arbitrary", "arbitrary"),
        name="nsa_attention",
    )(qT, k_cmp, v_cmpT, _nsa_overlap_t(), k_tok, plainT, k_tok, plainT, onehot, gatesT, zT)


def _ret_kernel(q_ref, k_ref, v_ref, g_ref, dt_ref, rd_ref, wd_ref, cd_ref, o_ref, st_ref):
    @pl.when(pl.program_id(1) == 0)
    def _():
        st_ref[...] = jnp.zeros(st_ref.shape, f32)

    qT = q_ref[...]
    kT = k_ref[...]
    vT = v_ref[...]
    innerT = lax.dot_general(kT, qT, TN_DIMS, preferred_element_type=f32) * dt_ref[0]
    st = st_ref[...]
    o = (jnp.dot(vT, innerT.astype(bf16), preferred_element_type=f32)
         + jnp.dot(st.astype(bf16), qT, preferred_element_type=f32) * rd_ref[0])
    kw = (kT.astype(f32) * wd_ref[0]).astype(bf16)
    st_ref[...] = cd_ref[0] * st + lax.dot_general(vT, kw, NT_DIMS, preferred_element_type=f32)
    mu = jnp.mean(o, axis=0, keepdims=True)
    d = o - mu
    var = jnp.mean(d * d, axis=0, keepdims=True)
    o_ref[...] = (g_ref[...].astype(f32) * (d * lax.rsqrt(var + EPS))).astype(bf16)


def retention(qkT, vT, gT):
    c = RET_CHUNK
    h = RET_HEADS
    log_gamma = jnp.log(1.0 - 2.0 ** (-5.0 - jnp.arange(h, dtype=f32)))
    i = jnp.arange(c, dtype=f32)
    diff = i[None, :] - i[:, None]
    dt = jnp.where(diff >= 0, jnp.exp(log_gamma[:, None, None] * jnp.maximum(diff, 0.0)), 0.0)
    rd = jnp.exp(log_gamma[:, None] * (i + 1.0))[:, None, :]
    wd = jnp.exp(log_gamma[:, None] * (c - 1.0 - i))[:, None, :]
    cd = jnp.broadcast_to(jnp.exp(log_gamma * c)[:, None, None], (h, 1, RET_QK_DIM))
    vec = lambda n: pl.BlockSpec((1, 1, n), lambda hh, t: (hh, 0, 0))
    return pl.pallas_call(
        _ret_kernel,
        grid=(h, SEQ // c),
        in_specs=[pl.BlockSpec((RET_QK_DIM, c), lambda hh, t: (hh, t)),
                  pl.BlockSpec((RET_QK_DIM, c), lambda hh, t: (h + hh, t)),
                  pl.BlockSpec((RET_V_DIM, c), lambda hh, t: (hh, t)),
                  pl.BlockSpec((RET_V_DIM, c), lambda hh, t: (hh, t)),
                  pl.BlockSpec((1, c, c), lambda hh, t: (hh, 0, 0)),
                  vec(c), vec(c), vec(RET_QK_DIM)],
        out_specs=pl.BlockSpec((RET_V_DIM, c), lambda hh, t: (hh, t)),
        out_shape=jax.ShapeDtypeStruct((h * RET_V_DIM, SEQ), bf16),
        scratch_shapes=[pltpu.VMEM((RET_V_DIM, RET_QK_DIM), f32)],
        compiler_params=_params("arbitrary", "arbitrary"),
        name="retention",
    )(qkT, qkT, vT, gT, dt, rd, wd, cd)


def _t(w):
    return w.T.astype(bf16)


def moba_mixer(hT, tabs16, w_in, layer, w_out):
    hd = MOBA_HEADS * HEAD_DIM
    proj = functools.partial(project, hT, w_in, layer=layer)
    qT = proj((0, hd), "rope", tables=tabs16, scale=HEAD_DIM ** -0.5, scale_rows=(0, hd),
              name="moba_proj_q")
    k_tok = proj((hd, hd), "rope", tables=tabs16, token_major=True, name="moba_proj_k")
    vT = proj((2 * hd, hd), "plain", name="moba_proj_v")
    zT = proj((3 * hd, hd), "silu", name="moba_proj_z")
    return moba_attention(qT, k_tok, vT, zT), _t(w_out)


def nsa_mixer(hT, tabs16, w_in, layer, w_out, pe, wk1, wk2, wv1, wv2):
    hd = NSA_HEADS * HEAD_DIM
    gd = NSA_KV_GROUPS * HEAD_DIM
    proj = functools.partial(project, hT, w_in, layer=layer)
    qT = proj((0, hd), "rope", tables=tabs16, scale=HEAD_DIM ** -0.5, scale_rows=(0, hd),
              name="nsa_proj_q")
    k_tok = proj((hd + 2 * gd, 2 * gd), "rope", col_step=2, tables=tabs16, token_major=True,
                 name="nsa_proj_k")
    plainT = proj((hd + 3 * gd, 2 * gd), "plain", col_step=2, name="nsa_proj_v")
    kc_tok = proj((hd, gd), "rope", tables=tabs16, token_major=True, name="nsa_proj_kc")
    vc_tok = proj((hd + gd, gd), "plain", token_major=True, name="nsa_proj_vc")
    g0 = hd + 6 * gd
    per = NSA_REP * 3
    wg = w_in[layer, :, g0:g0 + NSA_KV_GROUPS * per].reshape(D_MODEL, NSA_KV_GROUPS, per)
    wg = jnp.pad(wg, ((0, 0), (0, 0), (0, NSA_GATE_ROWS - per))).reshape(D_MODEL, -1)
    gatesT = project(hT, wg, (0, wg.shape[1]), "sigmoid", out_dtype=f32, name="nsa_proj_gate")
    gatesT = gatesT.reshape(NSA_KV_GROUPS, NSA_GATE_ROWS, SEQ)
    wz = w_in[layer, :, g0 + NSA_HEADS * 3:]
    zT = project(hT, wz, (0, hd), "silu", name="nsa_proj_z")
    k_cmp = nsa_compress(kc_tok, pe, wk1, wk2, transpose_out=False)
    v_cmpT = nsa_compress(vc_tok, pe, wv1, wv2, transpose_out=True)
    return nsa_attention(qT, k_tok, plainT, k_cmp, v_cmpT, gatesT, zT), _t(w_out)


def retention_mixer(hT, tabs128, w_in, layer, w_out):
    qk = 2 * RET_HEADS * RET_QK_DIM
    vd = RET_HEADS * RET_V_DIM
    proj = functools.partial(project, hT, w_in, layer=layer)
    qkT = proj((0, qk), "rope", tables=tabs128, slab=RET_QK_DIM, half=RET_QK_DIM // 2,
               scale=RET_QK_DIM ** -0.5, scale_rows=(qk // 2, qk), name="ret_proj_qk")
    vT = proj((qk, vd), "plain", name="ret_proj_v")
    gT = proj((qk + vd, vd), "silu", name="ret_proj_g")
    return retention(qkT, vT, gT), _t(w_out)


def kernel(x, c, positions, ada_w, ada_b, norm_pre, norm_post, moba_w_in, moba_w_out, nsa_w_in, nsa_w_out, nsa_cmp_pe, nsa_cmp_wk1, nsa_cmp_wk2, nsa_cmp_wv1, nsa_cmp_wv2, ret_w_in, ret_w_out):
    assert x.shape == (1, SEQ, D_MODEL)
    mod = ada_modulation(c, ada_w, ada_b)
    tabs16 = rope_tables(positions, ROPE_DIM // 2, ROPE_THETA)
    tabs128 = rope_tables(positions, RET_QK_DIM // 2, RET_ROT_THETA)
    col = lambda v: v.reshape(D_MODEL, 1)
    xT = x[0].T
    for i in range(DEPTH):
        kind, j = i % N_MIXERS, i // N_MIXERS
        shift, scale, gate = mod[i, :D_MODEL], mod[i, D_MODEL:2 * D_MODEL], mod[i, 2 * D_MODEL:]
        hT = prenorm(xT, col(norm_pre[i]), col(scale), col(shift))
        if kind == 0:
            aT, w_outT = moba_mixer(hT, tabs16, moba_w_in, j, moba_w_out[j])
        elif kind == 1:
            aT, w_outT = nsa_mixer(hT, tabs16, nsa_w_in, j, nsa_w_out[j], nsa_cmp_pe[j],
                                   nsa_cmp_wk1[j], nsa_cmp_wk2[j], nsa_cmp_wv1[j], nsa_cmp_wv2[j])
        else:
            aT, w_outT = retention_mixer(hT, tabs128, ret_w_in, j, ret_w_out[j])
        xT = outproj_residual(aT, w_outT, xT, col(gate), col(norm_post[i]))
    return xT.T[None]
```

```python
import functools

import numpy as np
import jax
import jax.numpy as jnp
from jax import lax
from jax.experimental import pallas as pl
from jax.experimental.pallas import tpu as pltpu

f32 = jnp.float32
bf16 = jnp.bfloat16
i32 = jnp.int32

D_MODEL = 2048
SEQ = 8192
DEPTH = 4
N_MIXERS = 3
HEAD_DIM = 128
ROPE_THETA = 500000.0
ROPE_DIM = HEAD_DIM // 4
EPS = 1e-6
NEG = -1e30
POS_BIG = 1e30

MOBA_HEADS = D_MODEL // HEAD_DIM
MOBA_BLOCK = 256
MOBA_TOPK = 3
MOBA_NB = SEQ // MOBA_BLOCK
MOBA_TQ = 1024
ONES_ROWS = 16
PROJ_TN = 512
COL_GROUP = 256

NSA_HEADS = D_MODEL // HEAD_DIM
NSA_KV_GROUPS = NSA_HEADS // 4
NSA_REP = NSA_HEADS // NSA_KV_GROUPS
NSA_CMP_LEN = 32
NSA_CMP_STRIDE = 16
NSA_CMP_HIDDEN = 256
NSA_SLC_BLOCK = 64
NSA_SLC_TOPK = 16
NSA_WINDOW = 512
NSA_NC_PAD = SEQ // NSA_CMP_STRIDE
NSA_NS = SEQ // NSA_SLC_BLOCK
NSA_GATE_ROWS = 16
NSA_TQ = 256
NSA_SPAN = 1024

RET_HEADS = 8
RET_QK_DIM = D_MODEL // RET_HEADS
RET_V_DIM = 2 * D_MODEL // RET_HEADS
RET_CHUNK = 256
RET_ROT_THETA = 10000.0
RET_HEADS_PER_STEP = 2

VMEM_LIMIT = 56 * 1024 * 1024

TN_DIMS = (((0,), (0,)), ((), ()))
NT_DIMS = (((1,), (1,)), ((), ()))


def _params(*sem):
    return pltpu.CompilerParams(dimension_semantics=sem, vmem_limit_bytes=VMEM_LIMIT)


def _rope_table_kernel(pos_ref, inv_ref, cos_ref, sin_ref):
    ang = pos_ref[...].astype(f32) * inv_ref[...]
    cos_ref[...] = jnp.cos(ang)
    sin_ref[...] = jnp.sin(ang)


def rope_tables(positions, half, theta):
    inv = (theta ** (-jnp.arange(half, dtype=f32) / half)).reshape(half, 1)
    ts = 1024
    return pl.pallas_call(
        _rope_table_kernel,
        grid=(SEQ // ts,),
        in_specs=[pl.BlockSpec((1, ts), lambda i: (0, i)),
                  pl.BlockSpec((half, 1), lambda i: (0, 0))],
        out_specs=[pl.BlockSpec((half, ts), lambda i: (0, i))] * 2,
        out_shape=[jax.ShapeDtypeStruct((half, SEQ), f32)] * 2,
        name=f"rope_tables_{half}",
    )(positions, inv)


def _ada_kernel(c_ref, w_ref, b_ref, o_ref):
    sc = jax.nn.silu(c_ref[...])
    sc8 = jnp.broadcast_to(sc, (8, D_MODEL)).astype(bf16)
    acc = jnp.dot(sc8, w_ref[0].astype(bf16), preferred_element_type=f32)
    o_ref[0] = acc[0:1] + b_ref[0]


def ada_modulation(c, ada_w, ada_b):
    tn = 1024
    n3 = 3 * D_MODEL
    out = pl.pallas_call(
        _ada_kernel,
        grid=(DEPTH, n3 // tn),
        in_specs=[pl.BlockSpec((1, D_MODEL), lambda l, j: (0, 0)),
                  pl.BlockSpec((1, D_MODEL, tn), lambda l, j: (l, 0, j)),
                  pl.BlockSpec((1, 1, tn), lambda l, j: (l, 0, j))],
        out_specs=pl.BlockSpec((1, 1, tn), lambda l, j: (l, 0, j)),
        out_shape=jax.ShapeDtypeStruct((DEPTH, 1, n3), f32),
        compiler_params=_params("arbitrary", "arbitrary"),
        name="ada_modulation",
    )(c, ada_w, ada_b.reshape(DEPTH, 1, n3))
    return out.reshape(DEPTH, n3)


def _prenorm_kernel(x_ref, g_ref, sc_ref, sh_ref, *o_refs, token_major_in):
    if token_major_in:
        xT_ref, o_ref = o_refs
        x = x_ref[...].T
        xT_ref[...] = x
    else:
        (o_ref,) = o_refs
        x = x_ref[...]
    ms = jnp.mean(x * x, axis=0, keepdims=True)
    y = (x * lax.rsqrt(ms + EPS)) * g_ref[...]
    o_ref[...] = (y * (1.0 + sc_ref[...]) + sh_ref[...]).astype(bf16)


def prenorm(x, g_col, scale_col, shift_col, token_major_in=False):
    tm = 512
    col = pl.BlockSpec((D_MODEL, 1), lambda i: (0, 0))
    fm = pl.BlockSpec((D_MODEL, tm), lambda i: (0, i))
    x_spec = pl.BlockSpec((tm, D_MODEL), lambda i: (i, 0)) if token_major_in else fm
    h_shape = jax.ShapeDtypeStruct((D_MODEL, SEQ), bf16)
    return pl.pallas_call(
        functools.partial(_prenorm_kernel, token_major_in=token_major_in),
        grid=(SEQ // tm,),
        in_specs=[x_spec, col, col, col],
        out_specs=[fm, fm] if token_major_in else fm,
        out_shape=[jax.ShapeDtypeStruct((D_MODEL, SEQ), f32), h_shape] if token_major_in else h_shape,
        compiler_params=_params("arbitrary"),
        name="prenorm",
    )(x, g_col, scale_col, shift_col)


def _proj_kernel(*refs, kind, tn, slab, half, scale, scale_lo, scale_hi, token_major):
    if kind == "rope":
        h_ref, w_ref, cos_ref, sin_ref, o_ref = refs
    else:
        h_ref, w_ref, o_ref = refs
    acc = lax.dot_general(w_ref[...].astype(bf16), h_ref[...], TN_DIMS,
                          preferred_element_type=f32)
    if scale is not None:
        j = pl.program_id(1)
        acc = acc * jnp.where((j >= scale_lo) & (j < scale_hi), scale, 1.0).astype(f32)
    if kind == "silu":
        acc = acc * jax.nn.sigmoid(acc)
    elif kind == "sigmoid":
        acc = jax.nn.sigmoid(acc)

    if kind == "rope":
        cos = cos_ref[...]
        sin = sin_ref[...]
        pieces = []
        for s in range(tn // slab):
            b = s * slab
            x1 = acc[b:b + half]
            x2 = acc[b + half:b + 2 * half]
            pieces.append(x1 * cos - x2 * sin)
            pieces.append(x1 * sin + x2 * cos)
            if 2 * half < slab:
                pieces.append(acc[b + 2 * half:b + slab])
        acc = jnp.concatenate(pieces, axis=0)
    if token_major:
        for s in range(tn // HEAD_DIM):
            o_ref[s] = acc[s * HEAD_DIM:(s + 1) * HEAD_DIM].T.astype(o_ref.dtype)
    else:
        o_ref[...] = acc.astype(o_ref.dtype)


def project(hT, w, cols, kind, *, layer=None, col_step=1, tables=None, slab=HEAD_DIM,
            half=ROPE_DIM // 2, scale=None, scale_rows=(0, 0), token_major=False,
            out_dtype=bf16, tn=PROJ_TN, name="proj"):
    start, n = cols
    tm = 1024
    tn = min(tn, n)
    first = start // tn
    assert n % tn == 0 and start % tn == 0
    assert (first + col_step * (n // tn - 1) + 1) * tn <= w.shape[-1]
    assert scale_rows[0] % tn == 0 and scale_rows[1] % tn == 0
    assert kind != "rope" or tn % slab == 0
    if layer is None:
        w_spec = pl.BlockSpec((D_MODEL, tn), lambda i, j: (0, first + col_step * j))
    else:
        w_spec = pl.BlockSpec((None, D_MODEL, tn), lambda i, j: (layer, 0, first + col_step * j))
    in_specs = [pl.BlockSpec((D_MODEL, tm), lambda i, j: (0, i)), w_spec]
    args = [hT, w]
    if kind == "rope":
        in_specs += [pl.BlockSpec((half, tm), lambda i, j: (0, i))] * 2
        args += list(tables)
    if token_major:
        out_spec = pl.BlockSpec((tn // HEAD_DIM, tm, HEAD_DIM), lambda i, j: (j, i, 0))
        out_shape = jax.ShapeDtypeStruct((n // HEAD_DIM, SEQ, HEAD_DIM), out_dtype)
    else:
        out_spec = pl.BlockSpec((tn, tm), lambda i, j: (j, i))
        out_shape = jax.ShapeDtypeStruct((n, SEQ), out_dtype)
    body = functools.partial(
        _proj_kernel, kind=kind, tn=tn, slab=slab, half=half, scale=scale,
        scale_lo=scale_rows[0] // tn, scale_hi=scale_rows[1] // tn, token_major=token_major)
    return pl.pallas_call(
        body, grid=(SEQ // tm, n // tn), in_specs=in_specs, out_specs=out_spec,
        out_shape=out_shape, compiler_params=_params("arbitrary", "arbitrary"), name=name,
    )(*args)


def _outproj_kernel(a_ref, w_ref, x_ref, gate_ref, g_ref, o_ref, *, token_major_out):
    y = jnp.dot(w_ref[...], a_ref[...], preferred_element_type=f32)
    ms = jnp.mean(y * y, axis=0, keepdims=True)
    yn = (y * lax.rsqrt(ms + EPS)) * g_ref[...]
    out = x_ref[...] + gate_ref[...] * yn
    o_ref[...] = out.T if token_major_out else out


def outproj_residual(aT, w_outT, xT, gate_col, g_col, token_major_out=False):
    k = aT.shape[0]
    tm = 256
    col = pl.BlockSpec((D_MODEL, 1), lambda i: (0, 0))
    if token_major_out:
        out_spec = pl.BlockSpec((tm, D_MODEL), lambda i: (i, 0))
        out_shape = jax.ShapeDtypeStruct((SEQ, D_MODEL), f32)
    else:
        out_spec = pl.BlockSpec((D_MODEL, tm), lambda i: (0, i))
        out_shape = jax.ShapeDtypeStruct((D_MODEL, SEQ), f32)
    return pl.pallas_call(
        functools.partial(_outproj_kernel, token_major_out=token_major_out),
        grid=(SEQ // tm,),
        in_specs=[pl.BlockSpec((k, tm), lambda i: (0, i)),
                  pl.BlockSpec((D_MODEL, k), lambda i: (0, 0)),
                  pl.BlockSpec((D_MODEL, tm), lambda i: (0, i)), col, col],
        out_specs=out_spec,
        out_shape=out_shape,
        compiler_params=_params("arbitrary"),
        name="outproj_residual",
    )(aT, w_outT, xT, gate_col, g_col)


def _first_k_select(val, idx, n_rows, k, slot_ok):
    sel = jnp.zeros(val.shape, f32)
    for t in range(k):
        m = jnp.max(val, axis=0, keepdims=True)
        first = jnp.min(jnp.where(val == m, idx, n_rows), axis=0, keepdims=True)
        pick = idx == first
        sel = jnp.where(pick, slot_ok(t, m), sel)
        val = jnp.where(pick, -jnp.inf, val)
    return sel


def _probs(s, m):
    return jnp.exp((s - m).astype(bf16))


def _flash_step(s, va, m, acc):
    mn = jnp.maximum(m, jnp.max(s, axis=0, keepdims=True))
    return mn, jnp.exp(m - mn) * acc + jnp.dot(va, _probs(s, mn), preferred_element_type=f32)


def _flash_step_ref(s_ref, va, m, acc, group=COL_GROUP):
    mns, ps = [], []
    for c in range(s_ref.shape[1] // group):
        cs = slice(c * group, (c + 1) * group)
        mn = jnp.maximum(m[:, cs], jnp.max(s_ref[:, cs], axis=0, keepdims=True))
        mns.append(mn)
        ps.append(_probs(s_ref[:, cs], mn))
    mn = jnp.concatenate(mns, axis=1)
    p = jnp.concatenate(ps, axis=1)
    return mn, jnp.exp(m - mn) * acc + jnp.dot(va, p, preferred_element_type=f32)


def _flash_pairs(k_ref, e_ref, vaug_ref, qaug_ref, sa_ref, sb_ref, pairs, carry):
    span = sa_ref.shape[0]
    last = 2 * pairs - 1

    def scores(n):
        off = pl.multiple_of(n * span, span)
        kaug = jnp.concatenate([k_ref[0, pl.ds(off, span), :], e_ref[pl.ds(off, span), :]], axis=1)
        return jnp.dot(kaug, qaug_ref[...], preferred_element_type=f32)

    def values(n):
        return vaug_ref[:, pl.ds(pl.multiple_of(n * span, span), span)]

    sa_ref[...] = scores(0)

    def body(j, carry):
        n = 2 * j
        sb_ref[...] = scores(n + 1)
        carry = _flash_step_ref(sa_ref, values(n), *carry)
        sa_ref[...] = scores(jnp.minimum(n + 2, last))
        return _flash_step_ref(sb_ref, values(n + 1), *carry)

    return lax.fori_loop(0, pairs, body, carry)


def _moba_attn_kernel(q_ref, k_ref, v_ref, z_ref, avg_ref, e_ref, o_ref,
                      kmean_ref, vaug_ref, qaug_ref, sa_ref, sb_ref):
    blk = MOBA_BLOCK
    tq = MOBA_TQ
    sub = tq // blk
    i = pl.program_id(1)
    b0 = i * sub

    @pl.when(i == 0)
    def _():
        kmean_ref[...] = jnp.dot(avg_ref[...], k_ref[0], preferred_element_type=f32)
        vaug_ref[0:HEAD_DIM, :] = v_ref[...]
        vaug_ref[HEAD_DIM:, :] = jnp.ones((ONES_ROWS, SEQ), bf16)

    qT = q_ref[...]
    gate = jnp.dot(kmean_ref[...], qT.astype(f32), precision=lax.Precision.HIGHEST,
                   preferred_element_type=f32)
    nidx = lax.broadcasted_iota(i32, (MOBA_NB, tq), 0)
    own = b0 + jnp.right_shift(lax.broadcasted_iota(i32, (1, tq), 1), blk.bit_length() - 1)
    gate = jnp.where(nidx < own, gate, NEG)
    sel = _first_k_select(gate, nidx, MOBA_NB, MOBA_TOPK, lambda t, m: jnp.where(t < own, 1.0, 0.0))
    qaug_ref[0:HEAD_DIM, :] = qT
    qaug_ref[HEAD_DIM:HEAD_DIM + MOBA_NB, :] = jnp.where(sel > 0.5, 0.0, NEG).astype(bf16)
    qaug_ref[HEAD_DIM + MOBA_NB:, :] = jnp.zeros((HEAD_DIM - MOBA_NB, tq), bf16)

    offs = [pl.multiple_of((b0 + c) * blk, blk) for c in range(sub)]
    s_own = jnp.concatenate(
        [jnp.dot(k_ref[0, pl.ds(offs[c], blk), :], qT[:, c * blk:(c + 1) * blk],
                 preferred_element_type=f32) for c in range(sub)], axis=1)
    krow = lax.broadcasted_iota(i32, (blk, tq), 0)
    qcol = lax.broadcasted_iota(i32, (blk, tq), 1) & (blk - 1)
    s_own = jnp.where(krow <= qcol, s_own, NEG)
    m = jnp.max(s_own, axis=0, keepdims=True)
    p = _probs(s_own, m)
    acc = jnp.concatenate(
        [jnp.dot(vaug_ref[:, pl.ds(offs[c], blk)], p[:, c * blk:(c + 1) * blk],
                 preferred_element_type=f32) for c in range(sub)], axis=1)

    inner = (sub - 1) * blk
    kaug = jnp.concatenate([k_ref[0, pl.ds(offs[0], inner), :], e_ref[pl.ds(offs[0], inner), :]],
                           axis=1)
    s_in = jnp.dot(kaug, qaug_ref[:, blk:], preferred_element_type=f32)
    m_hi, acc_hi = _flash_step(s_in, vaug_ref[:, pl.ds(offs[0], inner)], m[:, blk:], acc[:, blk:])
    m = jnp.concatenate([m[:, :blk], m_hi], axis=1)
    acc = jnp.concatenate([acc[:, :blk], acc_hi], axis=1)

    _, acc = _flash_pairs(k_ref, e_ref, vaug_ref, qaug_ref, sa_ref, sb_ref, i, (m, acc))
    o = acc[0:HEAD_DIM] / jnp.maximum(acc[HEAD_DIM:HEAD_DIM + 1], 1e-30)
    o_ref[...] = (o * z_ref[...].astype(f32)).astype(bf16)


def moba_attention(qT, k_tok, vT, zT):
    tq = MOBA_TQ
    h = MOBA_HEADS
    blk_of = jnp.arange(SEQ) // MOBA_BLOCK
    avg = ((blk_of[None, :] == jnp.arange(MOBA_NB)[:, None]).astype(f32) / MOBA_BLOCK).astype(bf16)
    onehot = (blk_of[:, None] == jnp.arange(HEAD_DIM)[None, :]).astype(bf16)
    return pl.pallas_call(
        _moba_attn_kernel,
        grid=(h, SEQ // tq),
        in_specs=[pl.BlockSpec((HEAD_DIM, tq), lambda hh, i: (hh, i)),
                  pl.BlockSpec((1, SEQ, HEAD_DIM), lambda hh, i: (hh, 0, 0)),
                  pl.BlockSpec((HEAD_DIM, SEQ), lambda hh, i: (hh, 0)),
                  pl.BlockSpec((HEAD_DIM, tq), lambda hh, i: (hh, i)),
                  pl.BlockSpec((MOBA_NB, SEQ), lambda hh, i: (0, 0)),
                  pl.BlockSpec((SEQ, HEAD_DIM), lambda hh, i: (0, 0))],
        out_specs=pl.BlockSpec((HEAD_DIM, tq), lambda hh, i: (hh, i)),
        out_shape=jax.ShapeDtypeStruct((h * HEAD_DIM, SEQ), bf16),
        scratch_shapes=[pltpu.VMEM((MOBA_NB, HEAD_DIM), f32),
                        pltpu.VMEM((HEAD_DIM + ONES_ROWS, SEQ), bf16),
                        pltpu.VMEM((2 * HEAD_DIM, tq), bf16),
                        pltpu.VMEM((tq // 2, tq), f32),
                        pltpu.VMEM((tq // 2, tq), f32)],
        compiler_params=_params("arbitrary", "arbitrary"),
        name="moba_attention",
    )(qT, k_tok, vT, zT, avg, onehot)


def _nsa_cmp_kernel(x_ref, wab_ref, pe_ref, w1_ref, w2_ref, o_ref, *, transpose_out):
    hid = NSA_CMP_HIDDEN
    ab = jnp.dot(x_ref[0], wab_ref[...], preferred_element_type=f32)
    first = ab[:, :hid]
    second = pltpu.roll(ab[:, hid:], NSA_NC_PAD - 1, 0)
    pe8 = jnp.broadcast_to(pe_ref[...], (8, NSA_CMP_LEN * HEAD_DIM)).astype(bf16)
    bias = jnp.dot(pe8, w1_ref[...], preferred_element_type=f32)[0:1]
    hpre = first + second + bias
    out = jnp.dot((hpre * jax.nn.sigmoid(hpre)).astype(bf16), w2_ref[...],
                  preferred_element_type=f32)
    o_ref[0] = (out.T if transpose_out else out).astype(o_ref.dtype)


def nsa_compress(t_tok, pe, w1, w2, transpose_out):
    g = NSA_KV_GROUPS
    half_k = NSA_CMP_STRIDE * HEAD_DIM
    x = t_tok.reshape(g, NSA_NC_PAD, half_k)
    w1b = w1.astype(bf16)
    wab = jnp.concatenate([w1b[:half_k], w1b[half_k:]], axis=1)
    oshape = (g, HEAD_DIM, NSA_NC_PAD) if transpose_out else (g, NSA_NC_PAD, HEAD_DIM)
    return pl.pallas_call(
        functools.partial(_nsa_cmp_kernel, transpose_out=transpose_out),
        grid=(g,),
        in_specs=[pl.BlockSpec((1, NSA_NC_PAD, half_k), lambda gg: (gg, 0, 0)),
                  pl.BlockSpec((half_k, 2 * NSA_CMP_HIDDEN), lambda gg: (0, 0)),
                  pl.BlockSpec((1, NSA_CMP_LEN * HEAD_DIM), lambda gg: (0, 0)),
                  pl.BlockSpec((NSA_CMP_LEN * HEAD_DIM, NSA_CMP_HIDDEN), lambda gg: (0, 0)),
                  pl.BlockSpec((NSA_CMP_HIDDEN, HEAD_DIM), lambda gg: (0, 0))],
        out_specs=pl.BlockSpec((1,) + oshape[1:], lambda gg: (gg, 0, 0)),
        out_shape=jax.ShapeDtypeStruct(oshape, bf16),
        compiler_params=_params("arbitrary"),
        name="nsa_compress",
    )(x, wab, pe.reshape(1, NSA_CMP_LEN * HEAD_DIM), w1b, w2.astype(bf16))


def _nsa_attn_kernel(q_ref, kc_ref, vc_ref, ov_ref, ks_ref, vs_ref, kw_ref, vw_ref, e_ref,
                     gt_ref, z_ref, o_ref, vsaug_ref, vwaug_ref, qaug_ref, sa_ref, sb_ref):
    tq = NSA_TQ
    rep = NSA_REP
    nq = rep * tq
    i = pl.program_id(1)
    q0 = i * tq

    @pl.when(i == 0)
    def _():
        for src, dst in ((vs_ref, vsaug_ref), (vw_ref, vwaug_ref)):
            dst[0:HEAD_DIM, :] = src[...]
            dst[HEAD_DIM:, :] = jnp.ones((ONES_ROWS, SEQ), bf16)

    qT = jnp.concatenate([q_ref[r * HEAD_DIM:(r + 1) * HEAD_DIM, :] for r in range(rep)],
                         axis=1)
    lane = lax.broadcasted_iota(i32, (1, nq), 1)
    qpos = q0 + (lane & (tq - 1))
    qpos1 = q0 + lax.broadcasted_iota(i32, (1, tq), 1)

    sc = jnp.dot(kc_ref[0], qT, preferred_element_type=f32)
    cend = lax.broadcasted_iota(i32, (NSA_NC_PAD, nq), 0) * NSA_CMP_STRIDE + (NSA_CMP_LEN - 1)
    ok_c = cend <= qpos
    sc = jnp.where(ok_c, sc, NEG)
    pc = jnp.where(ok_c, jnp.exp(sc - jnp.max(sc, axis=0, keepdims=True)), 0.0)
    pc = pc / jnp.maximum(jnp.sum(pc, axis=0, keepdims=True), 1e-30)
    ocT = jnp.dot(vc_ref[0], pc.astype(bf16), preferred_element_type=f32)

    ps = pc[:, 0:tq]
    for r in range(1, rep):
        ps = ps + pc[:, r * tq:(r + 1) * tq]
    ps_hi = ps.astype(bf16)
    ps_lo = (ps - ps_hi.astype(f32)).astype(bf16)
    imp = (jnp.dot(ov_ref[...], ps_hi, preferred_element_type=f32)
           + jnp.dot(ov_ref[...], ps_lo, preferred_element_type=f32))
    sidx = lax.broadcasted_iota(i32, (NSA_NS, tq), 0)
    cur = jnp.right_shift(qpos1, 6)
    valid = sidx * NSA_SLC_BLOCK <= qpos1
    forced = (sidx == 0) | (sidx == cur) | (sidx == cur - 1)
    val = jnp.where(valid, jnp.where(forced, POS_BIG, imp), NEG)
    sel = _first_k_select(val, sidx, NSA_NS, NSA_SLC_TOPK, lambda t, m: jnp.where(m > NEG / 2, 1.0, 0.0))

    bias = jnp.where(sel > 0.5, 0.0, NEG)
    qaug_ref[0:HEAD_DIM, :] = qT
    qaug_ref[HEAD_DIM:, :] = jnp.concatenate([bias] * rep, axis=1).astype(bf16)
    offd = pl.multiple_of(q0, tq)
    kaug = jnp.concatenate([ks_ref[0, pl.ds(offd, tq), :], e_ref[pl.ds(offd, tq), :]], axis=1)
    s = jnp.dot(kaug, qaug_ref[...], preferred_element_type=f32)
    s = jnp.where(q0 + lax.broadcasted_iota(i32, (tq, nq), 0) <= qpos, s, NEG)
    m = jnp.max(s, axis=0, keepdims=True)
    acc = jnp.dot(vsaug_ref[:, pl.ds(offd, tq)], _probs(s, m),
                  preferred_element_type=f32)

    own_first = i * (tq // NSA_SLC_BLOCK)
    bias_past = jnp.where(sidx >= own_first, NEG, bias)
    qaug_ref[HEAD_DIM:, :] = jnp.concatenate([bias_past] * rep, axis=1).astype(bf16)

    per_span = NSA_SPAN // tq
    pairs = jnp.right_shift(i + per_span - 1, per_span.bit_length() - 1)
    _, acc = _flash_pairs(ks_ref, e_ref, vsaug_ref, qaug_ref, sa_ref, sb_ref, pairs, (m, acc))
    osT = acc[0:HEAD_DIM] / jnp.maximum(acc[HEAD_DIM:HEAD_DIM + 1], 1e-30)

    wspan = NSA_WINDOW + tq
    start = pl.multiple_of(jnp.maximum(q0 - NSA_WINDOW, 0), tq)
    sw = jnp.dot(kw_ref[0, pl.ds(start, wspan), :], qT, preferred_element_type=f32)
    kpos = start + lax.broadcasted_iota(i32, (wspan, nq), 0)
    sw = jnp.where((kpos <= qpos) & (kpos > qpos - NSA_WINDOW), sw, NEG)
    pw = _probs(sw, jnp.max(sw, axis=0, keepdims=True))
    accw = jnp.dot(vwaug_ref[:, pl.ds(start, wspan)], pw, preferred_element_type=f32)
    owT = accw[0:HEAD_DIM] / jnp.maximum(accw[HEAD_DIM:HEAD_DIM + 1], 1e-30)

    gt = gt_ref[0]
    for r in range(rep):
        cs = slice(r * tq, (r + 1) * tq)
        rs = slice(r * HEAD_DIM, (r + 1) * HEAD_DIM)
        o = (gt[3 * r:3 * r + 1] * ocT[:, cs] + gt[3 * r + 1:3 * r + 2] * osT[:, cs]
             + gt[3 * r + 2:3 * r + 3] * owT[:, cs])
        o_ref[rs, :] = (o * z_ref[rs, :].astype(f32)).astype(bf16)


def _nsa_overlap_t():
    nc = NSA_NC_PAD - 1
    cs = np.arange(nc)[:, None] * NSA_CMP_STRIDE
    ss = np.arange(NSA_NS)[None, :] * NSA_SLC_BLOCK
    ov = np.clip(np.minimum(cs + NSA_CMP_LEN, ss + NSA_SLC_BLOCK) - np.maximum(cs, ss), 0, None)
    ov = np.concatenate([ov / NSA_CMP_LEN, np.zeros((1, NSA_NS))], axis=0)
    return jnp.asarray(ov.T, dtype=bf16)


def nsa_attention(qT, k_tok, plainT, k_cmp, v_cmpT, gatesT, zT):
    tq = NSA_TQ
    g = NSA_KV_GROUPS
    qrows = NSA_REP * HEAD_DIM
    onehot = ((jnp.arange(SEQ) // NSA_SLC_BLOCK)[:, None] == jnp.arange(NSA_NS)[None, :]).astype(bf16)
    keys = lambda first: pl.BlockSpec((1, SEQ, HEAD_DIM), lambda gg, i: (first + gg, 0, 0))
    vals = lambda first: pl.BlockSpec((HEAD_DIM, SEQ), lambda gg, i: (first + gg, 0))
    return pl.pallas_call(
        _nsa_attn_kernel,
        grid=(g, SEQ // tq),
        in_specs=[pl.BlockSpec((qrows, tq), lambda gg, i: (gg, i)),
                  pl.BlockSpec((1, NSA_NC_PAD, HEAD_DIM), lambda gg, i: (gg, 0, 0)),
                  pl.BlockSpec((1, HEAD_DIM, NSA_NC_PAD), lambda gg, i: (gg, 0, 0)),
                  pl.BlockSpec((NSA_NS, NSA_NC_PAD), lambda gg, i: (0, 0)),
                  keys(0), vals(0), keys(g), vals(g),
                  pl.BlockSpec((SEQ, NSA_NS), lambda gg, i: (0, 0)),
                  pl.BlockSpec((1, NSA_GATE_ROWS, tq), lambda gg, i: (gg, 0, i)),
                  pl.BlockSpec((qrows, tq), lambda gg, i: (gg, i))],
        out_specs=pl.BlockSpec((qrows, tq), lambda gg, i: (gg, i)),
        out_shape=jax.ShapeDtypeStruct((NSA_HEADS * HEAD_DIM, SEQ), bf16),
        scratch_shapes=[pltpu.VMEM((HEAD_DIM + ONES_ROWS, SEQ), bf16),
                        pltpu.VMEM((HEAD_DIM + ONES_ROWS, SEQ), bf16),
                        pltpu.VMEM((HEAD_DIM + NSA_NS, NSA_REP * tq), bf16),
                        pltpu.VMEM((NSA_SPAN // 2, NSA_REP * tq), f32),
                        pltpu.VMEM((NSA_SPAN // 2, NSA_REP * tq), f32)],
        compiler_params=_params("arbitrary", "arbitrary"),
        name="nsa_attention",
    )(qT, k_cmp, v_cmpT, _nsa_overlap_t(), k_tok, plainT, k_tok, plainT, onehot, gatesT, zT)


def _ret_kernel(q_ref, k_ref, v_ref, g_ref, dt_ref, rd_ref, wd_ref, cd_ref, o_ref, st_ref):
    @pl.when(pl.program_id(1) == 0)
    def _():
        st_ref[...] = jnp.zeros(st_ref.shape, f32)

    heads = range(RET_HEADS_PER_STEP)
    rows = lambda ref, p, n: ref[p * n:(p + 1) * n, :]
    qT = [rows(q_ref, p, RET_QK_DIM) for p in heads]
    kT = [rows(k_ref, p, RET_QK_DIM) for p in heads]
    vT = [rows(v_ref, p, RET_V_DIM) for p in heads]
    innerT = [lax.dot_general(kT[p], qT[p], TN_DIMS, preferred_element_type=f32) * dt_ref[p]
              for p in heads]
    st = [st_ref[p] for p in heads]
    cross = [jnp.dot(st[p].astype(bf16), qT[p], preferred_element_type=f32) * rd_ref[p]
             for p in heads]
    o = [jnp.dot(vT[p], innerT[p].astype(bf16), preferred_element_type=f32) + cross[p]
         for p in heads]
    kw = [(kT[p].astype(f32) * wd_ref[p]).astype(bf16) for p in heads]
    for p in heads:
        st_ref[p] = cd_ref[p] * st[p] + lax.dot_general(vT[p], kw[p], NT_DIMS,
                                                        preferred_element_type=f32)
    for p in heads:
        mu = jnp.mean(o[p], axis=0, keepdims=True)
        d = o[p] - mu
        var = jnp.mean(d * d, axis=0, keepdims=True)
        gate = rows(g_ref, p, RET_V_DIM).astype(f32)
        o_ref[p * RET_V_DIM:(p + 1) * RET_V_DIM, :] = (gate * (d * lax.rsqrt(var + EPS))).astype(bf16)


def retention(qkT, vT, gT):
    c = RET_CHUNK
    h = RET_HEADS
    log_gamma = jnp.log(1.0 - 2.0 ** (-5.0 - jnp.arange(h, dtype=f32)))
    i = jnp.arange(c, dtype=f32)
    diff = i[None, :] - i[:, None]
    dt = jnp.where(diff >= 0, jnp.exp(log_gamma[:, None, None] * jnp.maximum(diff, 0.0)), 0.0)
    rd = jnp.exp(log_gamma[:, None] * (i + 1.0))[:, None, :]
    wd = jnp.exp(log_gamma[:, None] * (c - 1.0 - i))[:, None, :]
    cd = jnp.broadcast_to(jnp.exp(log_gamma * c)[:, None, None], (h, 1, RET_QK_DIM))
    per = RET_HEADS_PER_STEP
    steps = h // per
    vec = lambda n: pl.BlockSpec((per, 1, n), lambda hh, t: (hh, 0, 0))
    return pl.pallas_call(
        _ret_kernel,
        grid=(steps, SEQ // c),
        in_specs=[pl.BlockSpec((per * RET_QK_DIM, c), lambda hh, t: (hh, t)),
                  pl.BlockSpec((per * RET_QK_DIM, c), lambda hh, t: (steps + hh, t)),
                  pl.BlockSpec((per * RET_V_DIM, c), lambda hh, t: (hh, t)),
                  pl.BlockSpec((per * RET_V_DIM, c), lambda hh, t: (hh, t)),
                  pl.BlockSpec((per, c, c), lambda hh, t: (hh, 0, 0)),
                  vec(c), vec(c), vec(RET_QK_DIM)],
        out_specs=pl.BlockSpec((per * RET_V_DIM, c), lambda hh, t: (hh, t)),
        out_shape=jax.ShapeDtypeStruct((h * RET_V_DIM, SEQ), bf16),
        scratch_shapes=[pltpu.VMEM((per, RET_V_DIM, RET_QK_DIM), f32)],
        compiler_params=_params("arbitrary", "arbitrary"),
        name="retention",
    )(qkT, qkT, vT, gT, dt, rd, wd, cd)


def _t(w):
    return w.T.astype(bf16)


def moba_mixer(hT, tabs16, w_in, layer, w_out):
    hd = MOBA_HEADS * HEAD_DIM
    proj = functools.partial(project, hT, w_in, layer=layer)
    qT = proj((0, hd), "rope", tables=tabs16, scale=HEAD_DIM ** -0.5, scale_rows=(0, hd),
              name="moba_proj_q")
    k_tok = proj((hd, hd), "rope", tables=tabs16, token_major=True, name="moba_proj_k")
    vT = proj((2 * hd, hd), "plain", name="moba_proj_v")
    zT = proj((3 * hd, hd), "silu", name="moba_proj_z")
    return moba_attention(qT, k_tok, vT, zT), _t(w_out)


def nsa_mixer(hT, tabs16, w_in, layer, w_out, pe, wk1, wk2, wv1, wv2):
    hd = NSA_HEADS * HEAD_DIM
    gd = NSA_KV_GROUPS * HEAD_DIM
    proj = functools.partial(project, hT, w_in, layer=layer)
    qT = proj((0, hd), "rope", tables=tabs16, scale=HEAD_DIM ** -0.5, scale_rows=(0, hd),
              name="nsa_proj_q")
    k_tok = proj((hd + 2 * gd, 2 * gd), "rope", col_step=2, tables=tabs16, token_major=True,
                 name="nsa_proj_k")
    plainT = proj((hd + 3 * gd, 2 * gd), "plain", col_step=2, name="nsa_proj_v")
    kc_tok = proj((hd, gd), "rope", tables=tabs16, token_major=True, name="nsa_proj_kc")
    vc_tok = proj((hd + gd, gd), "plain", token_major=True, name="nsa_proj_vc")
    g0 = hd + 6 * gd
    per = NSA_REP * 3
    wg = w_in[layer, :, g0:g0 + NSA_KV_GROUPS * per].reshape(D_MODEL, NSA_KV_GROUPS, per)
    wg = jnp.pad(wg, ((0, 0), (0, 0), (0, NSA_GATE_ROWS - per))).reshape(D_MODEL, -1)
    gatesT = project(hT, wg, (0, wg.shape[1]), "sigmoid", out_dtype=f32, name="nsa_proj_gate")
    gatesT = gatesT.reshape(NSA_KV_GROUPS, NSA_GATE_ROWS, SEQ)
    wz = w_in[layer, :, g0 + NSA_HEADS * 3:]
    zT = project(hT, wz, (0, hd), "silu", name="nsa_proj_z")
    k_cmp = nsa_compress(kc_tok, pe, wk1, wk2, transpose_out=False)
    v_cmpT = nsa_compress(vc_tok, pe, wv1, wv2, transpose_out=True)
    return nsa_attention(qT, k_tok, plainT, k_cmp, v_cmpT, gatesT, zT), _t(w_out)


def retention_mixer(hT, tabs128, w_in, layer, w_out):
    qk = 2 * RET_HEADS * RET_QK_DIM
    vd = RET_HEADS * RET_V_DIM
    proj = functools.partial(project, hT, w_in, layer=layer)
    qkT = proj((0, qk), "rope", tables=tabs128, slab=RET_QK_DIM, half=RET_QK_DIM // 2,
               scale=RET_QK_DIM ** -0.5, scale_rows=(qk // 2, qk), name="ret_proj_qk")
    vT = proj((qk, vd), "plain", name="ret_proj_v")
    gT = proj((qk + vd, vd), "silu", name="ret_proj_g")
    return retention(qkT, vT, gT), _t(w_out)


def kernel(x, c, positions, ada_w, ada_b, norm_pre, norm_post, moba_w_in, moba_w_out, nsa_w_in, nsa_w_out, nsa_cmp_pe, nsa_cmp_wk1, nsa_cmp_wk2, nsa_cmp_wv1, nsa_cmp_wv2, ret_w_in, ret_w_out):
    assert x.shape == (1, SEQ, D_MODEL)
    mod = ada_modulation(c, ada_w, ada_b)
    tabs16 = rope_tables(positions, ROPE_DIM // 2, ROPE_THETA)
    tabs128 = rope_tables(positions, RET_QK_DIM // 2, RET_ROT_THETA)
    col = lambda v: v.reshape(D_MODEL, 1)
    xT = None
    for i in range(DEPTH):
        kind, j = i % N_MIXERS, i // N_MIXERS
        shift, scale, gate = mod[i, :D_MODEL], mod[i, D_MODEL:2 * D_MODEL], mod[i, 2 * D_MODEL:]
        if i == 0:
            xT, hT = prenorm(x.reshape(SEQ, D_MODEL), col(norm_pre[i]), col(scale), col(shift),
                             token_major_in=True)
        else:
            hT = prenorm(xT, col(norm_pre[i]), col(scale), col(shift))
        if kind == 0:
            aT, w_outT = moba_mixer(hT, tabs16, moba_w_in, j, moba_w_out[j])
        elif kind == 1:
            aT, w_outT = nsa_mixer(hT, tabs16, nsa_w_in, j, nsa_w_out[j], nsa_cmp_pe[j],
                                   nsa_cmp_wk1[j], nsa_cmp_wk2[j], nsa_cmp_wv1[j], nsa_cmp_wv2[j])
        else:
            aT, w_outT = retention_mixer(hT, tabs128, ret_w_in, j, ret_w_out[j])
        xT = outproj_residual(aT, w_outT, xT, col(gate), col(norm_post[i]),
                              token_major_out=(i == DEPTH - 1))
    return xT.reshape(1, SEQ, D_MODEL)
```

```python
import functools

import numpy as np
import jax
import jax.numpy as jnp
from jax import lax
from jax.experimental import pallas as pl
from jax.experimental.pallas import tpu as pltpu

f32 = jnp.float32
bf16 = jnp.bfloat16
i32 = jnp.int32

D_MODEL = 2048
SEQ = 8192
DEPTH = 4
N_MIXERS = 3
HEAD_DIM = 128
ROPE_THETA = 500000.0
ROPE_DIM = HEAD_DIM // 4
EPS = 1e-6
NEG = -1e30
POS_BIG = 1e30

MOBA_HEADS = D_MODEL // HEAD_DIM
MOBA_BLOCK = 256
MOBA_TOPK = 3
MOBA_NB = SEQ // MOBA_BLOCK
MOBA_TQ = 1024
ONES_ROWS = 16
PROJ_TN = 512
PROJ_TM = 2048
COL_GROUP = 256

NSA_HEADS = D_MODEL // HEAD_DIM
NSA_KV_GROUPS = NSA_HEADS // 4
NSA_REP = NSA_HEADS // NSA_KV_GROUPS
NSA_CMP_LEN = 32
NSA_CMP_STRIDE = 16
NSA_CMP_HIDDEN = 256
NSA_SLC_BLOCK = 64
NSA_SLC_TOPK = 16
NSA_WINDOW = 512
NSA_NC_PAD = SEQ // NSA_CMP_STRIDE
NSA_NS = SEQ // NSA_SLC_BLOCK
NSA_GATE_ROWS = 16
NSA_TQ = 256
NSA_SPAN = 1024

RET_HEADS = 8
RET_QK_DIM = D_MODEL // RET_HEADS
RET_V_DIM = 2 * D_MODEL // RET_HEADS
RET_CHUNK = 256
RET_ROT_THETA = 10000.0
RET_HEADS_PER_STEP = 2

VMEM_LIMIT = 56 * 1024 * 1024

TN_DIMS = (((0,), (0,)), ((), ()))
NT_DIMS = (((1,), (1,)), ((), ()))


def _params(*sem):
    return pltpu.CompilerParams(dimension_semantics=sem, vmem_limit_bytes=VMEM_LIMIT)


def _rope_table_kernel(pos_ref, inv_ref, cos_ref, sin_ref):
    ang = pos_ref[...].astype(f32) * inv_ref[...]
    cos_ref[...] = jnp.cos(ang)
    sin_ref[...] = jnp.sin(ang)


def rope_tables(positions, half, theta):
    inv = (theta ** (-jnp.arange(half, dtype=f32) / half)).reshape(half, 1)
    ts = 1024
    return pl.pallas_call(
        _rope_table_kernel,
        grid=(SEQ // ts,),
        in_specs=[pl.BlockSpec((1, ts), lambda i: (0, i)),
                  pl.BlockSpec((half, 1), lambda i: (0, 0))],
        out_specs=[pl.BlockSpec((half, ts), lambda i: (0, i))] * 2,
        out_shape=[jax.ShapeDtypeStruct((half, SEQ), f32)] * 2,
        name=f"rope_tables_{half}",
    )(positions, inv)


def _ada_kernel(c_ref, w_ref, b_ref, o_ref):
    sc = jax.nn.silu(c_ref[...])
    sc8 = jnp.broadcast_to(sc, (8, D_MODEL)).astype(bf16)
    acc = jnp.dot(sc8, w_ref[0].astype(bf16), preferred_element_type=f32)
    o_ref[0] = acc[0:1] + b_ref[0]


def ada_modulation(c, ada_w, ada_b):
    tn = 1024
    n3 = 3 * D_MODEL
    out = pl.pallas_call(
        _ada_kernel,
        grid=(DEPTH, n3 // tn),
        in_specs=[pl.BlockSpec((1, D_MODEL), lambda l, j: (0, 0)),
                  pl.BlockSpec((1, D_MODEL, tn), lambda l, j: (l, 0, j)),
                  pl.BlockSpec((1, 1, tn), lambda l, j: (l, 0, j))],
        out_specs=pl.BlockSpec((1, 1, tn), lambda l, j: (l, 0, j)),
        out_shape=jax.ShapeDtypeStruct((DEPTH, 1, n3), f32),
        compiler_params=_params("arbitrary", "arbitrary"),
        name="ada_modulation",
    )(c, ada_w, ada_b.reshape(DEPTH, 1, n3))
    return out.reshape(DEPTH, n3)


def _prenorm_kernel(x_ref, g_ref, sc_ref, sh_ref, *o_refs, token_major_in):
    if token_major_in:
        xT_ref, o_ref = o_refs
        x = x_ref[...].T
        xT_ref[...] = x
    else:
        (o_ref,) = o_refs
        x = x_ref[...]
    ms = jnp.mean(x * x, axis=0, keepdims=True)
    y = (x * lax.rsqrt(ms + EPS)) * g_ref[...]
    o_ref[...] = (y * (1.0 + sc_ref[...]) + sh_ref[...]).astype(bf16)


def prenorm(x, g_col, scale_col, shift_col, token_major_in=False):
    tm = 512
    col = pl.BlockSpec((D_MODEL, 1), lambda i: (0, 0))
    fm = pl.BlockSpec((D_MODEL, tm), lambda i: (0, i))
    x_spec = pl.BlockSpec((tm, D_MODEL), lambda i: (i, 0)) if token_major_in else fm
    h_shape = jax.ShapeDtypeStruct((D_MODEL, SEQ), bf16)
    return pl.pallas_call(
        functools.partial(_prenorm_kernel, token_major_in=token_major_in),
        grid=(SEQ // tm,),
        in_specs=[x_spec, col, col, col],
        out_specs=[fm, fm] if token_major_in else fm,
        out_shape=[jax.ShapeDtypeStruct((D_MODEL, SEQ), f32), h_shape] if token_major_in else h_shape,
        compiler_params=_params("arbitrary"),
        name="prenorm",
    )(x, g_col, scale_col, shift_col)


def _proj_kernel(*refs, kind, tn, slab, half, scale, scale_lo, scale_hi, token_major):
    if kind == "rope":
        h_ref, w_ref, cos_ref, sin_ref, o_ref = refs
    else:
        h_ref, w_ref, o_ref = refs
    acc = lax.dot_general(w_ref[...].astype(bf16), h_ref[...], TN_DIMS,
                          preferred_element_type=f32)
    if scale is not None:
        j = pl.program_id(1)
        acc = acc * jnp.where((j >= scale_lo) & (j < scale_hi), scale, 1.0).astype(f32)
    if kind == "silu":
        acc = acc * jax.nn.sigmoid(acc)
    elif kind == "sigmoid":
        acc = jax.nn.sigmoid(acc)

    if kind == "rope":
        cos = cos_ref[...]
        sin = sin_ref[...]
        pieces = []
        for s in range(tn // slab):
            b = s * slab
            x1 = acc[b:b + half]
            x2 = acc[b + half:b + 2 * half]
            pieces.append(x1 * cos - x2 * sin)
            pieces.append(x1 * sin + x2 * cos)
            if 2 * half < slab:
                pieces.append(acc[b + 2 * half:b + slab])
        acc = jnp.concatenate(pieces, axis=0)
    if token_major:
        for s in range(tn // HEAD_DIM):
            o_ref[s] = acc[s * HEAD_DIM:(s + 1) * HEAD_DIM].T.astype(o_ref.dtype)
    else:
        o_ref[...] = acc.astype(o_ref.dtype)


def project(hT, w, cols, kind, *, layer=None, col_step=1, tables=None, slab=HEAD_DIM,
            half=ROPE_DIM // 2, scale=None, scale_rows=(0, 0), token_major=False,
            out_dtype=bf16, tn=PROJ_TN, name="proj"):
    start, n = cols
    tm = PROJ_TM
    tn = min(tn, n)
    first = start // tn
    assert n % tn == 0 and start % tn == 0
    assert (first + col_step * (n // tn - 1) + 1) * tn <= w.shape[-1]
    assert scale_rows[0] % tn == 0 and scale_rows[1] % tn == 0
    assert kind != "rope" or tn % slab == 0
    if layer is None:
        w_spec = pl.BlockSpec((D_MODEL, tn), lambda i, j: (0, first + col_step * j))
    else:
        w_spec = pl.BlockSpec((None, D_MODEL, tn), lambda i, j: (layer, 0, first + col_step * j))
    in_specs = [pl.BlockSpec((D_MODEL, tm), lambda i, j: (0, i)), w_spec]
    args = [hT, w]
    if kind == "rope":
        in_specs += [pl.BlockSpec((half, tm), lambda i, j: (0, i))] * 2
        args += list(tables)
    if token_major:
        out_spec = pl.BlockSpec((tn // HEAD_DIM, tm, HEAD_DIM), lambda i, j: (j, i, 0))
        out_shape = jax.ShapeDtypeStruct((n // HEAD_DIM, SEQ, HEAD_DIM), out_dtype)
    else:
        out_spec = pl.BlockSpec((tn, tm), lambda i, j: (j, i))
        out_shape = jax.ShapeDtypeStruct((n, SEQ), out_dtype)
    body = functools.partial(
        _proj_kernel, kind=kind, tn=tn, slab=slab, half=half, scale=scale,
        scale_lo=scale_rows[0] // tn, scale_hi=scale_rows[1] // tn, token_major=token_major)
    return pl.pallas_call(
        body, grid=(SEQ // tm, n // tn), in_specs=in_specs, out_specs=out_spec,
        out_shape=out_shape, compiler_params=_params("arbitrary", "arbitrary"), name=name,
    )(*args)


def _outproj_kernel(a_ref, w_ref, x_ref, gate_ref, g_ref, o_ref, *, token_major_out):
    y = jnp.dot(w_ref[...], a_ref[...], preferred_element_type=f32)
    ms = jnp.mean(y * y, axis=0, keepdims=True)
    yn = (y * lax.rsqrt(ms + EPS)) * g_ref[...]
    out = x_ref[...] + gate_ref[...] * yn
    o_ref[...] = out.T if token_major_out else out


def outproj_residual(aT, w_outT, xT, gate_col, g_col, token_major_out=False):
    k = aT.shape[0]
    tm = 256
    col = pl.BlockSpec((D_MODEL, 1), lambda i: (0, 0))
    if token_major_out:
        out_spec = pl.BlockSpec((tm, D_MODEL), lambda i: (i, 0))
        out_shape = jax.ShapeDtypeStruct((SEQ, D_MODEL), f32)
    else:
        out_spec = pl.BlockSpec((D_MODEL, tm), lambda i: (0, i))
        out_shape = jax.ShapeDtypeStruct((D_MODEL, SEQ), f32)
    return pl.pallas_call(
        functools.partial(_outproj_kernel, token_major_out=token_major_out),
        grid=(SEQ // tm,),
        in_specs=[pl.BlockSpec((k, tm), lambda i: (0, i)),
                  pl.BlockSpec((D_MODEL, k), lambda i: (0, 0)),
                  pl.BlockSpec((D_MODEL, tm), lambda i: (0, i)), col, col],
        out_specs=out_spec,
        out_shape=out_shape,
        compiler_params=_params("arbitrary"),
        name="outproj_residual",
    )(aT, w_outT, xT, gate_col, g_col)


def _first_k_select(val, idx, n_rows, k, slot_ok):
    sel = jnp.zeros(val.shape, f32)
    for t in range(k):
        m = jnp.max(val, axis=0, keepdims=True)
        first = jnp.min(jnp.where(val == m, idx, n_rows), axis=0, keepdims=True)
        pick = idx == first
        sel = jnp.where(pick, slot_ok(t, m), sel)
        val = jnp.where(pick, -jnp.inf, val)
    return sel


def _probs(s, m):
    return jnp.exp((s - m).astype(bf16))


def _flash_step(s, va, m, acc):
    mn = jnp.maximum(m, jnp.max(s, axis=0, keepdims=True))
    return mn, jnp.exp(m - mn) * acc + jnp.dot(va, _probs(s, mn), preferred_element_type=f32)


def _flash_step_ref(s_ref, va, m, acc, group=COL_GROUP):
    mns, ps = [], []
    for c in range(s_ref.shape[1] // group):
        cs = slice(c * group, (c + 1) * group)
        mn = jnp.maximum(m[:, cs], jnp.max(s_ref[:, cs], axis=0, keepdims=True))
        mns.append(mn)
        ps.append(_probs(s_ref[:, cs], mn))
    mn = jnp.concatenate(mns, axis=1)
    p = jnp.concatenate(ps, axis=1)
    return mn, jnp.exp(m - mn) * acc + jnp.dot(va, p, preferred_element_type=f32)


def _flash_pairs(k_ref, e_ref, vaug_ref, qaug_ref, sa_ref, sb_ref, pairs, carry):
    span = sa_ref.shape[0]
    last = 2 * pairs - 1

    def scores(n):
        off = pl.multiple_of(n * span, span)
        kaug = jnp.concatenate([k_ref[0, pl.ds(off, span), :], e_ref[pl.ds(off, span), :]], axis=1)
        return jnp.dot(kaug, qaug_ref[...], preferred_element_type=f32)

    def values(n):
        return vaug_ref[:, pl.ds(pl.multiple_of(n * span, span), span)]

    sa_ref[...] = scores(0)

    def body(j, carry):
        n = 2 * j
        sb_ref[...] = scores(n + 1)
        carry = _flash_step_ref(sa_ref, values(n), *carry)
        sa_ref[...] = scores(jnp.minimum(n + 2, last))
        return _flash_step_ref(sb_ref, values(n + 1), *carry)

    return lax.fori_loop(0, pairs, body, carry)


def _moba_attn_kernel(q_ref, k_ref, v_ref, z_ref, avg_ref, e_ref, o_ref,
                      kmean_ref, vaug_ref, qaug_ref, sa_ref, sb_ref):
    blk = MOBA_BLOCK
    tq = MOBA_TQ
    sub = tq // blk
    i = pl.program_id(1)
    b0 = i * sub

    @pl.when(i == 0)
    def _():
        kmean_ref[...] = jnp.dot(avg_ref[...], k_ref[0], preferred_element_type=f32)
        vaug_ref[0:HEAD_DIM, :] = v_ref[...]
        vaug_ref[HEAD_DIM:, :] = jnp.ones((ONES_ROWS, SEQ), bf16)

    qT = q_ref[...]
    gate = jnp.dot(kmean_ref[...], qT.astype(f32), precision=lax.Precision.HIGHEST,
                   preferred_element_type=f32)
    nidx = lax.broadcasted_iota(i32, (MOBA_NB, tq), 0)
    own = b0 + jnp.right_shift(lax.broadcasted_iota(i32, (1, tq), 1), blk.bit_length() - 1)
    gate = jnp.where(nidx < own, gate, NEG)
    sel = _first_k_select(gate, nidx, MOBA_NB, MOBA_TOPK, lambda t, m: jnp.where(t < own, 1.0, 0.0))
    qaug_ref[0:HEAD_DIM, :] = qT
    qaug_ref[HEAD_DIM:HEAD_DIM + MOBA_NB, :] = jnp.where(sel > 0.5, 0.0, NEG).astype(bf16)
    qaug_ref[HEAD_DIM + MOBA_NB:, :] = jnp.zeros((HEAD_DIM - MOBA_NB, tq), bf16)

    offs = [pl.multiple_of((b0 + c) * blk, blk) for c in range(sub)]
    s_own = jnp.concatenate(
        [jnp.dot(k_ref[0, pl.ds(offs[c], blk), :], qT[:, c * blk:(c + 1) * blk],
                 preferred_element_type=f32) for c in range(sub)], axis=1)
    krow = lax.broadcasted_iota(i32, (blk, tq), 0)
    qcol = lax.broadcasted_iota(i32, (blk, tq), 1) & (blk - 1)
    s_own = jnp.where(krow <= qcol, s_own, NEG)
    m = jnp.max(s_own, axis=0, keepdims=True)
    p = _probs(s_own, m)
    acc = jnp.concatenate(
        [jnp.dot(vaug_ref[:, pl.ds(offs[c], blk)], p[:, c * blk:(c + 1) * blk],
                 preferred_element_type=f32) for c in range(sub)], axis=1)

    inner = (sub - 1) * blk
    kaug = jnp.concatenate([k_ref[0, pl.ds(offs[0], inner), :], e_ref[pl.ds(offs[0], inner), :]],
                           axis=1)
    s_in = jnp.dot(kaug, qaug_ref[:, blk:], preferred_element_type=f32)
    m_hi, acc_hi = _flash_step(s_in, vaug_ref[:, pl.ds(offs[0], inner)], m[:, blk:], acc[:, blk:])
    m = jnp.concatenate([m[:, :blk], m_hi], axis=1)
    acc = jnp.concatenate([acc[:, :blk], acc_hi], axis=1)

    _, acc = _flash_pairs(k_ref, e_ref, vaug_ref, qaug_ref, sa_ref, sb_ref, i, (m, acc))
    o = acc[0:HEAD_DIM] / jnp.maximum(acc[HEAD_DIM:HEAD_DIM + 1], 1e-30)
    o_ref[...] = (o * z_ref[...].astype(f32)).astype(bf16)


def moba_attention(qT, k_tok, vT, zT):
    tq = MOBA_TQ
    h = MOBA_HEADS
    blk_of = jnp.arange(SEQ) // MOBA_BLOCK
    avg = ((blk_of[None, :] == jnp.arange(MOBA_NB)[:, None]).astype(f32) / MOBA_BLOCK).astype(bf16)
    onehot = (blk_of[:, None] == jnp.arange(HEAD_DIM)[None, :]).astype(bf16)
    return pl.pallas_call(
        _moba_attn_kernel,
        grid=(h, SEQ // tq),
        in_specs=[pl.BlockSpec((HEAD_DIM, tq), lambda hh, i: (hh, i)),
                  pl.BlockSpec((1, SEQ, HEAD_DIM), lambda hh, i: (hh, 0, 0)),
                  pl.BlockSpec((HEAD_DIM, SEQ), lambda hh, i: (hh, 0)),
                  pl.BlockSpec((HEAD_DIM, tq), lambda hh, i: (hh, i)),
                  pl.BlockSpec((MOBA_NB, SEQ), lambda hh, i: (0, 0)),
                  pl.BlockSpec((SEQ, HEAD_DIM), lambda hh, i: (0, 0))],
        out_specs=pl.BlockSpec((HEAD_DIM, tq), lambda hh, i: (hh, i)),
        out_shape=jax.ShapeDtypeStruct((h * HEAD_DIM, SEQ), bf16),
        scratch_shapes=[pltpu.VMEM((MOBA_NB, HEAD_DIM), f32),
                        pltpu.VMEM((HEAD_DIM + ONES_ROWS, SEQ), bf16),
                        pltpu.VMEM((2 * HEAD_DIM, tq), bf16),
                        pltpu.VMEM((tq // 2, tq), f32),
                        pltpu.VMEM((tq // 2, tq), f32)],
        compiler_params=_params("arbitrary", "arbitrary"),
        name="moba_attention",
    )(qT, k_tok, vT, zT, avg, onehot)


def _nsa_cmp_kernel(x_ref, wab_ref, pe_ref, w1_ref, w2_ref, o_ref, *, transpose_out):
    hid = NSA_CMP_HIDDEN
    ab = jnp.dot(x_ref[0], wab_ref[...], preferred_element_type=f32)
    first = ab[:, :hid]
    second = pltpu.roll(ab[:, hid:], NSA_NC_PAD - 1, 0)
    pe8 = jnp.broadcast_to(pe_ref[...], (8, NSA_CMP_LEN * HEAD_DIM)).astype(bf16)
    bias = jnp.dot(pe8, w1_ref[...], preferred_element_type=f32)[0:1]
    hpre = first + second + bias
    out = jnp.dot((hpre * jax.nn.sigmoid(hpre)).astype(bf16), w2_ref[...],
                  preferred_element_type=f32)
    o_ref[0] = (out.T if transpose_out else out).astype(o_ref.dtype)


def nsa_compress(t_tok, pe, w1, w2, transpose_out):
    g = NSA_KV_GROUPS
    half_k = NSA_CMP_STRIDE * HEAD_DIM
    x = t_tok.reshape(g, NSA_NC_PAD, half_k)
    w1b = w1.astype(bf16)
    wab = jnp.concatenate([w1b[:half_k], w1b[half_k:]], axis=1)
    oshape = (g, HEAD_DIM, NSA_NC_PAD) if transpose_out else (g, NSA_NC_PAD, HEAD_DIM)
    return pl.pallas_call(
        functools.partial(_nsa_cmp_kernel, transpose_out=transpose_out),
        grid=(g,),
        in_specs=[pl.BlockSpec((1, NSA_NC_PAD, half_k), lambda gg: (gg, 0, 0)),
                  pl.BlockSpec((half_k, 2 * NSA_CMP_HIDDEN), lambda gg: (0, 0)),
                  pl.BlockSpec((1, NSA_CMP_LEN * HEAD_DIM), lambda gg: (0, 0)),
                  pl.BlockSpec((NSA_CMP_LEN * HEAD_DIM, NSA_CMP_HIDDEN), lambda gg: (0, 0)),
                  pl.BlockSpec((NSA_CMP_HIDDEN, HEAD_DIM), lambda gg: (0, 0))],
        out_specs=pl.BlockSpec((1,) + oshape[1:], lambda gg: (gg, 0, 0)),
        out_shape=jax.ShapeDtypeStruct(oshape, bf16),
        compiler_params=_params("arbitrary"),
        name="nsa_compress",
    )(x, wab, pe.reshape(1, NSA_CMP_LEN * HEAD_DIM), w1b, w2.astype(bf16))


def _nsa_attn_kernel(q_ref, kc_ref, vc_ref, ov_ref, ks_ref, vs_ref, kw_ref, vw_ref, e_ref,
                     gt_ref, z_ref, o_ref, vsaug_ref, vwaug_ref, qaug_ref, sa_ref, sb_ref):
    tq = NSA_TQ
    rep = NSA_REP
    nq = rep * tq
    i = pl.program_id(1)
    q0 = i * tq

    @pl.when(i == 0)
    def _():
        for src, dst in ((vs_ref, vsaug_ref), (vw_ref, vwaug_ref)):
            dst[0:HEAD_DIM, :] = src[...]
            dst[HEAD_DIM:, :] = jnp.ones((ONES_ROWS, SEQ), bf16)

    qT = jnp.concatenate([q_ref[r * HEAD_DIM:(r + 1) * HEAD_DIM, :] for r in range(rep)],
                         axis=1)
    lane = lax.broadcasted_iota(i32, (1, nq), 1)
    qpos = q0 + (lane & (tq - 1))
    qpos1 = q0 + lax.broadcasted_iota(i32, (1, tq), 1)

    sc = jnp.dot(kc_ref[0], qT, preferred_element_type=f32)
    cend = lax.broadcasted_iota(i32, (NSA_NC_PAD, nq), 0) * NSA_CMP_STRIDE + (NSA_CMP_LEN - 1)
    ok_c = cend <= qpos
    sc = jnp.where(ok_c, sc, NEG)
    pc = jnp.where(ok_c, jnp.exp(sc - jnp.max(sc, axis=0, keepdims=True)), 0.0)
    pc = pc / jnp.maximum(jnp.sum(pc, axis=0, keepdims=True), 1e-30)
    ocT = jnp.dot(vc_ref[0], pc.astype(bf16), preferred_element_type=f32)

    ps = pc[:, 0:tq]
    for r in range(1, rep):
        ps = ps + pc[:, r * tq:(r + 1) * tq]
    ps_hi = ps.astype(bf16)
    ps_lo = (ps - ps_hi.astype(f32)).astype(bf16)
    imp = (jnp.dot(ov_ref[...], ps_hi, preferred_element_type=f32)
           + jnp.dot(ov_ref[...], ps_lo, preferred_element_type=f32))
    sidx = lax.broadcasted_iota(i32, (NSA_NS, tq), 0)
    cur = jnp.right_shift(qpos1, 6)
    valid = sidx * NSA_SLC_BLOCK <= qpos1
    forced = (sidx == 0) | (sidx == cur) | (sidx == cur - 1)
    val = jnp.where(valid, jnp.where(forced, POS_BIG, imp), NEG)
    sel = _first_k_select(val, sidx, NSA_NS, NSA_SLC_TOPK, lambda t, m: jnp.where(m > NEG / 2, 1.0, 0.0))

    bias = jnp.where(sel > 0.5, 0.0, NEG)
    qaug_ref[0:HEAD_DIM, :] = qT
    qaug_ref[HEAD_DIM:, :] = jnp.concatenate([bias] * rep, axis=1).astype(bf16)
    offd = pl.multiple_of(q0, tq)
    kaug = jnp.concatenate([ks_ref[0, pl.ds(offd, tq), :], e_ref[pl.ds(offd, tq), :]], axis=1)
    s = jnp.dot(kaug, qaug_ref[...], preferred_element_type=f32)
    s = jnp.where(q0 + lax.broadcasted_iota(i32, (tq, nq), 0) <= qpos, s, NEG)
    m = jnp.max(s, axis=0, keepdims=True)
    acc = jnp.dot(vsaug_ref[:, pl.ds(offd, tq)], _probs(s, m),
                  preferred_element_type=f32)

    own_first = i * (tq // NSA_SLC_BLOCK)
    bias_past = jnp.where(sidx >= own_first, NEG, bias)
    qaug_ref[HEAD_DIM:, :] = jnp.concatenate([bias_past] * rep, axis=1).astype(bf16)

    per_span = NSA_SPAN // tq
    pairs = jnp.right_shift(i + per_span - 1, per_span.bit_length() - 1)
    _, acc = _flash_pairs(ks_ref, e_ref, vsaug_ref, qaug_ref, sa_ref, sb_ref, pairs, (m, acc))
    osT = acc[0:HEAD_DIM] / jnp.maximum(acc[HEAD_DIM:HEAD_DIM + 1], 1e-30)

    wspan = NSA_WINDOW + tq
    start = pl.multiple_of(jnp.maximum(q0 - NSA_WINDOW, 0), tq)
    sw = jnp.dot(kw_ref[0, pl.ds(start, wspan), :], qT, preferred_element_type=f32)
    kpos = start + lax.broadcasted_iota(i32, (wspan, nq), 0)
    sw = jnp.where((kpos <= qpos) & (kpos > qpos - NSA_WINDOW), sw, NEG)
    pw = _probs(sw, jnp.max(sw, axis=0, keepdims=True))
    accw = jnp.dot(vwaug_ref[:, pl.ds(start, wspan)], pw, preferred_element_type=f32)
    owT = accw[0:HEAD_DIM] / jnp.maximum(accw[HEAD_DIM:HEAD_DIM + 1], 1e-30)

    gt = gt_ref[0]
    for r in range(rep):
        cs = slice(r * tq, (r + 1) * tq)
        rs = slice(r * HEAD_DIM, (r + 1) * HEAD_DIM)
        o = (gt[3 * r:3 * r + 1] * ocT[:, cs] + gt[3 * r + 1:3 * r + 2] * osT[:, cs]
             + gt[3 * r + 2:3 * r + 3] * owT[:, cs])
        o_ref[rs, :] = (o * z_ref[rs, :].astype(f32)).astype(bf16)


def _nsa_overlap_t():
    nc = NSA_NC_PAD - 1
    cs = np.arange(nc)[:, None] * NSA_CMP_STRIDE
    ss = np.arange(NSA_NS)[None, :] * NSA_SLC_BLOCK
    ov = np.clip(np.minimum(cs + NSA_CMP_LEN, ss + NSA_SLC_BLOCK) - np.maximum(cs, ss), 0, None)
    ov = np.concatenate([ov / NSA_CMP_LEN, np.zeros((1, NSA_NS))], axis=0)
    return jnp.asarray(ov.T, dtype=bf16)


def nsa_attention(qT, k_tok, plainT, k_cmp, v_cmpT, gatesT, zT):
    tq = NSA_TQ
    g = NSA_KV_GROUPS
    qrows = NSA_REP * HEAD_DIM
    onehot = ((jnp.arange(SEQ) // NSA_SLC_BLOCK)[:, None] == jnp.arange(NSA_NS)[None, :]).astype(bf16)
    keys = lambda first: pl.BlockSpec((1, SEQ, HEAD_DIM), lambda gg, i: (first + gg, 0, 0))
    vals = lambda first: pl.BlockSpec((HEAD_DIM, SEQ), lambda gg, i: (first + gg, 0))
    return pl.pallas_call(
        _nsa_attn_kernel,
        grid=(g, SEQ // tq),
        in_specs=[pl.BlockSpec((qrows, tq), lambda gg, i: (gg, i)),
                  pl.BlockSpec((1, NSA_NC_PAD, HEAD_DIM), lambda gg, i: (gg, 0, 0)),
                  pl.BlockSpec((1, HEAD_DIM, NSA_NC_PAD), lambda gg, i: (gg, 0, 0)),
                  pl.BlockSpec((NSA_NS, NSA_NC_PAD), lambda gg, i: (0, 0)),
                  keys(0), vals(0), keys(g), vals(g),
                  pl.BlockSpec((SEQ, NSA_NS), lambda gg, i: (0, 0)),
                  pl.BlockSpec((1, NSA_GATE_ROWS, tq), lambda gg, i: (gg, 0, i)),
                  pl.BlockSpec((qrows, tq), lambda gg, i: (gg, i))],
        out_specs=pl.BlockSpec((qrows, tq), lambda gg, i: (gg, i)),
        out_shape=jax.ShapeDtypeStruct((NSA_HEADS * HEAD_DIM, SEQ), bf16),
        scratch_shapes=[pltpu.VMEM((HEAD_DIM + ONES_ROWS, SEQ), bf16),
                        pltpu.VMEM((HEAD_DIM + ONES_ROWS, SEQ), bf16),
                        pltpu.VMEM((HEAD_DIM + NSA_NS, NSA_REP * tq), bf16),
                        pltpu.VMEM((NSA_SPAN // 2, NSA_REP * tq), f32),
                        pltpu.VMEM((NSA_SPAN // 2, NSA_REP * tq), f32)],
        compiler_params=_params("arbitrary", "arbitrary"),
        name="nsa_attention",
    )(qT, k_cmp, v_cmpT, _nsa_overlap_t(), k_tok, plainT, k_tok, plainT, onehot, gatesT, zT)


def _ret_kernel(q_ref, k_ref, v_ref, g_ref, dt_ref, rd_ref, wd_ref, cd_ref, o_ref, st_ref):
    @pl.when(pl.program_id(1) == 0)
    def _():
        st_ref[...] = jnp.zeros(st_ref.shape, f32)

    heads = range(RET_HEADS_PER_STEP)
    rows = lambda ref, p, n: ref[p * n:(p + 1) * n, :]
    qT = [rows(q_ref, p, RET_QK_DIM) for p in heads]
    kT = [rows(k_ref, p, RET_QK_DIM) for p in heads]
    vT = [rows(v_ref, p, RET_V_DIM) for p in heads]
    innerT = [lax.dot_general(kT[p], qT[p], TN_DIMS, preferred_element_type=f32) * dt_ref[p]
              for p in heads]
    st = [st_ref[p] for p in heads]
    cross = [jnp.dot(st[p].astype(bf16), qT[p], preferred_element_type=f32) * rd_ref[p]
             for p in heads]
    o = [jnp.dot(vT[p], innerT[p].astype(bf16), preferred_element_type=f32) + cross[p]
         for p in heads]
    kw = [(kT[p].astype(f32) * wd_ref[p]).astype(bf16) for p in heads]
    for p in heads:
        st_ref[p] = cd_ref[p] * st[p] + lax.dot_general(vT[p], kw[p], NT_DIMS,
                                                        preferred_element_type=f32)
    for p in heads:
        mu = jnp.mean(o[p], axis=0, keepdims=True)
        d = o[p] - mu
        var = jnp.mean(d * d, axis=0, keepdims=True)
        gate = rows(g_ref, p, RET_V_DIM).astype(f32)
        o_ref[p * RET_V_DIM:(p + 1) * RET_V_DIM, :] = (gate * (d * lax.rsqrt(var + EPS))).astype(bf16)


def retention(qkT, vT, gT):
    c = RET_CHUNK
    h = RET_HEADS
    log_gamma = jnp.log(1.0 - 2.0 ** (-5.0 - jnp.arange(h, dtype=f32)))
    i = jnp.arange(c, dtype=f32)
    diff = i[None, :] - i[:, None]
    dt = jnp.where(diff >= 0, jnp.exp(log_gamma[:, None, None] * jnp.maximum(diff, 0.0)), 0.0)
    rd = jnp.exp(log_gamma[:, None] * (i + 1.0))[:, None, :]
    wd = jnp.exp(log_gamma[:, None] * (c - 1.0 - i))[:, None, :]
    cd = jnp.broadcast_to(jnp.exp(log_gamma * c)[:, None, None], (h, 1, RET_QK_DIM))
    per = RET_HEADS_PER_STEP
    steps = h // per
    vec = lambda n: pl.BlockSpec((per, 1, n), lambda hh, t: (hh, 0, 0))
    return pl.pallas_call(
        _ret_kernel,
        grid=(steps, SEQ // c),
        in_specs=[pl.BlockSpec((per * RET_QK_DIM, c), lambda hh, t: (hh, t)),
                  pl.BlockSpec((per * RET_QK_DIM, c), lambda hh, t: (steps + hh, t)),
                  pl.BlockSpec((per * RET_V_DIM, c), lambda hh, t: (hh, t)),
                  pl.BlockSpec((per * RET_V_DIM, c), lambda hh, t: (hh, t)),
                  pl.BlockSpec((per, c, c), lambda hh, t: (hh, 0, 0)),
                  vec(c), vec(c), vec(RET_QK_DIM)],
        out_specs=pl.BlockSpec((per * RET_V_DIM, c), lambda hh, t: (hh, t)),
        out_shape=jax.ShapeDtypeStruct((h * RET_V_DIM, SEQ), bf16),
        scratch_shapes=[pltpu.VMEM((per, RET_V_DIM, RET_QK_DIM), f32)],
        compiler_params=_params("arbitrary", "arbitrary"),
        name="retention",
    )(qkT, qkT, vT, gT, dt, rd, wd, cd)


def _t(w):
    return w.T.astype(bf16)


def moba_mixer(hT, tabs16, w_in, layer, w_out):
    hd = MOBA_HEADS * HEAD_DIM
    proj = functools.partial(project, hT, w_in, layer=layer)
    qT = proj((0, hd), "rope", tables=tabs16, scale=HEAD_DIM ** -0.5, scale_rows=(0, hd),
              name="moba_proj_q")
    k_tok = proj((hd, hd), "rope", tables=tabs16, token_major=True, name="moba_proj_k")
    vT = proj((2 * hd, hd), "plain", name="moba_proj_v")
    zT = proj((3 * hd, hd), "silu", name="moba_proj_z")
    return moba_attention(qT, k_tok, vT, zT), _t(w_out)


def nsa_mixer(hT, tabs16, w_in, layer, w_out, pe, wk1, wk2, wv1, wv2):
    hd = NSA_HEADS * HEAD_DIM
    gd = NSA_KV_GROUPS * HEAD_DIM
    proj = functools.partial(project, hT, w_in, layer=layer)
    qT = proj((0, hd), "rope", tables=tabs16, scale=HEAD_DIM ** -0.5, scale_rows=(0, hd),
              name="nsa_proj_q")
    k_tok = proj((hd + 2 * gd, 2 * gd), "rope", col_step=2, tables=tabs16, token_major=True,
                 name="nsa_proj_k")
    plainT = proj((hd + 3 * gd, 2 * gd), "plain", col_step=2, name="nsa_proj_v")
    kc_tok = proj((hd, gd), "rope", tables=tabs16, token_major=True, name="nsa_proj_kc")
    vc_tok = proj((hd + gd, gd), "plain", token_major=True, name="nsa_proj_vc")
    g0 = hd + 6 * gd
    per = NSA_REP * 3
    wg = w_in[layer, :, g0:g0 + NSA_KV_GROUPS * per].reshape(D_MODEL, NSA_KV_GROUPS, per)
    wg = jnp.pad(wg, ((0, 0), (0, 0), (0, NSA_GATE_ROWS - per))).reshape(D_MODEL, -1)
    gatesT = project(hT, wg, (0, wg.shape[1]), "sigmoid", out_dtype=f32, name="nsa_proj_gate")
    gatesT = gatesT.reshape(NSA_KV_GROUPS, NSA_GATE_ROWS, SEQ)
    wz = w_in[layer, :, g0 + NSA_HEADS * 3:]
    zT = project(hT, wz, (0, hd), "silu", name="nsa_proj_z")
    k_cmp = nsa_compress(kc_tok, pe, wk1, wk2, transpose_out=False)
    v_cmpT = nsa_compress(vc_tok, pe, wv1, wv2, transpose_out=True)
    return nsa_attention(qT, k_tok, plainT, k_cmp, v_cmpT, gatesT, zT), _t(w_out)


def retention_mixer(hT, tabs128, w_in, layer, w_out):
    qk = 2 * RET_HEADS * RET_QK_DIM
    vd = RET_HEADS * RET_V_DIM
    proj = functools.partial(project, hT, w_in, layer=layer)
    qkT = proj((0, qk), "rope", tables=tabs128, slab=RET_QK_DIM, half=RET_QK_DIM // 2,
               scale=RET_QK_DIM ** -0.5, scale_rows=(qk // 2, qk), name="ret_proj_qk")
    vT = proj((qk, vd), "plain", name="ret_proj_v")
    gT = proj((qk + vd, vd), "silu", name="ret_proj_g")
    return retention(qkT, vT, gT), _t(w_out)


def kernel(x, c, positions, ada_w, ada_b, norm_pre, norm_post, moba_w_in, moba_w_out, nsa_w_in, nsa_w_out, nsa_cmp_pe, nsa_cmp_wk1, nsa_cmp_wk2, nsa_cmp_wv1, nsa_cmp_wv2, ret_w_in, ret_w_out):
    assert x.shape == (1, SEQ, D_MODEL)
    mod = ada_modulation(c, ada_w, ada_b)
    tabs16 = rope_tables(positions, ROPE_DIM // 2, ROPE_THETA)
    tabs128 = rope_tables(positions, RET_QK_DIM // 2, RET_ROT_THETA)
    col = lambda v: v.reshape(D_MODEL, 1)
    xT = None
    for i in range(DEPTH):
        kind, j = i % N_MIXERS, i // N_MIXERS
        shift, scale, gate = mod[i, :D_MODEL], mod[i, D_MODEL:2 * D_MODEL], mod[i, 2 * D_MODEL:]
        if i == 0:
            xT, hT = prenorm(x.reshape(SEQ, D_MODEL), col(norm_pre[i]), col(scale), col(shift),
                             token_major_in=True)
        else:
            hT = prenorm(xT, col(norm_pre[i]), col(scale), col(shift))
        if kind == 0:
            aT, w_outT = moba_mixer(hT, tabs16, moba_w_in, j, moba_w_out[j])
        elif kind == 1:
            aT, w_outT = nsa_mixer(hT, tabs16, nsa_w_in, j, nsa_w_out[j], nsa_cmp_pe[j],
                                   nsa_cmp_wk1[j], nsa_cmp_wk2[j], nsa_cmp_wv1[j], nsa_cmp_wv2[j])
        else:
            aT, w_outT = retention_mixer(hT, tabs128, ret_w_in, j, ret_w_out[j])
        xT = outproj_residual(aT, w_outT, xT, col(gate), col(norm_post[i]),
                              token_major_out=(i == DEPTH - 1))
    return xT.reshape(1, SEQ, D_MODEL)
```

```python
import functools

import numpy as np
import jax
import jax.numpy as jnp
from jax import lax
from jax.experimental import pallas as pl
from jax.experimental.pallas import tpu as pltpu

f32 = jnp.float32
bf16 = jnp.bfloat16
i32 = jnp.int32

D_MODEL = 2048
SEQ = 8192
DEPTH = 4
N_MIXERS = 3
HEAD_DIM = 128
ROPE_THETA = 500000.0
ROPE_DIM = HEAD_DIM // 4
EPS = 1e-6
NEG = -1e30
POS_BIG = 1e30

MOBA_HEADS = D_MODEL // HEAD_DIM
MOBA_BLOCK = 256
MOBA_TOPK = 3
MOBA_NB = SEQ // MOBA_BLOCK
MOBA_TQ = 1024
ONES_ROWS = 16
PROJ_TN = 512
PROJ_TM = 2048
COL_GROUP = 256

NSA_HEADS = D_MODEL // HEAD_DIM
NSA_KV_GROUPS = NSA_HEADS // 4
NSA_REP = NSA_HEADS // NSA_KV_GROUPS
NSA_CMP_LEN = 32
NSA_CMP_STRIDE = 16
NSA_CMP_HIDDEN = 256
NSA_SLC_BLOCK = 64
NSA_SLC_TOPK = 16
NSA_WINDOW = 512
NSA_NC_PAD = SEQ // NSA_CMP_STRIDE
NSA_NS = SEQ // NSA_SLC_BLOCK
NSA_GATE_ROWS = 16
NSA_TQ = 256
NSA_SPAN = 1024

RET_HEADS = 8
RET_QK_DIM = D_MODEL // RET_HEADS
RET_V_DIM = 2 * D_MODEL // RET_HEADS
RET_CHUNK = 256
RET_ROT_THETA = 10000.0
RET_HEADS_PER_STEP = 2

VMEM_LIMIT = 56 * 1024 * 1024

TN_DIMS = (((0,), (0,)), ((), ()))
NT_DIMS = (((1,), (1,)), ((), ()))


def _params(*sem):
    return pltpu.CompilerParams(dimension_semantics=sem, vmem_limit_bytes=VMEM_LIMIT)


def _rope_table_kernel(pos_ref, inv_ref, cos_ref, sin_ref):
    ang = pos_ref[...].astype(f32) * inv_ref[...]
    cos_ref[...] = jnp.cos(ang)
    sin_ref[...] = jnp.sin(ang)


def rope_tables(positions, half, theta):
    inv = (theta ** (-jnp.arange(half, dtype=f32) / half)).reshape(half, 1)
    ts = 1024
    return pl.pallas_call(
        _rope_table_kernel,
        grid=(SEQ // ts,),
        in_specs=[pl.BlockSpec((1, ts), lambda i: (0, i)),
                  pl.BlockSpec((half, 1), lambda i: (0, 0))],
        out_specs=[pl.BlockSpec((half, ts), lambda i: (0, i))] * 2,
        out_shape=[jax.ShapeDtypeStruct((half, SEQ), f32)] * 2,
        name=f"rope_tables_{half}",
    )(positions, inv)


def _ada_kernel(c_ref, w_ref, b_ref, o_ref):
    sc = jax.nn.silu(c_ref[...])
    sc8 = jnp.broadcast_to(sc, (8, D_MODEL)).astype(bf16)
    acc = jnp.dot(sc8, w_ref[0].astype(bf16), preferred_element_type=f32)
    o_ref[0] = acc[0:1] + b_ref[0]


def ada_modulation(c, ada_w, ada_b):
    tn = 1024
    n3 = 3 * D_MODEL
    out = pl.pallas_call(
        _ada_kernel,
        grid=(DEPTH, n3 // tn),
        in_specs=[pl.BlockSpec((1, D_MODEL), lambda l, j: (0, 0)),
                  pl.BlockSpec((1, D_MODEL, tn), lambda l, j: (l, 0, j)),
                  pl.BlockSpec((1, 1, tn), lambda l, j: (l, 0, j))],
        out_specs=pl.BlockSpec((1, 1, tn), lambda l, j: (l, 0, j)),
        out_shape=jax.ShapeDtypeStruct((DEPTH, 1, n3), f32),
        compiler_params=_params("arbitrary", "arbitrary"),
        name="ada_modulation",
    )(c, ada_w, ada_b.reshape(DEPTH, 1, n3))
    return out.reshape(DEPTH, n3)


def _prenorm_kernel(x_ref, g_ref, sc_ref, sh_ref, *o_refs, token_major_in):
    if token_major_in:
        xT_ref, o_ref = o_refs
        x = x_ref[...].T
        xT_ref[...] = x
    else:
        (o_ref,) = o_refs
        x = x_ref[...]
    ms = jnp.mean(x * x, axis=0, keepdims=True)
    y = (x * lax.rsqrt(ms + EPS)) * g_ref[...]
    o_ref[...] = (y * (1.0 + sc_ref[...]) + sh_ref[...]).astype(bf16)


def prenorm(x, g_col, scale_col, shift_col, token_major_in=False):
    tm = 512
    col = pl.BlockSpec((D_MODEL, 1), lambda i: (0, 0))
    fm = pl.BlockSpec((D_MODEL, tm), lambda i: (0, i))
    x_spec = pl.BlockSpec((tm, D_MODEL), lambda i: (i, 0)) if token_major_in else fm
    h_shape = jax.ShapeDtypeStruct((D_MODEL, SEQ), bf16)
    return pl.pallas_call(
        functools.partial(_prenorm_kernel, token_major_in=token_major_in),
        grid=(SEQ // tm,),
        in_specs=[x_spec, col, col, col],
        out_specs=[fm, fm] if token_major_in else fm,
        out_shape=[jax.ShapeDtypeStruct((D_MODEL, SEQ), f32), h_shape] if token_major_in else h_shape,
        compiler_params=_params("arbitrary"),
        name="prenorm",
    )(x, g_col, scale_col, shift_col)


def _proj_kernel(*refs, kind, tn, slab, half, scale, scale_lo, scale_hi, token_major):
    if kind == "rope":
        h_ref, w_ref, cos_ref, sin_ref, o_ref = refs
    else:
        h_ref, w_ref, o_ref = refs
    acc = lax.dot_general(w_ref[...].astype(bf16), h_ref[...], TN_DIMS,
                          preferred_element_type=f32)
    if scale is not None:
        j = pl.program_id(1)
        acc = acc * jnp.where((j >= scale_lo) & (j < scale_hi), scale, 1.0).astype(f32)
    if kind == "silu":
        acc = acc * jax.nn.sigmoid(acc)
    elif kind == "sigmoid":
        acc = jax.nn.sigmoid(acc)

    if kind == "rope":
        cos = cos_ref[...]
        sin = sin_ref[...]
        pieces = []
        for s in range(tn // slab):
            b = s * slab
            x1 = acc[b:b + half]
            x2 = acc[b + half:b + 2 * half]
            pieces.append(x1 * cos - x2 * sin)
            pieces.append(x1 * sin + x2 * cos)
            if 2 * half < slab:
                pieces.append(acc[b + 2 * half:b + slab])
        acc = jnp.concatenate(pieces, axis=0)
    if token_major:
        for s in range(tn // HEAD_DIM):
            o_ref[s] = acc[s * HEAD_DIM:(s + 1) * HEAD_DIM].T.astype(o_ref.dtype)
    else:
        o_ref[...] = acc.astype(o_ref.dtype)


def project(hT, w, cols, kind, *, layer=None, col_step=1, tables=None, slab=HEAD_DIM,
            half=ROPE_DIM // 2, scale=None, scale_rows=(0, 0), token_major=False,
            out_dtype=bf16, tn=PROJ_TN, name="proj"):
    start, n = cols
    tm = PROJ_TM
    tn = min(tn, n)
    first = start // tn
    assert n % tn == 0 and start % tn == 0
    assert (first + col_step * (n // tn - 1) + 1) * tn <= w.shape[-1]
    assert scale_rows[0] % tn == 0 and scale_rows[1] % tn == 0
    assert kind != "rope" or tn % slab == 0
    if layer is None:
        w_spec = pl.BlockSpec((D_MODEL, tn), lambda i, j: (0, first + col_step * j))
    else:
        w_spec = pl.BlockSpec((None, D_MODEL, tn), lambda i, j: (layer, 0, first + col_step * j))
    in_specs = [pl.BlockSpec((D_MODEL, tm), lambda i, j: (0, i)), w_spec]
    args = [hT, w]
    if kind == "rope":
        in_specs += [pl.BlockSpec((half, tm), lambda i, j: (0, i))] * 2
        args += list(tables)
    if token_major:
        out_spec = pl.BlockSpec((tn // HEAD_DIM, tm, HEAD_DIM), lambda i, j: (j, i, 0))
        out_shape = jax.ShapeDtypeStruct((n // HEAD_DIM, SEQ, HEAD_DIM), out_dtype)
    else:
        out_spec = pl.BlockSpec((tn, tm), lambda i, j: (j, i))
        out_shape = jax.ShapeDtypeStruct((n, SEQ), out_dtype)
    body = functools.partial(
        _proj_kernel, kind=kind, tn=tn, slab=slab, half=half, scale=scale,
        scale_lo=scale_rows[0] // tn, scale_hi=scale_rows[1] // tn, token_major=token_major)
    return pl.pallas_call(
        body, grid=(SEQ // tm, n // tn), in_specs=in_specs, out_specs=out_spec,
        out_shape=out_shape, compiler_params=_params("arbitrary", "arbitrary"), name=name,
    )(*args)


def _outproj_kernel(a_ref, w_ref, x_ref, gate_ref, g_ref, o_ref, *, token_major_out):
    y = jnp.dot(w_ref[...], a_ref[...], preferred_element_type=f32)
    ms = jnp.mean(y * y, axis=0, keepdims=True)
    yn = (y * lax.rsqrt(ms + EPS)) * g_ref[...]
    out = x_ref[...] + gate_ref[...] * yn
    o_ref[...] = out.T if token_major_out else out


def outproj_residual(aT, w_outT, xT, gate_col, g_col, token_major_out=False):
    k = aT.shape[0]
    tm = 256
    col = pl.BlockSpec((D_MODEL, 1), lambda i: (0, 0))
    if token_major_out:
        out_spec = pl.BlockSpec((tm, D_MODEL), lambda i: (i, 0))
        out_shape = jax.ShapeDtypeStruct((SEQ, D_MODEL), f32)
    else:
        out_spec = pl.BlockSpec((D_MODEL, tm), lambda i: (0, i))
        out_shape = jax.ShapeDtypeStruct((D_MODEL, SEQ), f32)
    return pl.pallas_call(
        functools.partial(_outproj_kernel, token_major_out=token_major_out),
        grid=(SEQ // tm,),
        in_specs=[pl.BlockSpec((k, tm), lambda i: (0, i)),
                  pl.BlockSpec((D_MODEL, k), lambda i: (0, 0)),
                  pl.BlockSpec((D_MODEL, tm), lambda i: (0, i)), col, col],
        out_specs=out_spec,
        out_shape=out_shape,
        compiler_params=_params("arbitrary"),
        name="outproj_residual",
    )(aT, w_outT, xT, gate_col, g_col)


def _first_k_select(val, idx, n_rows, k, slot_ok):
    sel = jnp.zeros(val.shape, f32)
    for t in range(k):
        m = jnp.max(val, axis=0, keepdims=True)
        first = jnp.min(jnp.where(val == m, idx, n_rows), axis=0, keepdims=True)
        pick = idx == first
        sel = jnp.where(pick, slot_ok(t, m), sel)
        val = jnp.where(pick, -jnp.inf, val)
    return sel


def _probs(s, m):
    return jnp.exp((s - m).astype(bf16))


def _flash_step(s, va, m, acc):
    mn = jnp.maximum(m, jnp.max(s, axis=0, keepdims=True))
    return mn, jnp.exp(m - mn) * acc + jnp.dot(va, _probs(s, mn), preferred_element_type=f32)


def _flash_step_ref(s_ref, va, m, acc, group=COL_GROUP):
    mns, ps = [], []
    for c in range(s_ref.shape[1] // group):
        cs = slice(c * group, (c + 1) * group)
        mn = jnp.maximum(m[:, cs], jnp.max(s_ref[:, cs], axis=0, keepdims=True))
        mns.append(mn)
        ps.append(_probs(s_ref[:, cs], mn))
    mn = jnp.concatenate(mns, axis=1)
    p = jnp.concatenate(ps, axis=1)
    return mn, jnp.exp(m - mn) * acc + jnp.dot(va, p, preferred_element_type=f32)


def _flash_pairs(k_ref, e_ref, vaug_ref, qaug_ref, sa_ref, sb_ref, pairs, carry):
    span = sa_ref.shape[0]

    def scores(n):
        off = pl.multiple_of(n * span, span)
        kaug = jnp.concatenate([k_ref[0, pl.ds(off, span), :], e_ref[pl.ds(off, span), :]], axis=1)
        return jnp.dot(kaug, qaug_ref[...], preferred_element_type=f32)

    def values(n):
        return vaug_ref[:, pl.ds(pl.multiple_of(n * span, span), span)]

    def pair(j, carry, prefetch):
        n = 2 * j
        sb_ref[...] = scores(n + 1)
        carry = _flash_step_ref(sa_ref, values(n), *carry)
        if prefetch:
            sa_ref[...] = scores(n + 2)
        return _flash_step_ref(sb_ref, values(n + 1), *carry)

    def run(carry):
        sa_ref[...] = scores(0)
        carry = lax.fori_loop(0, pairs - 1, lambda j, c: pair(j, c, True), carry)
        return pair(pairs - 1, carry, False)

    return lax.cond(pairs > 0, run, lambda c: c, carry)


def _moba_attn_kernel(q_ref, k_ref, v_ref, z_ref, avg_ref, e_ref, o_ref,
                      kmean_ref, vaug_ref, qaug_ref, sa_ref, sb_ref):
    blk = MOBA_BLOCK
    tq = MOBA_TQ
    sub = tq // blk
    i = pl.program_id(1)
    b0 = i * sub

    @pl.when(i == 0)
    def _():
        kmean_ref[...] = jnp.dot(avg_ref[...], k_ref[0], preferred_element_type=f32)
        vaug_ref[0:HEAD_DIM, :] = v_ref[...]
        vaug_ref[HEAD_DIM:, :] = jnp.ones((ONES_ROWS, SEQ), bf16)

    qT = q_ref[...]
    gate = jnp.dot(kmean_ref[...], qT.astype(f32), precision=lax.Precision.HIGHEST,
                   preferred_element_type=f32)
    nidx = lax.broadcasted_iota(i32, (MOBA_NB, tq), 0)
    own = b0 + jnp.right_shift(lax.broadcasted_iota(i32, (1, tq), 1), blk.bit_length() - 1)
    gate = jnp.where(nidx < own, gate, NEG)
    sel = _first_k_select(gate, nidx, MOBA_NB, MOBA_TOPK, lambda t, m: jnp.where(t < own, 1.0, 0.0))
    qaug_ref[0:HEAD_DIM, :] = qT
    qaug_ref[HEAD_DIM:HEAD_DIM + MOBA_NB, :] = jnp.where(sel > 0.5, 0.0, NEG).astype(bf16)
    qaug_ref[HEAD_DIM + MOBA_NB:, :] = jnp.zeros((HEAD_DIM - MOBA_NB, tq), bf16)

    offs = [pl.multiple_of((b0 + c) * blk, blk) for c in range(sub)]
    s_own = jnp.concatenate(
        [jnp.dot(k_ref[0, pl.ds(offs[c], blk), :], qT[:, c * blk:(c + 1) * blk],
                 preferred_element_type=f32) for c in range(sub)], axis=1)
    krow = lax.broadcasted_iota(i32, (blk, tq), 0)
    qcol = lax.broadcasted_iota(i32, (blk, tq), 1) & (blk - 1)
    s_own = jnp.where(krow <= qcol, s_own, NEG)
    m = jnp.max(s_own, axis=0, keepdims=True)
    p = _probs(s_own, m)
    acc = jnp.concatenate(
        [jnp.dot(vaug_ref[:, pl.ds(offs[c], blk)], p[:, c * blk:(c + 1) * blk],
                 preferred_element_type=f32) for c in range(sub)], axis=1)

    inner = (sub - 1) * blk
    kaug = jnp.concatenate([k_ref[0, pl.ds(offs[0], inner), :], e_ref[pl.ds(offs[0], inner), :]],
                           axis=1)
    s_in = jnp.dot(kaug, qaug_ref[:, blk:], preferred_element_type=f32)
    m_hi, acc_hi = _flash_step(s_in, vaug_ref[:, pl.ds(offs[0], inner)], m[:, blk:], acc[:, blk:])
    m = jnp.concatenate([m[:, :blk], m_hi], axis=1)
    acc = jnp.concatenate([acc[:, :blk], acc_hi], axis=1)

    _, acc = _flash_pairs(k_ref, e_ref, vaug_ref, qaug_ref, sa_ref, sb_ref, i, (m, acc))
    o = acc[0:HEAD_DIM] / jnp.maximum(acc[HEAD_DIM:HEAD_DIM + 1], 1e-30)
    o_ref[...] = (o * z_ref[...].astype(f32)).astype(bf16)


def moba_attention(qT, k_tok, vT, zT):
    tq = MOBA_TQ
    h = MOBA_HEADS
    blk_of = jnp.arange(SEQ) // MOBA_BLOCK
    avg = ((blk_of[None, :] == jnp.arange(MOBA_NB)[:, None]).astype(f32) / MOBA_BLOCK).astype(bf16)
    onehot = (blk_of[:, None] == jnp.arange(HEAD_DIM)[None, :]).astype(bf16)
    return pl.pallas_call(
        _moba_attn_kernel,
        grid=(h, SEQ // tq),
        in_specs=[pl.BlockSpec((HEAD_DIM, tq), lambda hh, i: (hh, i)),
                  pl.BlockSpec((1, SEQ, HEAD_DIM), lambda hh, i: (hh, 0, 0)),
                  pl.BlockSpec((HEAD_DIM, SEQ), lambda hh, i: (hh, 0)),
                  pl.BlockSpec((HEAD_DIM, tq), lambda hh, i: (hh, i)),
                  pl.BlockSpec((MOBA_NB, SEQ), lambda hh, i: (0, 0)),
                  pl.BlockSpec((SEQ, HEAD_DIM), lambda hh, i: (0, 0))],
        out_specs=pl.BlockSpec((HEAD_DIM, tq), lambda hh, i: (hh, i)),
        out_shape=jax.ShapeDtypeStruct((h * HEAD_DIM, SEQ), bf16),
        scratch_shapes=[pltpu.VMEM((MOBA_NB, HEAD_DIM), f32),
                        pltpu.VMEM((HEAD_DIM + ONES_ROWS, SEQ), bf16),
                        pltpu.VMEM((2 * HEAD_DIM, tq), bf16),
                        pltpu.VMEM((tq // 2, tq), f32),
                        pltpu.VMEM((tq // 2, tq), f32)],
        compiler_params=_params("arbitrary", "arbitrary"),
        name="moba_attention",
    )(qT, k_tok, vT, zT, avg, onehot)


def _nsa_cmp_kernel(x_ref, wab_ref, pe_ref, w1_ref, w2_ref, o_ref, *, transpose_out):
    hid = NSA_CMP_HIDDEN
    ab = jnp.dot(x_ref[0], wab_ref[...], preferred_element_type=f32)
    first = ab[:, :hid]
    second = pltpu.roll(ab[:, hid:], NSA_NC_PAD - 1, 0)
    pe8 = jnp.broadcast_to(pe_ref[...], (8, NSA_CMP_LEN * HEAD_DIM)).astype(bf16)
    bias = jnp.dot(pe8, w1_ref[...], preferred_element_type=f32)[0:1]
    hpre = first + second + bias
    out = jnp.dot((hpre * jax.nn.sigmoid(hpre)).astype(bf16), w2_ref[...],
                  preferred_element_type=f32)
    o_ref[0] = (out.T if transpose_out else out).astype(o_ref.dtype)


def nsa_compress(t_tok, pe, w1, w2, transpose_out):
    g = NSA_KV_GROUPS
    half_k = NSA_CMP_STRIDE * HEAD_DIM
    x = t_tok.reshape(g, NSA_NC_PAD, half_k)
    w1b = w1.astype(bf16)
    wab = jnp.concatenate([w1b[:half_k], w1b[half_k:]], axis=1)
    oshape = (g, HEAD_DIM, NSA_NC_PAD) if transpose_out else (g, NSA_NC_PAD, HEAD_DIM)
    return pl.pallas_call(
        functools.partial(_nsa_cmp_kernel, transpose_out=transpose_out),
        grid=(g,),
        in_specs=[pl.BlockSpec((1, NSA_NC_PAD, half_k), lambda gg: (gg, 0, 0)),
                  pl.BlockSpec((half_k, 2 * NSA_CMP_HIDDEN), lambda gg: (0, 0)),
                  pl.BlockSpec((1, NSA_CMP_LEN * HEAD_DIM), lambda gg: (0, 0)),
                  pl.BlockSpec((NSA_CMP_LEN * HEAD_DIM, NSA_CMP_HIDDEN), lambda gg: (0, 0)),
                  pl.BlockSpec((NSA_CMP_HIDDEN, HEAD_DIM), lambda gg: (0, 0))],
        out_specs=pl.BlockSpec((1,) + oshape[1:], lambda gg: (gg, 0, 0)),
        out_shape=jax.ShapeDtypeStruct(oshape, bf16),
        compiler_params=_params("arbitrary"),
        name="nsa_compress",
    )(x, wab, pe.reshape(1, NSA_CMP_LEN * HEAD_DIM), w1b, w2.astype(bf16))


def _nsa_attn_kernel(q_ref, kc_ref, vc_ref, ov_ref, ks_ref, vs_ref, kw_ref, vw_ref, e_ref,
                     gt_ref, z_ref, o_ref, vsaug_ref, vwaug_ref, qaug_ref, sa_ref, sb_ref):
    tq = NSA_TQ
    rep = NSA_REP
    nq = rep * tq
    i = pl.program_id(1)
    q0 = i * tq

    @pl.when(i == 0)
    def _():
        for src, dst in ((vs_ref, vsaug_ref), (vw_ref, vwaug_ref)):
            dst[0:HEAD_DIM, :] = src[...]
            dst[HEAD_DIM:, :] = jnp.ones((ONES_ROWS, SEQ), bf16)

    qT = jnp.concatenate([q_ref[r * HEAD_DIM:(r + 1) * HEAD_DIM, :] for r in range(rep)],
                         axis=1)
    lane = lax.broadcasted_iota(i32, (1, nq), 1)
    qpos = q0 + (lane & (tq - 1))
    qpos1 = q0 + lax.broadcasted_iota(i32, (1, tq), 1)

    sc = jnp.dot(kc_ref[0], qT, preferred_element_type=f32)
    cend = lax.broadcasted_iota(i32, (NSA_NC_PAD, nq), 0) * NSA_CMP_STRIDE + (NSA_CMP_LEN - 1)
    ok_c = cend <= qpos
    sc = jnp.where(ok_c, sc, NEG)
    pc = jnp.where(ok_c, jnp.exp(sc - jnp.max(sc, axis=0, keepdims=True)), 0.0)
    pc = pc / jnp.maximum(jnp.sum(pc, axis=0, keepdims=True), 1e-30)
    ocT = jnp.dot(vc_ref[0], pc.astype(bf16), preferred_element_type=f32)

    ps = pc[:, 0:tq]
    for r in range(1, rep):
        ps = ps + pc[:, r * tq:(r + 1) * tq]
    ps_hi = ps.astype(bf16)
    ps_lo = (ps - ps_hi.astype(f32)).astype(bf16)
    imp = (jnp.dot(ov_ref[...], ps_hi, preferred_element_type=f32)
           + jnp.dot(ov_ref[...], ps_lo, preferred_element_type=f32))
    sidx = lax.broadcasted_iota(i32, (NSA_NS, tq), 0)
    cur = jnp.right_shift(qpos1, 6)
    valid = sidx * NSA_SLC_BLOCK <= qpos1
    forced = (sidx == 0) | (sidx == cur) | (sidx == cur - 1)
    val = jnp.where(valid, jnp.where(forced, POS_BIG, imp), NEG)
    sel = _first_k_select(val, sidx, NSA_NS, NSA_SLC_TOPK, lambda t, m: jnp.where(m > NEG / 2, 1.0, 0.0))

    bias = jnp.where(sel > 0.5, 0.0, NEG)
    qaug_ref[0:HEAD_DIM, :] = qT
    qaug_ref[HEAD_DIM:, :] = jnp.concatenate([bias] * rep, axis=1).astype(bf16)
    offd = pl.multiple_of(q0, tq)
    kaug = jnp.concatenate([ks_ref[0, pl.ds(offd, tq), :], e_ref[pl.ds(offd, tq), :]], axis=1)
    s = jnp.dot(kaug, qaug_ref[...], preferred_element_type=f32)
    s = jnp.where(q0 + lax.broadcasted_iota(i32, (tq, nq), 0) <= qpos, s, NEG)
    m = jnp.max(s, axis=0, keepdims=True)
    acc = jnp.dot(vsaug_ref[:, pl.ds(offd, tq)], _probs(s, m),
                  preferred_element_type=f32)

    own_first = i * (tq // NSA_SLC_BLOCK)
    bias_past = jnp.where(sidx >= own_first, NEG, bias)
    qaug_ref[HEAD_DIM:, :] = jnp.concatenate([bias_past] * rep, axis=1).astype(bf16)

    per_span = NSA_SPAN // tq
    pairs = jnp.right_shift(i + per_span - 1, per_span.bit_length() - 1)
    _, acc = _flash_pairs(ks_ref, e_ref, vsaug_ref, qaug_ref, sa_ref, sb_ref, pairs, (m, acc))
    osT = acc[0:HEAD_DIM] / jnp.maximum(acc[HEAD_DIM:HEAD_DIM + 1], 1e-30)

    wspan = NSA_WINDOW + tq
    start = pl.multiple_of(jnp.maximum(q0 - NSA_WINDOW, 0), tq)
    sw = jnp.dot(kw_ref[0, pl.ds(start, wspan), :], qT, preferred_element_type=f32)
    kpos = start + lax.broadcasted_iota(i32, (wspan, nq), 0)
    sw = jnp.where((kpos <= qpos) & (kpos > qpos - NSA_WINDOW), sw, NEG)
    pw = _probs(sw, jnp.max(sw, axis=0, keepdims=True))
    accw = jnp.dot(vwaug_ref[:, pl.ds(start, wspan)], pw, preferred_element_type=f32)
    owT = accw[0:HEAD_DIM] / jnp.maximum(accw[HEAD_DIM:HEAD_DIM + 1], 1e-30)

    gt = gt_ref[0]
    for r in range(rep):
        cs = slice(r * tq, (r + 1) * tq)
        rs = slice(r * HEAD_DIM, (r + 1) * HEAD_DIM)
        o = (gt[3 * r:3 * r + 1] * ocT[:, cs] + gt[3 * r + 1:3 * r + 2] * osT[:, cs]
             + gt[3 * r + 2:3 * r + 3] * owT[:, cs])
        o_ref[rs, :] = (o * z_ref[rs, :].astype(f32)).astype(bf16)


def _nsa_overlap_t():
    nc = NSA_NC_PAD - 1
    cs = np.arange(nc)[:, None] * NSA_CMP_STRIDE
    ss = np.arange(NSA_NS)[None, :] * NSA_SLC_BLOCK
    ov = np.clip(np.minimum(cs + NSA_CMP_LEN, ss + NSA_SLC_BLOCK) - np.maximum(cs, ss), 0, None)
    ov = np.concatenate([ov / NSA_CMP_LEN, np.zeros((1, NSA_NS))], axis=0)
    return jnp.asarray(ov.T, dtype=bf16)


def nsa_attention(qT, k_tok, plainT, k_cmp, v_cmpT, gatesT, zT):
    tq = NSA_TQ
    g = NSA_KV_GROUPS
    qrows = NSA_REP * HEAD_DIM
    onehot = ((jnp.arange(SEQ) // NSA_SLC_BLOCK)[:, None] == jnp.arange(NSA_NS)[None, :]).astype(bf16)
    keys = lambda first: pl.BlockSpec((1, SEQ, HEAD_DIM), lambda gg, i: (first + gg, 0, 0))
    vals = lambda first: pl.BlockSpec((HEAD_DIM, SEQ), lambda gg, i: (first + gg, 0))
    return pl.pallas_call(
        _nsa_attn_kernel,
        grid=(g, SEQ // tq),
        in_specs=[pl.BlockSpec((qrows, tq), lambda gg, i: (gg, i)),
                  pl.BlockSpec((1, NSA_NC_PAD, HEAD_DIM), lambda gg, i: (gg, 0, 0)),
                  pl.BlockSpec((1, HEAD_DIM, NSA_NC_PAD), lambda gg, i: (gg, 0, 0)),
                  pl.BlockSpec((NSA_NS, NSA_NC_PAD), lambda gg, i: (0, 0)),
                  keys(0), vals(0), keys(g), vals(g),
                  pl.BlockSpec((SEQ, NSA_NS), lambda gg, i: (0, 0)),
                  pl.BlockSpec((1, NSA_GATE_ROWS, tq), lambda gg, i: (gg, 0, i)),
                  pl.BlockSpec((qrows, tq), lambda gg, i: (gg, i))],
        out_specs=pl.BlockSpec((qrows, tq), lambda gg, i: (gg, i)),
        out_shape=jax.ShapeDtypeStruct((NSA_HEADS * HEAD_DIM, SEQ), bf16),
        scratch_shapes=[pltpu.VMEM((HEAD_DIM + ONES_ROWS, SEQ), bf16),
                        pltpu.VMEM((HEAD_DIM + ONES_ROWS, SEQ), bf16),
                        pltpu.VMEM((HEAD_DIM + NSA_NS, NSA_REP * tq), bf16),
                        pltpu.VMEM((NSA_SPAN // 2, NSA_REP * tq), f32),
                        pltpu.VMEM((NSA_SPAN // 2, NSA_REP * tq), f32)],
        compiler_params=_params("arbitrary", "arbitrary"),
        name="nsa_attention",
    )(qT, k_cmp, v_cmpT, _nsa_overlap_t(), k_tok, plainT, k_tok, plainT, onehot, gatesT, zT)


def _ret_kernel(q_ref, k_ref, v_ref, g_ref, dt_ref, rd_ref, wd_ref, cd_ref, o_ref, st_ref):
    @pl.when(pl.program_id(1) == 0)
    def _():
        st_ref[...] = jnp.zeros(st_ref.shape, f32)

    heads = range(RET_HEADS_PER_STEP)
    rows = lambda ref, p, n: ref[p * n:(p + 1) * n, :]
    qT = [rows(q_ref, p, RET_QK_DIM) for p in heads]
    kT = [rows(k_ref, p, RET_QK_DIM) for p in heads]
    vT = [rows(v_ref, p, RET_V_DIM) for p in heads]
    innerT = [lax.dot_general(kT[p], qT[p], TN_DIMS, preferred_element_type=f32) * dt_ref[p]
              for p in heads]
    st = [st_ref[p] for p in heads]
    cross = [jnp.dot(st[p].astype(bf16), qT[p], preferred_element_type=f32) * rd_ref[p]
             for p in heads]
    o = [jnp.dot(vT[p], innerT[p].astype(bf16), preferred_element_type=f32) + cross[p]
         for p in heads]
    kw = [(kT[p].astype(f32) * wd_ref[p]).astype(bf16) for p in heads]
    for p in heads:
        st_ref[p] = cd_ref[p] * st[p] + lax.dot_general(vT[p], kw[p], NT_DIMS,
                                                        preferred_element_type=f32)
    for p in heads:
        mu = jnp.mean(o[p], axis=0, keepdims=True)
        d = o[p] - mu
        var = jnp.mean(d * d, axis=0, keepdims=True)
        gate = rows(g_ref, p, RET_V_DIM).astype(f32)
        o_ref[p * RET_V_DIM:(p + 1) * RET_V_DIM, :] = (gate * (d * lax.rsqrt(var + EPS))).astype(bf16)


def retention(qkT, vT, gT):
    c = RET_CHUNK
    h = RET_HEADS
    log_gamma = jnp.log(1.0 - 2.0 ** (-5.0 - jnp.arange(h, dtype=f32)))
    i = jnp.arange(c, dtype=f32)
    diff = i[None, :] - i[:, None]
    dt = jnp.where(diff >= 0, jnp.exp(log_gamma[:, None, None] * jnp.maximum(diff, 0.0)), 0.0)
    rd = jnp.exp(log_gamma[:, None] * (i + 1.0))[:, None, :]
    wd = jnp.exp(log_gamma[:, None] * (c - 1.0 - i))[:, None, :]
    cd = jnp.broadcast_to(jnp.exp(log_gamma * c)[:, None, None], (h, 1, RET_QK_DIM))
    per = RET_HEADS_PER_STEP
    steps = h // per
    vec = lambda n: pl.BlockSpec((per, 1, n), lambda hh, t: (hh, 0, 0))
    return pl.pallas_call(
        _ret_kernel,
        grid=(steps, SEQ // c),
        in_specs=[pl.BlockSpec((per * RET_QK_DIM, c), lambda hh, t: (hh, t)),
                  pl.BlockSpec((per * RET_QK_DIM, c), lambda hh, t: (steps + hh, t)),
                  pl.BlockSpec((per * RET_V_DIM, c), lambda hh, t: (hh, t)),
                  pl.BlockSpec((per * RET_V_DIM, c), lambda hh, t: (hh, t)),
                  pl.BlockSpec((per, c, c), lambda hh, t: (hh, 0, 0)),
                  vec(c), vec(c), vec(RET_QK_DIM)],
        out_specs=pl.BlockSpec((per * RET_V_DIM, c), lambda hh, t: (hh, t)),
        out_shape=jax.ShapeDtypeStruct((h * RET_V_DIM, SEQ), bf16),
        scratch_shapes=[pltpu.VMEM((per, RET_V_DIM, RET_QK_DIM), f32)],
        compiler_params=_params("arbitrary", "arbitrary"),
        name="retention",
    )(qkT, qkT, vT, gT, dt, rd, wd, cd)


def _t(w):
    return w.T.astype(bf16)


def moba_mixer(hT, tabs16, w_in, layer, w_out):
    hd = MOBA_HEADS * HEAD_DIM
    proj = functools.partial(project, hT, w_in, layer=layer)
    qT = proj((0, hd), "rope", tables=tabs16, scale=HEAD_DIM ** -0.5, scale_rows=(0, hd),
              name="moba_proj_q")
    k_tok = proj((hd, hd), "rope", tables=tabs16, token_major=True, name="moba_proj_k")
    vT = proj((2 * hd, hd), "plain", name="moba_proj_v")
    zT = proj((3 * hd, hd), "silu", name="moba_proj_z")
    return moba_attention(qT, k_tok, vT, zT), _t(w_out)


def nsa_mixer(hT, tabs16, w_in, layer, w_out, pe, wk1, wk2, wv1, wv2):
    hd = NSA_HEADS * HEAD_DIM
    gd = NSA_KV_GROUPS * HEAD_DIM
    proj = functools.partial(project, hT, w_in, layer=layer)
    qT = proj((0, hd), "rope", tables=tabs16, scale=HEAD_DIM ** -0.5, scale_rows=(0, hd),
              name="nsa_proj_q")
    k_tok = proj((hd + 2 * gd, 2 * gd), "rope", col_step=2, tables=tabs16, token_major=True,
                 name="nsa_proj_k")
    plainT = proj((hd + 3 * gd, 2 * gd), "plain", col_step=2, name="nsa_proj_v")
    kc_tok = proj((hd, gd), "rope", tables=tabs16, token_major=True, name="nsa_proj_kc")
    vc_tok = proj((hd + gd, gd), "plain", token_major=True, name="nsa_proj_vc")
    g0 = hd + 6 * gd
    per = NSA_REP * 3
    wg = w_in[layer, :, g0:g0 + NSA_KV_GROUPS * per].reshape(D_MODEL, NSA_KV_GROUPS, per)
    wg = jnp.pad(wg, ((0, 0), (0, 0), (0, NSA_GATE_ROWS - per))).reshape(D_MODEL, -1)
    gatesT = project(hT, wg, (0, wg.shape[1]), "sigmoid", out_dtype=f32, name="nsa_proj_gate")
    gatesT = gatesT.reshape(NSA_KV_GROUPS, NSA_GATE_ROWS, SEQ)
    wz = w_in[layer, :, g0 + NSA_HEADS * 3:]
    zT = project(hT, wz, (0, hd), "silu", name="nsa_proj_z")
    k_cmp = nsa_compress(kc_tok, pe, wk1, wk2, transpose_out=False)
    v_cmpT = nsa_compress(vc_tok, pe, wv1, wv2, transpose_out=True)
    return nsa_attention(qT, k_tok, plainT, k_cmp, v_cmpT, gatesT, zT), _t(w_out)


def retention_mixer(hT, tabs128, w_in, layer, w_out):
    qk = 2 * RET_HEADS * RET_QK_DIM
    vd = RET_HEADS * RET_V_DIM
    proj = functools.partial(project, hT, w_in, layer=layer)
    qkT = proj((0, qk), "rope", tables=tabs128, slab=RET_QK_DIM, half=RET_QK_DIM // 2,
               scale=RET_QK_DIM ** -0.5, scale_rows=(qk // 2, qk), name="ret_proj_qk")
    vT = proj((qk, vd), "plain", name="ret_proj_v")
    gT = proj((qk + vd, vd), "silu", name="ret_proj_g")
    return retention(qkT, vT, gT), _t(w_out)


def kernel(x, c, positions, ada_w, ada_b, norm_pre, norm_post, moba_w_in, moba_w_out, nsa_w_in, nsa_w_out, nsa_cmp_pe, nsa_cmp_wk1, nsa_cmp_wk2, nsa_cmp_wv1, nsa_cmp_wv2, ret_w_in, ret_w_out):
    assert x.shape == (1, SEQ, D_MODEL)
    mod = ada_modulation(c, ada_w, ada_b)
    tabs16 = rope_tables(positions, ROPE_DIM // 2, ROPE_THETA)
    tabs128 = rope_tables(positions, RET_QK_DIM // 2, RET_ROT_THETA)
    col = lambda v: v.reshape(D_MODEL, 1)
    shift = lambda i: col(mod[i, :D_MODEL])
    scale = lambda i: col(mod[i, D_MODEL:2 * D_MODEL])
    gate = lambda i: col(mod[i, 2 * D_MODEL:])
    xT, hT = prenorm(x.reshape(SEQ, D_MODEL), col(norm_pre[0]), scale(0), shift(0),
                     token_major_in=True)
    for i in range(DEPTH):
        kind, j = i % N_MIXERS, i // N_MIXERS
        if kind == 0:
            aT, w_outT = moba_mixer(hT, tabs16, moba_w_in, j, moba_w_out[j])
        elif kind == 1:
            aT, w_outT = nsa_mixer(hT, tabs16, nsa_w_in, j, nsa_w_out[j], nsa_cmp_pe[j],
                                   nsa_cmp_wk1[j], nsa_cmp_wk2[j], nsa_cmp_wv1[j], nsa_cmp_wv2[j])
        else:
            aT, w_outT = retention_mixer(hT, tabs128, ret_w_in, j, ret_w_out[j])
        if i + 1 < DEPTH:
            xT = outproj_residual(aT, w_outT, xT, gate(i), col(norm_post[i]))
            hT = prenorm(xT, col(norm_pre[i + 1]), scale(i + 1), shift(i + 1))
        else:
            xT = outproj_residual(aT, w_outT, xT, gate(i), col(norm_post[i]), token_major_out=True)
    return xT.reshape(1, SEQ, D_MODEL)
```

```python
import functools

import numpy as np
import jax
import jax.numpy as jnp
from jax import lax
from jax.experimental import pallas as pl
from jax.experimental.pallas import tpu as pltpu

f32 = jnp.float32
bf16 = jnp.bfloat16
i32 = jnp.int32

D_MODEL = 2048
SEQ = 8192
DEPTH = 4
N_MIXERS = 3
HEAD_DIM = 128
ROPE_THETA = 500000.0
ROPE_DIM = HEAD_DIM // 4
EPS = 1e-6
NEG = -1e30
POS_BIG = 1e30

MOBA_HEADS = D_MODEL // HEAD_DIM
MOBA_BLOCK = 256
MOBA_TOPK = 3
MOBA_NB = SEQ // MOBA_BLOCK
MOBA_TQ = 1024
ONES_ROWS = 16
PROJ_TN = 512
PROJ_TM = 2048
RING = 4
RING_AHEAD = 2
COL_GROUP = 256

NSA_HEADS = D_MODEL // HEAD_DIM
NSA_KV_GROUPS = NSA_HEADS // 4
NSA_REP = NSA_HEADS // NSA_KV_GROUPS
NSA_CMP_LEN = 32
NSA_CMP_STRIDE = 16
NSA_CMP_HIDDEN = 256
NSA_SLC_BLOCK = 64
NSA_SLC_TOPK = 16
NSA_WINDOW = 512
NSA_NC_PAD = SEQ // NSA_CMP_STRIDE
NSA_NS = SEQ // NSA_SLC_BLOCK
NSA_GATE_ROWS = 16
NSA_TQ = 256
NSA_SPAN = 1024

RET_HEADS = 8
RET_QK_DIM = D_MODEL // RET_HEADS
RET_V_DIM = 2 * D_MODEL // RET_HEADS
RET_CHUNK = 256
RET_ROT_THETA = 10000.0
RET_HEADS_PER_STEP = 2

VMEM_LIMIT = 56 * 1024 * 1024

TN_DIMS = (((0,), (0,)), ((), ()))
NT_DIMS = (((1,), (1,)), ((), ()))


def _params(*sem):
    return pltpu.CompilerParams(dimension_semantics=sem, vmem_limit_bytes=VMEM_LIMIT)


def _rope_table_kernel(pos_ref, inv_ref, cos_ref, sin_ref):
    ang = pos_ref[...].astype(f32) * inv_ref[...]
    cos_ref[...] = jnp.cos(ang)
    sin_ref[...] = jnp.sin(ang)


def rope_tables(positions, half, theta):
    inv = (theta ** (-jnp.arange(half, dtype=f32) / half)).reshape(half, 1)
    ts = 1024
    return pl.pallas_call(
        _rope_table_kernel,
        grid=(SEQ // ts,),
        in_specs=[pl.BlockSpec((1, ts), lambda i: (0, i)),
                  pl.BlockSpec((half, 1), lambda i: (0, 0))],
        out_specs=[pl.BlockSpec((half, ts), lambda i: (0, i))] * 2,
        out_shape=[jax.ShapeDtypeStruct((half, SEQ), f32)] * 2,
        name=f"rope_tables_{half}",
    )(positions, inv)


def _ada_kernel(c_ref, w_ref, b_ref, o_ref):
    sc = jax.nn.silu(c_ref[...])
    sc8 = jnp.broadcast_to(sc, (8, D_MODEL)).astype(bf16)
    acc = jnp.dot(sc8, w_ref[0].astype(bf16), preferred_element_type=f32)
    o_ref[0] = acc[0:1] + b_ref[0]


def ada_modulation(c, ada_w, ada_b):
    tn = 1024
    n3 = 3 * D_MODEL
    out = pl.pallas_call(
        _ada_kernel,
        grid=(DEPTH, n3 // tn),
        in_specs=[pl.BlockSpec((1, D_MODEL), lambda l, j: (0, 0)),
                  pl.BlockSpec((1, D_MODEL, tn), lambda l, j: (l, 0, j)),
                  pl.BlockSpec((1, 1, tn), lambda l, j: (l, 0, j))],
        out_specs=pl.BlockSpec((1, 1, tn), lambda l, j: (l, 0, j)),
        out_shape=jax.ShapeDtypeStruct((DEPTH, 1, n3), f32),
        compiler_params=_params("arbitrary", "arbitrary"),
        name="ada_modulation",
    )(c, ada_w, ada_b.reshape(DEPTH, 1, n3))
    return out.reshape(DEPTH, n3)


def _prenorm_kernel(x_ref, g_ref, sc_ref, sh_ref, *o_refs, token_major_in):
    if token_major_in:
        xT_ref, o_ref = o_refs
        x = x_ref[...].T
        xT_ref[...] = x
    else:
        (o_ref,) = o_refs
        x = x_ref[...]
    ms = jnp.mean(x * x, axis=0, keepdims=True)
    y = (x * lax.rsqrt(ms + EPS)) * g_ref[...]
    o_ref[...] = (y * (1.0 + sc_ref[...]) + sh_ref[...]).astype(bf16)


def prenorm(x, g_col, scale_col, shift_col, token_major_in=False):
    tm = 512
    col = pl.BlockSpec((D_MODEL, 1), lambda i: (0, 0))
    fm = pl.BlockSpec((D_MODEL, tm), lambda i: (0, i))
    x_spec = pl.BlockSpec((tm, D_MODEL), lambda i: (i, 0)) if token_major_in else fm
    h_shape = jax.ShapeDtypeStruct((D_MODEL, SEQ), bf16)
    return pl.pallas_call(
        functools.partial(_prenorm_kernel, token_major_in=token_major_in),
        grid=(SEQ // tm,),
        in_specs=[x_spec, col, col, col],
        out_specs=[fm, fm] if token_major_in else fm,
        out_shape=[jax.ShapeDtypeStruct((D_MODEL, SEQ), f32), h_shape] if token_major_in else h_shape,
        compiler_params=_params("arbitrary"),
        name="prenorm",
    )(x, g_col, scale_col, shift_col)


def _proj_kernel(*refs, kind, tn, slab, half, scale, scale_lo, scale_hi, token_major):
    if kind == "rope":
        h_ref, w_ref, cos_ref, sin_ref, o_ref = refs
    else:
        h_ref, w_ref, o_ref = refs
    acc = lax.dot_general(w_ref[...].astype(bf16), h_ref[...], TN_DIMS,
                          preferred_element_type=f32)
    if scale is not None:
        j = pl.program_id(1)
        acc = acc * jnp.where((j >= scale_lo) & (j < scale_hi), scale, 1.0).astype(f32)
    if kind == "silu":
        acc = acc * jax.nn.sigmoid(acc)
    elif kind == "sigmoid":
        acc = jax.nn.sigmoid(acc)

    if kind == "rope":
        cos = cos_ref[...]
        sin = sin_ref[...]
        pieces = []
        for s in range(tn // slab):
            b = s * slab
            x1 = acc[b:b + half]
            x2 = acc[b + half:b + 2 * half]
            pieces.append(x1 * cos - x2 * sin)
            pieces.append(x1 * sin + x2 * cos)
            if 2 * half < slab:
                pieces.append(acc[b + 2 * half:b + slab])
        acc = jnp.concatenate(pieces, axis=0)
    if token_major:
        for s in range(tn // HEAD_DIM):
            o_ref[s] = acc[s * HEAD_DIM:(s + 1) * HEAD_DIM].T.astype(o_ref.dtype)
    else:
        o_ref[...] = acc.astype(o_ref.dtype)


def project(hT, w, cols, kind, *, layer=None, col_step=1, tables=None, slab=HEAD_DIM,
            half=ROPE_DIM // 2, scale=None, scale_rows=(0, 0), token_major=False,
            out_dtype=bf16, tn=PROJ_TN, name="proj"):
    start, n = cols
    tm = PROJ_TM
    tn = min(tn, n)
    first = start // tn
    assert n % tn == 0 and start % tn == 0
    assert (first + col_step * (n // tn - 1) + 1) * tn <= w.shape[-1]
    assert scale_rows[0] % tn == 0 and scale_rows[1] % tn == 0
    assert kind != "rope" or tn % slab == 0
    if layer is None:
        w_spec = pl.BlockSpec((D_MODEL, tn), lambda i, j: (0, first + col_step * j))
    else:
        w_spec = pl.BlockSpec((None, D_MODEL, tn), lambda i, j: (layer, 0, first + col_step * j))
    in_specs = [pl.BlockSpec((D_MODEL, tm), lambda i, j: (0, i)), w_spec]
    args = [hT, w]
    if kind == "rope":
        in_specs += [pl.BlockSpec((half, tm), lambda i, j: (0, i))] * 2
        args += list(tables)
    if token_major:
        out_spec = pl.BlockSpec((tn // HEAD_DIM, tm, HEAD_DIM), lambda i, j: (j, i, 0))
        out_shape = jax.ShapeDtypeStruct((n // HEAD_DIM, SEQ, HEAD_DIM), out_dtype)
    else:
        out_spec = pl.BlockSpec((tn, tm), lambda i, j: (j, i))
        out_shape = jax.ShapeDtypeStruct((n, SEQ), out_dtype)
    body = functools.partial(
        _proj_kernel, kind=kind, tn=tn, slab=slab, half=half, scale=scale,
        scale_lo=scale_rows[0] // tn, scale_hi=scale_rows[1] // tn, token_major=token_major)
    return pl.pallas_call(
        body, grid=(SEQ // tm, n // tn), in_specs=in_specs, out_specs=out_spec,
        out_shape=out_shape, compiler_params=_params("arbitrary", "arbitrary"), name=name,
    )(*args)


def _outproj_kernel(a_ref, w_ref, x_ref, gate_ref, g_ref, o_ref, *, token_major_out):
    y = jnp.dot(w_ref[...], a_ref[...], preferred_element_type=f32)
    ms = jnp.mean(y * y, axis=0, keepdims=True)
    yn = (y * lax.rsqrt(ms + EPS)) * g_ref[...]
    out = x_ref[...] + gate_ref[...] * yn
    o_ref[...] = out.T if token_major_out else out


def outproj_residual(aT, w_outT, xT, gate_col, g_col, token_major_out=False):
    k = aT.shape[0]
    tm = 256
    col = pl.BlockSpec((D_MODEL, 1), lambda i: (0, 0))
    if token_major_out:
        out_spec = pl.BlockSpec((tm, D_MODEL), lambda i: (i, 0))
        out_shape = jax.ShapeDtypeStruct((SEQ, D_MODEL), f32)
    else:
        out_spec = pl.BlockSpec((D_MODEL, tm), lambda i: (0, i))
        out_shape = jax.ShapeDtypeStruct((D_MODEL, SEQ), f32)
    return pl.pallas_call(
        functools.partial(_outproj_kernel, token_major_out=token_major_out),
        grid=(SEQ // tm,),
        in_specs=[pl.BlockSpec((k, tm), lambda i: (0, i)),
                  pl.BlockSpec((D_MODEL, k), lambda i: (0, 0)),
                  pl.BlockSpec((D_MODEL, tm), lambda i: (0, i)), col, col],
        out_specs=out_spec,
        out_shape=out_shape,
        compiler_params=_params("arbitrary"),
        name="outproj_residual",
    )(aT, w_outT, xT, gate_col, g_col)


def _first_k_select(val, idx, n_rows, k, slot_ok):
    sel = jnp.zeros(val.shape, f32)
    for t in range(k):
        m = jnp.max(val, axis=0, keepdims=True)
        first = jnp.min(jnp.where(val == m, idx, n_rows), axis=0, keepdims=True)
        pick = idx == first
        sel = jnp.where(pick, slot_ok(t, m), sel)
        val = jnp.where(pick, -jnp.inf, val)
    return sel


def _probs(s, m):
    return jnp.exp((s - m).astype(bf16))


def _flash_step(s, va, m, acc):
    mn = jnp.maximum(m, jnp.max(s, axis=0, keepdims=True))
    return mn, jnp.exp(m - mn) * acc + jnp.dot(va, _probs(s, mn), preferred_element_type=f32)


def _flash_step_ref(s_ref, va, m, acc, group=COL_GROUP):
    mns, ps = [], []
    for c in range(s_ref.shape[1] // group):
        cs = slice(c * group, (c + 1) * group)
        mn = jnp.maximum(m[:, cs], jnp.max(s_ref[:, cs], axis=0, keepdims=True))
        mns.append(mn)
        ps.append(_probs(s_ref[:, cs], mn))
    mn = jnp.concatenate(mns, axis=1)
    p = jnp.concatenate(ps, axis=1)
    return mn, jnp.exp(m - mn) * acc + jnp.dot(va, p, preferred_element_type=f32)


def _flash_ring(k_ref, e_ref, vaug_ref, qaug_ref, bufs, rounds, carry):
    ring = len(bufs)
    span = bufs[0].shape[0]

    def scores(n):
        off = pl.multiple_of(n * span, span)
        kaug = jnp.concatenate([k_ref[0, pl.ds(off, span), :], e_ref[pl.ds(off, span), :]], axis=1)
        return jnp.dot(kaug, qaug_ref[...], preferred_element_type=f32)

    def values(n):
        return vaug_ref[:, pl.ds(pl.multiple_of(n * span, span), span)]

    def one_round(j, carry, prefetch_next):
        n = ring * j
        for t in range(ring):
            if t + RING_AHEAD < ring or prefetch_next:
                bufs[(t + RING_AHEAD) % ring][...] = scores(n + t + RING_AHEAD)
            carry = _flash_step_ref(bufs[t], values(n + t), *carry)
        return carry

    def run(carry):
        for t in range(RING_AHEAD):
            bufs[t][...] = scores(t)
        carry = lax.fori_loop(0, rounds - 1, lambda j, c: one_round(j, c, True), carry)
        return one_round(rounds - 1, carry, False)

    return lax.cond(rounds > 0, run, lambda c: c, carry)


def _moba_attn_kernel(q_ref, k_ref, v_ref, z_ref, avg_ref, e_ref, o_ref,
                      kmean_ref, vaug_ref, qaug_ref, *score_bufs):
    blk = MOBA_BLOCK
    tq = MOBA_TQ
    sub = tq // blk
    i = pl.program_id(1)
    b0 = i * sub

    @pl.when(i == 0)
    def _():
        kmean_ref[...] = jnp.dot(avg_ref[...], k_ref[0], preferred_element_type=f32)
        vaug_ref[0:HEAD_DIM, :] = v_ref[...]
        vaug_ref[HEAD_DIM:, :] = jnp.ones((ONES_ROWS, SEQ), bf16)

    qT = q_ref[...]
    gate = jnp.dot(kmean_ref[...], qT.astype(f32), precision=lax.Precision.HIGHEST,
                   preferred_element_type=f32)
    nidx = lax.broadcasted_iota(i32, (MOBA_NB, tq), 0)
    own = b0 + jnp.right_shift(lax.broadcasted_iota(i32, (1, tq), 1), blk.bit_length() - 1)
    gate = jnp.where(nidx < own, gate, NEG)
    sel = _first_k_select(gate, nidx, MOBA_NB, MOBA_TOPK, lambda t, m: jnp.where(t < own, 1.0, 0.0))
    qaug_ref[0:HEAD_DIM, :] = qT
    qaug_ref[HEAD_DIM:HEAD_DIM + MOBA_NB, :] = jnp.where(sel > 0.5, 0.0, NEG).astype(bf16)
    qaug_ref[HEAD_DIM + MOBA_NB:, :] = jnp.zeros((HEAD_DIM - MOBA_NB, tq), bf16)

    offs = [pl.multiple_of((b0 + c) * blk, blk) for c in range(sub)]
    s_own = jnp.concatenate(
        [jnp.dot(k_ref[0, pl.ds(offs[c], blk), :], qT[:, c * blk:(c + 1) * blk],
                 preferred_element_type=f32) for c in range(sub)], axis=1)
    krow = lax.broadcasted_iota(i32, (blk, tq), 0)
    qcol = lax.broadcasted_iota(i32, (blk, tq), 1) & (blk - 1)
    s_own = jnp.where(krow <= qcol, s_own, NEG)
    m = jnp.max(s_own, axis=0, keepdims=True)
    p = _probs(s_own, m)
    acc = jnp.concatenate(
        [jnp.dot(vaug_ref[:, pl.ds(offs[c], blk)], p[:, c * blk:(c + 1) * blk],
                 preferred_element_type=f32) for c in range(sub)], axis=1)

    inner = (sub - 1) * blk
    kaug = jnp.concatenate([k_ref[0, pl.ds(offs[0], inner), :], e_ref[pl.ds(offs[0], inner), :]],
                           axis=1)
    s_in = jnp.dot(kaug, qaug_ref[:, blk:], preferred_element_type=f32)
    m_hi, acc_hi = _flash_step(s_in, vaug_ref[:, pl.ds(offs[0], inner)], m[:, blk:], acc[:, blk:])
    m = jnp.concatenate([m[:, :blk], m_hi], axis=1)
    acc = jnp.concatenate([acc[:, :blk], acc_hi], axis=1)

    _, acc = _flash_ring(k_ref, e_ref, vaug_ref, qaug_ref, score_bufs, i, (m, acc))
    o = acc[0:HEAD_DIM] / jnp.maximum(acc[HEAD_DIM:HEAD_DIM + 1], 1e-30)
    o_ref[...] = (o * z_ref[...].astype(f32)).astype(bf16)


def moba_attention(qT, k_tok, vT, zT):
    tq = MOBA_TQ
    h = MOBA_HEADS
    blk_of = jnp.arange(SEQ) // MOBA_BLOCK
    avg = ((blk_of[None, :] == jnp.arange(MOBA_NB)[:, None]).astype(f32) / MOBA_BLOCK).astype(bf16)
    onehot = (blk_of[:, None] == jnp.arange(HEAD_DIM)[None, :]).astype(bf16)
    return pl.pallas_call(
        _moba_attn_kernel,
        grid=(h, SEQ // tq),
        in_specs=[pl.BlockSpec((HEAD_DIM, tq), lambda hh, i: (hh, i)),
                  pl.BlockSpec((1, SEQ, HEAD_DIM), lambda hh, i: (hh, 0, 0)),
                  pl.BlockSpec((HEAD_DIM, SEQ), lambda hh, i: (hh, 0)),
                  pl.BlockSpec((HEAD_DIM, tq), lambda hh, i: (hh, i)),
                  pl.BlockSpec((MOBA_NB, SEQ), lambda hh, i: (0, 0)),
                  pl.BlockSpec((SEQ, HEAD_DIM), lambda hh, i: (0, 0))],
        out_specs=pl.BlockSpec((HEAD_DIM, tq), lambda hh, i: (hh, i)),
        out_shape=jax.ShapeDtypeStruct((h * HEAD_DIM, SEQ), bf16),
        scratch_shapes=[pltpu.VMEM((MOBA_NB, HEAD_DIM), f32),
                        pltpu.VMEM((HEAD_DIM + ONES_ROWS, SEQ), bf16),
                        pltpu.VMEM((2 * HEAD_DIM, tq), bf16),
                        ] + [pltpu.VMEM((tq // RING, tq), f32)] * RING,
        compiler_params=_params("arbitrary", "arbitrary"),
        name="moba_attention",
    )(qT, k_tok, vT, zT, avg, onehot)


def _nsa_cmp_kernel(x_ref, wab_ref, pe_ref, w1_ref, w2_ref, o_ref, *, transpose_out):
    hid = NSA_CMP_HIDDEN
    ab = jnp.dot(x_ref[0], wab_ref[...], preferred_element_type=f32)
    first = ab[:, :hid]
    second = pltpu.roll(ab[:, hid:], NSA_NC_PAD - 1, 0)
    pe8 = jnp.broadcast_to(pe_ref[...], (8, NSA_CMP_LEN * HEAD_DIM)).astype(bf16)
    bias = jnp.dot(pe8, w1_ref[...], preferred_element_type=f32)[0:1]
    hpre = first + second + bias
    out = jnp.dot((hpre * jax.nn.sigmoid(hpre)).astype(bf16), w2_ref[...],
                  preferred_element_type=f32)
    o_ref[0] = (out.T if transpose_out else out).astype(o_ref.dtype)


def nsa_compress(t_tok, pe, w1, w2, transpose_out):
    g = NSA_KV_GROUPS
    half_k = NSA_CMP_STRIDE * HEAD_DIM
    x = t_tok.reshape(g, NSA_NC_PAD, half_k)
    w1b = w1.astype(bf16)
    wab = jnp.concatenate([w1b[:half_k], w1b[half_k:]], axis=1)
    oshape = (g, HEAD_DIM, NSA_NC_PAD) if transpose_out else (g, NSA_NC_PAD, HEAD_DIM)
    return pl.pallas_call(
        functools.partial(_nsa_cmp_kernel, transpose_out=transpose_out),
        grid=(g,),
        in_specs=[pl.BlockSpec((1, NSA_NC_PAD, half_k), lambda gg: (gg, 0, 0)),
                  pl.BlockSpec((half_k, 2 * NSA_CMP_HIDDEN), lambda gg: (0, 0)),
                  pl.BlockSpec((1, NSA_CMP_LEN * HEAD_DIM), lambda gg: (0, 0)),
                  pl.BlockSpec((NSA_CMP_LEN * HEAD_DIM, NSA_CMP_HIDDEN), lambda gg: (0, 0)),
                  pl.BlockSpec((NSA_CMP_HIDDEN, HEAD_DIM), lambda gg: (0, 0))],
        out_specs=pl.BlockSpec((1,) + oshape[1:], lambda gg: (gg, 0, 0)),
        out_shape=jax.ShapeDtypeStruct(oshape, bf16),
        compiler_params=_params("arbitrary"),
        name="nsa_compress",
    )(x, wab, pe.reshape(1, NSA_CMP_LEN * HEAD_DIM), w1b, w2.astype(bf16))


def _nsa_attn_kernel(q_ref, kc_ref, vc_ref, ov_ref, ks_ref, vs_ref, kw_ref, vw_ref, e_ref,
                     gt_ref, z_ref, o_ref, vsaug_ref, vwaug_ref, qaug_ref, *score_bufs):
    tq = NSA_TQ
    rep = NSA_REP
    nq = rep * tq
    i = pl.program_id(1)
    q0 = i * tq

    @pl.when(i == 0)
    def _():
        for src, dst in ((vs_ref, vsaug_ref), (vw_ref, vwaug_ref)):
            dst[0:HEAD_DIM, :] = src[...]
            dst[HEAD_DIM:, :] = jnp.ones((ONES_ROWS, SEQ), bf16)

    qT = jnp.concatenate([q_ref[r * HEAD_DIM:(r + 1) * HEAD_DIM, :] for r in range(rep)],
                         axis=1)
    lane = lax.broadcasted_iota(i32, (1, nq), 1)
    qpos = q0 + (lane & (tq - 1))
    qpos1 = q0 + lax.broadcasted_iota(i32, (1, tq), 1)

    sc = jnp.dot(kc_ref[0], qT, preferred_element_type=f32)
    cend = lax.broadcasted_iota(i32, (NSA_NC_PAD, nq), 0) * NSA_CMP_STRIDE + (NSA_CMP_LEN - 1)
    ok_c = cend <= qpos
    sc = jnp.where(ok_c, sc, NEG)
    pc = jnp.where(ok_c, jnp.exp(sc - jnp.max(sc, axis=0, keepdims=True)), 0.0)
    pc = pc / jnp.maximum(jnp.sum(pc, axis=0, keepdims=True), 1e-30)
    ocT = jnp.dot(vc_ref[0], pc.astype(bf16), preferred_element_type=f32)

    ps = pc[:, 0:tq]
    for r in range(1, rep):
        ps = ps + pc[:, r * tq:(r + 1) * tq]
    ps_hi = ps.astype(bf16)
    ps_lo = (ps - ps_hi.astype(f32)).astype(bf16)
    imp = (jnp.dot(ov_ref[...], ps_hi, preferred_element_type=f32)
           + jnp.dot(ov_ref[...], ps_lo, preferred_element_type=f32))
    sidx = lax.broadcasted_iota(i32, (NSA_NS, tq), 0)
    cur = jnp.right_shift(qpos1, 6)
    valid = sidx * NSA_SLC_BLOCK <= qpos1
    forced = (sidx == 0) | (sidx == cur) | (sidx == cur - 1)
    val = jnp.where(valid, jnp.where(forced, POS_BIG, imp), NEG)
    sel = _first_k_select(val, sidx, NSA_NS, NSA_SLC_TOPK, lambda t, m: jnp.where(m > NEG / 2, 1.0, 0.0))

    bias = jnp.where(sel > 0.5, 0.0, NEG)
    qaug_ref[0:HEAD_DIM, :] = qT
    qaug_ref[HEAD_DIM:, :] = jnp.concatenate([bias] * rep, axis=1).astype(bf16)
    offd = pl.multiple_of(q0, tq)
    kaug = jnp.concatenate([ks_ref[0, pl.ds(offd, tq), :], e_ref[pl.ds(offd, tq), :]], axis=1)
    s = jnp.dot(kaug, qaug_ref[...], preferred_element_type=f32)
    s = jnp.where(q0 + lax.broadcasted_iota(i32, (tq, nq), 0) <= qpos, s, NEG)
    m = jnp.max(s, axis=0, keepdims=True)
    acc = jnp.dot(vsaug_ref[:, pl.ds(offd, tq)], _probs(s, m),
                  preferred_element_type=f32)

    own_first = i * (tq // NSA_SLC_BLOCK)
    bias_past = jnp.where(sidx >= own_first, NEG, bias)
    qaug_ref[HEAD_DIM:, :] = jnp.concatenate([bias_past] * rep, axis=1).astype(bf16)

    per_span = NSA_SPAN // tq
    rounds = jnp.right_shift(i + per_span - 1, per_span.bit_length() - 1)
    _, acc = _flash_ring(ks_ref, e_ref, vsaug_ref, qaug_ref, score_bufs, rounds, (m, acc))
    osT = acc[0:HEAD_DIM] / jnp.maximum(acc[HEAD_DIM:HEAD_DIM + 1], 1e-30)

    wspan = NSA_WINDOW + tq
    start = pl.multiple_of(jnp.maximum(q0 - NSA_WINDOW, 0), tq)
    sw = jnp.dot(kw_ref[0, pl.ds(start, wspan), :], qT, preferred_element_type=f32)
    kpos = start + lax.broadcasted_iota(i32, (wspan, nq), 0)
    sw = jnp.where((kpos <= qpos) & (kpos > qpos - NSA_WINDOW), sw, NEG)
    pw = _probs(sw, jnp.max(sw, axis=0, keepdims=True))
    accw = jnp.dot(vwaug_ref[:, pl.ds(start, wspan)], pw, preferred_element_type=f32)
    owT = accw[0:HEAD_DIM] / jnp.maximum(accw[HEAD_DIM:HEAD_DIM + 1], 1e-30)

    gt = gt_ref[0]
    for r in range(rep):
        cs = slice(r * tq, (r + 1) * tq)
        rs = slice(r * HEAD_DIM, (r + 1) * HEAD_DIM)
        o = (gt[3 * r:3 * r + 1] * ocT[:, cs] + gt[3 * r + 1:3 * r + 2] * osT[:, cs]
             + gt[3 * r + 2:3 * r + 3] * owT[:, cs])
        o_ref[rs, :] = (o * z_ref[rs, :].astype(f32)).astype(bf16)


def _nsa_overlap_t():
    nc = NSA_NC_PAD - 1
    cs = np.arange(nc)[:, None] * NSA_CMP_STRIDE
    ss = np.arange(NSA_NS)[None, :] * NSA_SLC_BLOCK
    ov = np.clip(np.minimum(cs + NSA_CMP_LEN, ss + NSA_SLC_BLOCK) - np.maximum(cs, ss), 0, None)
    ov = np.concatenate([ov / NSA_CMP_LEN, np.zeros((1, NSA_NS))], axis=0)
    return jnp.asarray(ov.T, dtype=bf16)


def nsa_attention(qT, k_tok, plainT, k_cmp, v_cmpT, gatesT, zT):
    tq = NSA_TQ
    g = NSA_KV_GROUPS
    qrows = NSA_REP * HEAD_DIM
    onehot = ((jnp.arange(SEQ) // NSA_SLC_BLOCK)[:, None] == jnp.arange(NSA_NS)[None, :]).astype(bf16)
    keys = lambda first: pl.BlockSpec((1, SEQ, HEAD_DIM), lambda gg, i: (first + gg, 0, 0))
    vals = lambda first: pl.BlockSpec((HEAD_DIM, SEQ), lambda gg, i: (first + gg, 0))
    return pl.pallas_call(
        _nsa_attn_kernel,
        grid=(g, SEQ // tq),
        in_specs=[pl.BlockSpec((qrows, tq), lambda gg, i: (gg, i)),
                  pl.BlockSpec((1, NSA_NC_PAD, HEAD_DIM), lambda gg, i: (gg, 0, 0)),
                  pl.BlockSpec((1, HEAD_DIM, NSA_NC_PAD), lambda gg, i: (gg, 0, 0)),
                  pl.BlockSpec((NSA_NS, NSA_NC_PAD), lambda gg, i: (0, 0)),
                  keys(0), vals(0), keys(g), vals(g),
                  pl.BlockSpec((SEQ, NSA_NS), lambda gg, i: (0, 0)),
                  pl.BlockSpec((1, NSA_GATE_ROWS, tq), lambda gg, i: (gg, 0, i)),
                  pl.BlockSpec((qrows, tq), lambda gg, i: (gg, i))],
        out_specs=pl.BlockSpec((qrows, tq), lambda gg, i: (gg, i)),
        out_shape=jax.ShapeDtypeStruct((NSA_HEADS * HEAD_DIM, SEQ), bf16),
        scratch_shapes=[pltpu.VMEM((HEAD_DIM + ONES_ROWS, SEQ), bf16),
                        pltpu.VMEM((HEAD_DIM + ONES_ROWS, SEQ), bf16),
                        pltpu.VMEM((HEAD_DIM + NSA_NS, NSA_REP * tq), bf16),
                        ] + [pltpu.VMEM((NSA_SPAN // RING, NSA_REP * tq), f32)] * RING,
        compiler_params=_params("arbitrary", "arbitrary"),
        name="nsa_attention",
    )(qT, k_cmp, v_cmpT, _nsa_overlap_t(), k_tok, plainT, k_tok, plainT, onehot, gatesT, zT)


def _ret_kernel(q_ref, k_ref, v_ref, g_ref, dt_ref, rd_ref, wd_ref, cd_ref, o_ref, st_ref):
    @pl.when(pl.program_id(1) == 0)
    def _():
        st_ref[...] = jnp.zeros(st_ref.shape, f32)

    heads = range(RET_HEADS_PER_STEP)
    rows = lambda ref, p, n: ref[p * n:(p + 1) * n, :]
    qT = [rows(q_ref, p, RET_QK_DIM) for p in heads]
    kT = [rows(k_ref, p, RET_QK_DIM) for p in heads]
    vT = [rows(v_ref, p, RET_V_DIM) for p in heads]
    innerT = [lax.dot_general(kT[p], qT[p], TN_DIMS, preferred_element_type=f32) * dt_ref[p]
              for p in heads]
    st = [st_ref[p] for p in heads]
    cross = [jnp.dot(st[p].astype(bf16), qT[p], preferred_element_type=f32) * rd_ref[p]
             for p in heads]
    o = [jnp.dot(vT[p], innerT[p].astype(bf16), preferred_element_type=f32) + cross[p]
         for p in heads]
    kw = [(kT[p].astype(f32) * wd_ref[p]).astype(bf16) for p in heads]
    for p in heads:
        st_ref[p] = cd_ref[p] * st[p] + lax.dot_general(vT[p], kw[p], NT_DIMS,
                                                        preferred_element_type=f32)
    for p in heads:
        mu = jnp.mean(o[p], axis=0, keepdims=True)
        d = o[p] - mu
        var = jnp.mean(d * d, axis=0, keepdims=True)
        gate = rows(g_ref, p, RET_V_DIM).astype(f32)
        o_ref[p * RET_V_DIM:(p + 1) * RET_V_DIM, :] = (gate * (d * lax.rsqrt(var + EPS))).astype(bf16)


def retention(qkT, vT, gT):
    c = RET_CHUNK
    h = RET_HEADS
    log_gamma = jnp.log(1.0 - 2.0 ** (-5.0 - jnp.arange(h, dtype=f32)))
    i = jnp.arange(c, dtype=f32)
    diff = i[None, :] - i[:, None]
    dt = jnp.where(diff >= 0, jnp.exp(log_gamma[:, None, None] * jnp.maximum(diff, 0.0)), 0.0)
    rd = jnp.exp(log_gamma[:, None] * (i + 1.0))[:, None, :]
    wd = jnp.exp(log_gamma[:, None] * (c - 1.0 - i))[:, None, :]
    cd = jnp.broadcast_to(jnp.exp(log_gamma * c)[:, None, None], (h, 1, RET_QK_DIM))
    per = RET_HEADS_PER_STEP
    steps = h // per
    vec = lambda n: pl.BlockSpec((per, 1, n), lambda hh, t: (hh, 0, 0))
    return pl.pallas_call(
        _ret_kernel,
        grid=(steps, SEQ // c),
        in_specs=[pl.BlockSpec((per * RET_QK_DIM, c), lambda hh, t: (hh, t)),
                  pl.BlockSpec((per * RET_QK_DIM, c), lambda hh, t: (steps + hh, t)),
                  pl.BlockSpec((per * RET_V_DIM, c), lambda hh, t: (hh, t)),
                  pl.BlockSpec((per * RET_V_DIM, c), lambda hh, t: (hh, t)),
                  pl.BlockSpec((per, c, c), lambda hh, t: (hh, 0, 0)),
                  vec(c), vec(c), vec(RET_QK_DIM)],
        out_specs=pl.BlockSpec((per * RET_V_DIM, c), lambda hh, t: (hh, t)),
        out_shape=jax.ShapeDtypeStruct((h * RET_V_DIM, SEQ), bf16),
        scratch_shapes=[pltpu.VMEM((per, RET_V_DIM, RET_QK_DIM), f32)],
        compiler_params=_params("arbitrary", "arbitrary"),
        name="retention",
    )(qkT, qkT, vT, gT, dt, rd, wd, cd)


def _t(w):
    return w.T.astype(bf16)


def moba_mixer(hT, tabs16, w_in, layer, w_out):
    hd = MOBA_HEADS * HEAD_DIM
    proj = functools.partial(project, hT, w_in, layer=layer)
    qT = proj((0, hd), "rope", tables=tabs16, scale=HEAD_DIM ** -0.5, scale_rows=(0, hd),
              name="moba_proj_q")
    k_tok = proj((hd, hd), "rope", tables=tabs16, token_major=True, name="moba_proj_k")
    vT = proj((2 * hd, hd), "plain", name="moba_proj_v")
    zT = proj((3 * hd, hd), "silu", name="moba_proj_z")
    return moba_attention(qT, k_tok, vT, zT), _t(w_out)


def nsa_mixer(hT, tabs16, w_in, layer, w_out, pe, wk1, wk2, wv1, wv2):
    hd = NSA_HEADS * HEAD_DIM
    gd = NSA_KV_GROUPS * HEAD_DIM
    proj = functools.partial(project, hT, w_in, layer=layer)
    qT = proj((0, hd), "rope", tables=tabs16, scale=HEAD_DIM ** -0.5, scale_rows=(0, hd),
              name="nsa_proj_q")
    k_tok = proj((hd + 2 * gd, 2 * gd), "rope", col_step=2, tables=tabs16, token_major=True,
                 name="nsa_proj_k")
    plainT = proj((hd + 3 * gd, 2 * gd), "plain", col_step=2, name="nsa_proj_v")
    kc_tok = proj((hd, gd), "rope", tables=tabs16, token_major=True, name="nsa_proj_kc")
    vc_tok = proj((hd + gd, gd), "plain", token_major=True, name="nsa_proj_vc")
    g0 = hd + 6 * gd
    per = NSA_REP * 3
    wg = w_in[layer, :, g0:g0 + NSA_KV_GROUPS * per].reshape(D_MODEL, NSA_KV_GROUPS, per)
    wg = jnp.pad(wg, ((0, 0), (0, 0), (0, NSA_GATE_ROWS - per))).reshape(D_MODEL, -1)
    gatesT = project(hT, wg, (0, wg.shape[1]), "sigmoid", out_dtype=f32, name="nsa_proj_gate")
    gatesT = gatesT.reshape(NSA_KV_GROUPS, NSA_GATE_ROWS, SEQ)
    wz = w_in[layer, :, g0 + NSA_HEADS * 3:]
    zT = project(hT, wz, (0, hd), "silu", name="nsa_proj_z")
    k_cmp = nsa_compress(kc_tok, pe, wk1, wk2, transpose_out=False)
    v_cmpT = nsa_compress(vc_tok, pe, wv1, wv2, transpose_out=True)
    return nsa_attention(qT, k_tok, plainT, k_cmp, v_cmpT, gatesT, zT), _t(w_out)


def retention_mixer(hT, tabs128, w_in, layer, w_out):
    qk = 2 * RET_HEADS * RET_QK_DIM
    vd = RET_HEADS * RET_V_DIM
    proj = functools.partial(project, hT, w_in, layer=layer)
    qkT = proj((0, qk), "rope", tables=tabs128, slab=RET_QK_DIM, half=RET_QK_DIM // 2,
               scale=RET_QK_DIM ** -0.5, scale_rows=(qk // 2, qk), name="ret_proj_qk")
    vT = proj((qk, vd), "plain", name="ret_proj_v")
    gT = proj((qk + vd, vd), "silu", name="ret_proj_g")
    return retention(qkT, vT, gT), _t(w_out)


def kernel(x, c, positions, ada_w, ada_b, norm_pre, norm_post, moba_w_in, moba_w_out, nsa_w_in, nsa_w_out, nsa_cmp_pe, nsa_cmp_wk1, nsa_cmp_wk2, nsa_cmp_wv1, nsa_cmp_wv2, ret_w_in, ret_w_out):
    assert x.shape == (1, SEQ, D_MODEL)
    mod = ada_modulation(c, ada_w, ada_b)
    tabs16 = rope_tables(positions, ROPE_DIM // 2, ROPE_THETA)
    tabs128 = rope_tables(positions, RET_QK_DIM // 2, RET_ROT_THETA)
    col = lambda v: v.reshape(D_MODEL, 1)
    shift = lambda i: col(mod[i, :D_MODEL])
    scale = lambda i: col(mod[i, D_MODEL:2 * D_MODEL])
    gate = lambda i: col(mod[i, 2 * D_MODEL:])
    xT, hT = prenorm(x.reshape(SEQ, D_MODEL), col(norm_pre[0]), scale(0), shift(0),
                     token_major_in=True)
    for i in range(DEPTH):
        kind, j = i % N_MIXERS, i // N_MIXERS
        if kind == 0:
            aT, w_outT = moba_mixer(hT, tabs16, moba_w_in, j, moba_w_out[j])
        elif kind == 1:
            aT, w_outT = nsa_mixer(hT, tabs16, nsa_w_in, j, nsa_w_out[j], nsa_cmp_pe[j],
                                   nsa_cmp_wk1[j], nsa_cmp_wk2[j], nsa_cmp_wv1[j], nsa_cmp_wv2[j])
        else:
            aT, w_outT = retention_mixer(hT, tabs128, ret_w_in, j, ret_w_out[j])
        if i + 1 < DEPTH:
            xT = outproj_residual(aT, w_outT, xT, gate(i), col(norm_post[i]))
            hT = prenorm(xT, col(norm_pre[i + 1]), scale(i + 1), shift(i + 1))
        else:
            xT = outproj_residual(aT, w_outT, xT, gate(i), col(norm_post[i]), token_major_out=True)
    return xT.reshape(1, SEQ, D_MODEL)
```

```python
import functools

import numpy as np
import jax
import jax.numpy as jnp
from jax import lax
from jax.experimental import pallas as pl
from jax.experimental.pallas import tpu as pltpu

f32 = jnp.float32
bf16 = jnp.bfloat16
i32 = jnp.int32

D_MODEL = 2048
SEQ = 8192
DEPTH = 4
N_MIXERS = 3
HEAD_DIM = 128
ROPE_THETA = 500000.0
ROPE_DIM = HEAD_DIM // 4
EPS = 1e-6
NEG = -1e30
POS_BIG = 1e30

MOBA_HEADS = D_MODEL // HEAD_DIM
MOBA_BLOCK = 256
MOBA_TOPK = 3
MOBA_NB = SEQ // MOBA_BLOCK
MOBA_TQ = 1024
ONES_ROWS = 16
OUT_T_ROWS = 512
PROJ_TN = 512
PROJ_TM = 2048
RING = 4
RING_AHEAD = 2
COL_GROUP = 256

NSA_HEADS = D_MODEL // HEAD_DIM
NSA_KV_GROUPS = NSA_HEADS // 4
NSA_REP = NSA_HEADS // NSA_KV_GROUPS
NSA_CMP_LEN = 32
NSA_CMP_STRIDE = 16
NSA_CMP_HIDDEN = 256
NSA_SLC_BLOCK = 64
NSA_SLC_TOPK = 16
NSA_WINDOW = 512
NSA_NC_PAD = SEQ // NSA_CMP_STRIDE
NSA_NS = SEQ // NSA_SLC_BLOCK
NSA_GATE_ROWS = 16
NSA_TQ = 256
NSA_SPAN = 1024

RET_HEADS = 8
RET_QK_DIM = D_MODEL // RET_HEADS
RET_V_DIM = 2 * D_MODEL // RET_HEADS
RET_CHUNK = 256
RET_ROT_THETA = 10000.0
RET_HEADS_PER_STEP = 2

VMEM_LIMIT = 56 * 1024 * 1024

TN_DIMS = (((0,), (0,)), ((), ()))
NT_DIMS = (((1,), (1,)), ((), ()))


def _params(*sem):
    return pltpu.CompilerParams(dimension_semantics=sem, vmem_limit_bytes=VMEM_LIMIT)


def _rope_table_kernel(pos_ref, inv_ref, cos_ref, sin_ref):
    ang = pos_ref[...].astype(f32) * inv_ref[...]
    cos_ref[...] = jnp.cos(ang)
    sin_ref[...] = jnp.sin(ang)


def rope_tables(positions, half, theta):
    inv = (theta ** (-jnp.arange(half, dtype=f32) / half)).reshape(half, 1)
    ts = 1024
    return pl.pallas_call(
        _rope_table_kernel,
        grid=(SEQ // ts,),
        in_specs=[pl.BlockSpec((1, ts), lambda i: (0, i)),
                  pl.BlockSpec((half, 1), lambda i: (0, 0))],
        out_specs=[pl.BlockSpec((half, ts), lambda i: (0, i))] * 2,
        out_shape=[jax.ShapeDtypeStruct((half, SEQ), f32)] * 2,
        name=f"rope_tables_{half}",
    )(positions, inv)


def _ada_kernel(c_ref, w_ref, b_ref, o_ref):
    sc = jax.nn.silu(c_ref[...])
    sc8 = jnp.broadcast_to(sc, (8, D_MODEL)).astype(bf16)
    acc = jnp.dot(sc8, w_ref[0].astype(bf16), preferred_element_type=f32)
    o_ref[...] = (acc + b_ref[0]).T[:, 0:1]


def ada_modulation(c, ada_w, ada_b):
    tn = 1024
    n3 = 3 * D_MODEL
    per = n3 // tn
    return pl.pallas_call(
        _ada_kernel,
        grid=(DEPTH, per),
        in_specs=[pl.BlockSpec((1, D_MODEL), lambda l, j: (0, 0)),
                  pl.BlockSpec((1, D_MODEL, tn), lambda l, j: (l, 0, j)),
                  pl.BlockSpec((1, 1, tn), lambda l, j: (l, 0, j))],
        out_specs=pl.BlockSpec((tn, 1), lambda l, j: (l * per + j, 0)),
        out_shape=jax.ShapeDtypeStruct((DEPTH * n3, 1), f32),
        compiler_params=_params("arbitrary", "arbitrary"),
        name="ada_modulation",
    )(c, ada_w, ada_b.reshape(DEPTH, 1, n3))


def _col_spec(block):
    return pl.BlockSpec((D_MODEL, 1), lambda *_: (block, 0))


def _prenorm_kernel(x_ref, g_ref, sc_ref, sh_ref, *o_refs, token_major_in):
    if token_major_in:
        xT_ref, o_ref = o_refs
        x = x_ref[...].T
        xT_ref[...] = x
    else:
        (o_ref,) = o_refs
        x = x_ref[...]
    ms = jnp.mean(x * x, axis=0, keepdims=True)
    y = (x * lax.rsqrt(ms + EPS)) * g_ref[...]
    o_ref[...] = (y * (1.0 + sc_ref[...]) + sh_ref[...]).astype(bf16)


def prenorm(x, g_col, scale_col, shift_col, token_major_in=False):
    tm = 512
    cols = [g_col, scale_col, shift_col]
    fm = pl.BlockSpec((D_MODEL, tm), lambda i: (0, i))
    x_spec = pl.BlockSpec((tm, D_MODEL), lambda i: (i, 0)) if token_major_in else fm
    h_shape = jax.ShapeDtypeStruct((D_MODEL, SEQ), bf16)
    return pl.pallas_call(
        functools.partial(_prenorm_kernel, token_major_in=token_major_in),
        grid=(SEQ // tm,),
        in_specs=[x_spec] + [_col_spec(b) for _, b in cols],
        out_specs=[fm, fm] if token_major_in else fm,
        out_shape=[jax.ShapeDtypeStruct((D_MODEL, SEQ), f32), h_shape] if token_major_in else h_shape,
        compiler_params=_params("arbitrary"),
        name="prenorm",
    )(x, *[a for a, _ in cols])


def _proj_kernel(*refs, kind, tn, slab, half, scale, scale_lo, scale_hi, token_major):
    if kind == "rope":
        h_ref, w_ref, cos_ref, sin_ref, o_ref = refs
    else:
        h_ref, w_ref, o_ref = refs
    acc = lax.dot_general(w_ref[...].astype(bf16), h_ref[...], TN_DIMS,
                          preferred_element_type=f32)
    if scale is not None:
        j = pl.program_id(1)
        acc = acc * jnp.where((j >= scale_lo) & (j < scale_hi), scale, 1.0).astype(f32)
    if kind == "silu":
        acc = acc * jax.nn.sigmoid(acc)
    elif kind == "sigmoid":
        acc = jax.nn.sigmoid(acc)

    if kind == "rope":
        cos = cos_ref[...]
        sin = sin_ref[...]
        pieces = []
        for s in range(tn // slab):
            b = s * slab
            x1 = acc[b:b + half]
            x2 = acc[b + half:b + 2 * half]
            pieces.append(x1 * cos - x2 * sin)
            pieces.append(x1 * sin + x2 * cos)
            if 2 * half < slab:
                pieces.append(acc[b + 2 * half:b + slab])
        acc = jnp.concatenate(pieces, axis=0)
    if token_major:
        for s in range(tn // HEAD_DIM):
            o_ref[s] = acc[s * HEAD_DIM:(s + 1) * HEAD_DIM].T.astype(o_ref.dtype)
    else:
        o_ref[...] = acc.astype(o_ref.dtype)


def project(hT, w, cols, kind, *, layer=None, col_step=1, tables=None, slab=HEAD_DIM,
            half=ROPE_DIM // 2, scale=None, scale_rows=(0, 0), token_major=False,
            out_dtype=bf16, tn=PROJ_TN, name="proj"):
    start, n = cols
    tm = PROJ_TM
    tn = min(tn, n)
    first = start // tn
    assert n % tn == 0 and start % tn == 0
    assert (first + col_step * (n // tn - 1) + 1) * tn <= w.shape[-1]
    assert scale_rows[0] % tn == 0 and scale_rows[1] % tn == 0
    assert kind != "rope" or tn % slab == 0
    if layer is None:
        w_spec = pl.BlockSpec((D_MODEL, tn), lambda i, j: (0, first + col_step * j))
    else:
        w_spec = pl.BlockSpec((None, D_MODEL, tn), lambda i, j: (layer, 0, first + col_step * j))
    in_specs = [pl.BlockSpec((D_MODEL, tm), lambda i, j: (0, i)), w_spec]
    args = [hT, w]
    if kind == "rope":
        in_specs += [pl.BlockSpec((half, tm), lambda i, j: (0, i))] * 2
        args += list(tables)
    if token_major:
        out_spec = pl.BlockSpec((tn // HEAD_DIM, tm, HEAD_DIM), lambda i, j: (j, i, 0))
        out_shape = jax.ShapeDtypeStruct((n // HEAD_DIM, SEQ, HEAD_DIM), out_dtype)
    else:
        out_spec = pl.BlockSpec((tn, tm), lambda i, j: (j, i))
        out_shape = jax.ShapeDtypeStruct((n, SEQ), out_dtype)
    body = functools.partial(
        _proj_kernel, kind=kind, tn=tn, slab=slab, half=half, scale=scale,
        scale_lo=scale_rows[0] // tn, scale_hi=scale_rows[1] // tn, token_major=token_major)
    return pl.pallas_call(
        body, grid=(SEQ // tm, n // tn), in_specs=in_specs, out_specs=out_spec,
        out_shape=out_shape, compiler_params=_params("arbitrary", "arbitrary"), name=name,
    )(*args)


def _outproj_kernel(a_ref, w_ref, x_ref, gate_ref, g_ref, o_ref, wt_ref, *, n_load, token_major_out):
    s = pl.program_id(0)

    @pl.when(s < n_load)
    def _():
        off = pl.multiple_of(s * OUT_T_ROWS, OUT_T_ROWS)
        wt_ref[:, pl.ds(off, OUT_T_ROWS)] = w_ref[...].T.astype(bf16)

    @pl.when(s >= n_load)
    def _():
        y = jnp.dot(wt_ref[...], a_ref[...], preferred_element_type=f32)
        ms = jnp.mean(y * y, axis=0, keepdims=True)
        yn = (y * lax.rsqrt(ms + EPS)) * g_ref[...]
        out = x_ref[...] + gate_ref[...] * yn
        o_ref[...] = out.T if token_major_out else out


def outproj_residual(aT, w_out, layer, xT, gate_col, g_col, token_major_out=False):
    k = aT.shape[0]
    tm = 256
    n_load = k // OUT_T_ROWS
    tile = lambda s: jnp.maximum(s - n_load, 0)
    if token_major_out:
        out_spec = pl.BlockSpec((tm, D_MODEL), lambda s: (tile(s), 0))
        out_shape = jax.ShapeDtypeStruct((SEQ, D_MODEL), f32)
    else:
        out_spec = pl.BlockSpec((D_MODEL, tm), lambda s: (0, tile(s)))
        out_shape = jax.ShapeDtypeStruct((D_MODEL, SEQ), f32)
    return pl.pallas_call(
        functools.partial(_outproj_kernel, n_load=n_load, token_major_out=token_major_out),
        grid=(n_load + SEQ // tm,),
        in_specs=[pl.BlockSpec((k, tm), lambda s: (0, tile(s))),
                  pl.BlockSpec((None, OUT_T_ROWS, D_MODEL),
                               lambda s: (layer, jnp.minimum(s, n_load - 1), 0)),
                  pl.BlockSpec((D_MODEL, tm), lambda s: (0, tile(s))),
                  _col_spec(gate_col[1]), _col_spec(g_col[1])],
        out_specs=out_spec,
        out_shape=out_shape,
        scratch_shapes=[pltpu.VMEM((D_MODEL, k), bf16)],
        compiler_params=_params("arbitrary"),
        name="outproj_residual",
    )(aT, w_out, xT, gate_col[0], g_col[0])


def _first_k_select(val, idx, n_rows, k, slot_ok):
    sel = jnp.zeros(val.shape, f32)
    for t in range(k):
        m = jnp.max(val, axis=0, keepdims=True)
        first = jnp.min(jnp.where(val == m, idx, n_rows), axis=0, keepdims=True)
        pick = idx == first
        sel = jnp.where(pick, slot_ok(t, m), sel)
        val = jnp.where(pick, -jnp.inf, val)
    return sel


def _probs(s, m):
    return jnp.exp((s - m).astype(bf16))


def _flash_step(s, va, m, acc):
    mn = jnp.maximum(m, jnp.max(s, axis=0, keepdims=True))
    return mn, jnp.exp(m - mn) * acc + jnp.dot(va, _probs(s, mn), preferred_element_type=f32)


def _flash_step_ref(s_ref, va, m, acc, group=COL_GROUP):
    mns, ps = [], []
    for c in range(s_ref.shape[1] // group):
        cs = slice(c * group, (c + 1) * group)
        mn = jnp.maximum(m[:, cs], jnp.max(s_ref[:, cs], axis=0, keepdims=True))
        mns.append(mn)
        ps.append(_probs(s_ref[:, cs], mn))
    mn = jnp.concatenate(mns, axis=1)
    p = jnp.concatenate(ps, axis=1)
    return mn, jnp.exp(m - mn) * acc + jnp.dot(va, p, preferred_element_type=f32)


def _flash_ring(k_ref, e_ref, vaug_ref, qaug_ref, bufs, rounds, carry):
    ring = len(bufs)
    span = bufs[0].shape[0]

    def scores(n):
        off = pl.multiple_of(n * span, span)
        kaug = jnp.concatenate([k_ref[0, pl.ds(off, span), :], e_ref[pl.ds(off, span), :]], axis=1)
        return jnp.dot(kaug, qaug_ref[...], preferred_element_type=f32)

    def values(n):
        return vaug_ref[:, pl.ds(pl.multiple_of(n * span, span), span)]

    def one_round(j, carry, prefetch_next):
        n = ring * j
        for t in range(ring):
            if t + RING_AHEAD < ring or prefetch_next:
                bufs[(t + RING_AHEAD) % ring][...] = scores(n + t + RING_AHEAD)
            carry = _flash_step_ref(bufs[t], values(n + t), *carry)
        return carry

    def run(carry):
        for t in range(RING_AHEAD):
            bufs[t][...] = scores(t)
        carry = lax.fori_loop(0, rounds - 1, lambda j, c: one_round(j, c, True), carry)
        return one_round(rounds - 1, carry, False)

    return lax.cond(rounds > 0, run, lambda c: c, carry)


def _moba_attn_kernel(q_ref, k_ref, v_ref, z_ref, avg_ref, e_ref, o_ref,
                      kmean_ref, vaug_ref, qaug_ref, *score_bufs):
    blk = MOBA_BLOCK
    tq = MOBA_TQ
    sub = tq // blk
    i = pl.program_id(1)
    b0 = i * sub

    @pl.when(i == 0)
    def _():
        kmean_ref[...] = jnp.dot(avg_ref[...], k_ref[0], preferred_element_type=f32)
        vaug_ref[0:HEAD_DIM, :] = v_ref[...]
        vaug_ref[HEAD_DIM:, :] = jnp.ones((ONES_ROWS, SEQ), bf16)

    qT = q_ref[...]
    gate = jnp.dot(kmean_ref[...], qT.astype(f32), precision=lax.Precision.HIGHEST,
                   preferred_element_type=f32)
    nidx = lax.broadcasted_iota(i32, (MOBA_NB, tq), 0)
    own = b0 + jnp.right_shift(lax.broadcasted_iota(i32, (1, tq), 1), blk.bit_length() - 1)
    gate = jnp.where(nidx < own, gate, NEG)
    sel = _first_k_select(gate, nidx, MOBA_NB, MOBA_TOPK, lambda t, m: jnp.where(t < own, 1.0, 0.0))
    qaug_ref[0:HEAD_DIM, :] = qT
    qaug_ref[HEAD_DIM:HEAD_DIM + MOBA_NB, :] = jnp.where(sel > 0.5, 0.0, NEG).astype(bf16)
    qaug_ref[HEAD_DIM + MOBA_NB:, :] = jnp.zeros((HEAD_DIM - MOBA_NB, tq), bf16)

    offs = [pl.multiple_of((b0 + c) * blk, blk) for c in range(sub)]
    s_own = jnp.concatenate(
        [jnp.dot(k_ref[0, pl.ds(offs[c], blk), :], qT[:, c * blk:(c + 1) * blk],
                 preferred_element_type=f32) for c in range(sub)], axis=1)
    krow = lax.broadcasted_iota(i32, (blk, tq), 0)
    qcol = lax.broadcasted_iota(i32, (blk, tq), 1) & (blk - 1)
    s_own = jnp.where(krow <= qcol, s_own, NEG)
    m = jnp.max(s_own, axis=0, keepdims=True)
    p = _probs(s_own, m)
    acc = jnp.concatenate(
        [jnp.dot(vaug_ref[:, pl.ds(offs[c], blk)], p[:, c * blk:(c + 1) * blk],
                 preferred_element_type=f32) for c in range(sub)], axis=1)

    inner = (sub - 1) * blk
    kaug = jnp.concatenate([k_ref[0, pl.ds(offs[0], inner), :], e_ref[pl.ds(offs[0], inner), :]],
                           axis=1)
    s_in = jnp.dot(kaug, qaug_ref[:, blk:], preferred_element_type=f32)
    m_hi, acc_hi = _flash_step(s_in, vaug_ref[:, pl.ds(offs[0], inner)], m[:, blk:], acc[:, blk:])
    m = jnp.concatenate([m[:, :blk], m_hi], axis=1)
    acc = jnp.concatenate([acc[:, :blk], acc_hi], axis=1)

    _, acc = _flash_ring(k_ref, e_ref, vaug_ref, qaug_ref, score_bufs, i, (m, acc))
    o = acc[0:HEAD_DIM] / jnp.maximum(acc[HEAD_DIM:HEAD_DIM + 1], 1e-30)
    o_ref[...] = (o * z_ref[...].astype(f32)).astype(bf16)


def moba_attention(qT, k_tok, vT, zT):
    tq = MOBA_TQ
    h = MOBA_HEADS
    blk_of = jnp.arange(SEQ) // MOBA_BLOCK
    avg = ((blk_of[None, :] == jnp.arange(MOBA_NB)[:, None]).astype(f32) / MOBA_BLOCK).astype(bf16)
    onehot = (blk_of[:, None] == jnp.arange(HEAD_DIM)[None, :]).astype(bf16)
    return pl.pallas_call(
        _moba_attn_kernel,
        grid=(h, SEQ // tq),
        in_specs=[pl.BlockSpec((HEAD_DIM, tq), lambda hh, i: (hh, i)),
                  pl.BlockSpec((1, SEQ, HEAD_DIM), lambda hh, i: (hh, 0, 0)),
                  pl.BlockSpec((HEAD_DIM, SEQ), lambda hh, i: (hh, 0)),
                  pl.BlockSpec((HEAD_DIM, tq), lambda hh, i: (hh, i)),
                  pl.BlockSpec((MOBA_NB, SEQ), lambda hh, i: (0, 0)),
                  pl.BlockSpec((SEQ, HEAD_DIM), lambda hh, i: (0, 0))],
        out_specs=pl.BlockSpec((HEAD_DIM, tq), lambda hh, i: (hh, i)),
        out_shape=jax.ShapeDtypeStruct((h * HEAD_DIM, SEQ), bf16),
        scratch_shapes=[pltpu.VMEM((MOBA_NB, HEAD_DIM), f32),
                        pltpu.VMEM((HEAD_DIM + ONES_ROWS, SEQ), bf16),
                        pltpu.VMEM((2 * HEAD_DIM, tq), bf16),
                        ] + [pltpu.VMEM((tq // RING, tq), f32)] * RING,
        compiler_params=_params("arbitrary", "arbitrary"),
        name="moba_attention",
    )(qT, k_tok, vT, zT, avg, onehot)


def _nsa_cmp_kernel(x_ref, wab_ref, pe_ref, w1_ref, w2_ref, o_ref, *, transpose_out):
    hid = NSA_CMP_HIDDEN
    ab = jnp.dot(x_ref[0], wab_ref[...], preferred_element_type=f32)
    first = ab[:, :hid]
    second = pltpu.roll(ab[:, hid:], NSA_NC_PAD - 1, 0)
    pe8 = jnp.broadcast_to(pe_ref[...], (8, NSA_CMP_LEN * HEAD_DIM)).astype(bf16)
    bias = jnp.dot(pe8, w1_ref[...], preferred_element_type=f32)[0:1]
    hpre = first + second + bias
    out = jnp.dot((hpre * jax.nn.sigmoid(hpre)).astype(bf16), w2_ref[...],
                  preferred_element_type=f32)
    o_ref[0] = (out.T if transpose_out else out).astype(o_ref.dtype)


def nsa_compress(t_tok, pe, w1, w2, transpose_out):
    g = NSA_KV_GROUPS
    half_k = NSA_CMP_STRIDE * HEAD_DIM
    x = t_tok.reshape(g, NSA_NC_PAD, half_k)
    w1b = w1.astype(bf16)
    wab = jnp.concatenate([w1b[:half_k], w1b[half_k:]], axis=1)
    oshape = (g, HEAD_DIM, NSA_NC_PAD) if transpose_out else (g, NSA_NC_PAD, HEAD_DIM)
    return pl.pallas_call(
        functools.partial(_nsa_cmp_kernel, transpose_out=transpose_out),
        grid=(g,),
        in_specs=[pl.BlockSpec((1, NSA_NC_PAD, half_k), lambda gg: (gg, 0, 0)),
                  pl.BlockSpec((half_k, 2 * NSA_CMP_HIDDEN), lambda gg: (0, 0)),
                  pl.BlockSpec((1, NSA_CMP_LEN * HEAD_DIM), lambda gg: (0, 0)),
                  pl.BlockSpec((NSA_CMP_LEN * HEAD_DIM, NSA_CMP_HIDDEN), lambda gg: (0, 0)),
                  pl.BlockSpec((NSA_CMP_HIDDEN, HEAD_DIM), lambda gg: (0, 0))],
        out_specs=pl.BlockSpec((1,) + oshape[1:], lambda gg: (gg, 0, 0)),
        out_shape=jax.ShapeDtypeStruct(oshape, bf16),
        compiler_params=_params("arbitrary"),
        name="nsa_compress",
    )(x, wab, pe.reshape(1, NSA_CMP_LEN * HEAD_DIM), w1b, w2.astype(bf16))


def _nsa_attn_kernel(q_ref, kc_ref, vc_ref, ov_ref, ks_ref, vs_ref, kw_ref, vw_ref, e_ref,
                     gt_ref, z_ref, o_ref, vsaug_ref, vwaug_ref, qaug_ref, *score_bufs):
    tq = NSA_TQ
    rep = NSA_REP
    nq = rep * tq
    i = pl.program_id(1)
    q0 = i * tq

    @pl.when(i == 0)
    def _():
        for src, dst in ((vs_ref, vsaug_ref), (vw_ref, vwaug_ref)):
            dst[0:HEAD_DIM, :] = src[...]
            dst[HEAD_DIM:, :] = jnp.ones((ONES_ROWS, SEQ), bf16)

    qT = jnp.concatenate([q_ref[r * HEAD_DIM:(r + 1) * HEAD_DIM, :] for r in range(rep)],
                         axis=1)
    lane = lax.broadcasted_iota(i32, (1, nq), 1)
    qpos = q0 + (lane & (tq - 1))
    qpos1 = q0 + lax.broadcasted_iota(i32, (1, tq), 1)

    sc = jnp.dot(kc_ref[0], qT, preferred_element_type=f32)
    cend = lax.broadcasted_iota(i32, (NSA_NC_PAD, nq), 0) * NSA_CMP_STRIDE + (NSA_CMP_LEN - 1)
    ok_c = cend <= qpos
    sc = jnp.where(ok_c, sc, NEG)
    pc = jnp.where(ok_c, jnp.exp(sc - jnp.max(sc, axis=0, keepdims=True)), 0.0)
    pc = pc / jnp.maximum(jnp.sum(pc, axis=0, keepdims=True), 1e-30)
    ocT = jnp.dot(vc_ref[0], pc.astype(bf16), preferred_element_type=f32)

    ps = pc[:, 0:tq]
    for r in range(1, rep):
        ps = ps + pc[:, r * tq:(r + 1) * tq]
    ps_hi = ps.astype(bf16)
    ps_lo = (ps - ps_hi.astype(f32)).astype(bf16)
    imp = (jnp.dot(ov_ref[...], ps_hi, preferred_element_type=f32)
           + jnp.dot(ov_ref[...], ps_lo, preferred_element_type=f32))
    sidx = lax.broadcasted_iota(i32, (NSA_NS, tq), 0)
    cur = jnp.right_shift(qpos1, 6)
    valid = sidx * NSA_SLC_BLOCK <= qpos1
    forced = (sidx == 0) | (sidx == cur) | (sidx == cur - 1)
    val = jnp.where(valid, jnp.where(forced, POS_BIG, imp), NEG)
    sel = _first_k_select(val, sidx, NSA_NS, NSA_SLC_TOPK, lambda t, m: jnp.where(m > NEG / 2, 1.0, 0.0))

    bias = jnp.where(sel > 0.5, 0.0, NEG)
    qaug_ref[0:HEAD_DIM, :] = qT
    qaug_ref[HEAD_DIM:, :] = jnp.concatenate([bias] * rep, axis=1).astype(bf16)
    offd = pl.multiple_of(q0, tq)
    kaug = jnp.concatenate([ks_ref[0, pl.ds(offd, tq), :], e_ref[pl.ds(offd, tq), :]], axis=1)
    s = jnp.dot(kaug, qaug_ref[...], preferred_element_type=f32)
    s = jnp.where(q0 + lax.broadcasted_iota(i32, (tq, nq), 0) <= qpos, s, NEG)
    m = jnp.max(s, axis=0, keepdims=True)
    acc = jnp.dot(vsaug_ref[:, pl.ds(offd, tq)], _probs(s, m),
                  preferred_element_type=f32)

    own_first = i * (tq // NSA_SLC_BLOCK)
    bias_past = jnp.where(sidx >= own_first, NEG, bias)
    qaug_ref[HEAD_DIM:, :] = jnp.concatenate([bias_past] * rep, axis=1).astype(bf16)

    per_span = NSA_SPAN // tq
    rounds = jnp.right_shift(i + per_span - 1, per_span.bit_length() - 1)
    _, acc = _flash_ring(ks_ref, e_ref, vsaug_ref, qaug_ref, score_bufs, rounds, (m, acc))
    osT = acc[0:HEAD_DIM] / jnp.maximum(acc[HEAD_DIM:HEAD_DIM + 1], 1e-30)

    wspan = NSA_WINDOW + tq
    start = pl.multiple_of(jnp.maximum(q0 - NSA_WINDOW, 0), tq)
    sw = jnp.dot(kw_ref[0, pl.ds(start, wspan), :], qT, preferred_element_type=f32)
    kpos = start + lax.broadcasted_iota(i32, (wspan, nq), 0)
    sw = jnp.where((kpos <= qpos) & (kpos > qpos - NSA_WINDOW), sw, NEG)
    pw = _probs(sw, jnp.max(sw, axis=0, keepdims=True))
    accw = jnp.dot(vwaug_ref[:, pl.ds(start, wspan)], pw, preferred_element_type=f32)
    owT = accw[0:HEAD_DIM] / jnp.maximum(accw[HEAD_DIM:HEAD_DIM + 1], 1e-30)

    gt = gt_ref[0]
    for r in range(rep):
        cs = slice(r * tq, (r + 1) * tq)
        rs = slice(r * HEAD_DIM, (r + 1) * HEAD_DIM)
        o = (gt[3 * r:3 * r + 1] * ocT[:, cs] + gt[3 * r + 1:3 * r + 2] * osT[:, cs]
             + gt[3 * r + 2:3 * r + 3] * owT[:, cs])
        o_ref[rs, :] = (o * z_ref[rs, :].astype(f32)).astype(bf16)


def _nsa_overlap_t():
    nc = NSA_NC_PAD - 1
    cs = np.arange(nc)[:, None] * NSA_CMP_STRIDE
    ss = np.arange(NSA_NS)[None, :] * NSA_SLC_BLOCK
    ov = np.clip(np.minimum(cs + NSA_CMP_LEN, ss + NSA_SLC_BLOCK) - np.maximum(cs, ss), 0, None)
    ov = np.concatenate([ov / NSA_CMP_LEN, np.zeros((1, NSA_NS))], axis=0)
    return jnp.asarray(ov.T, dtype=bf16)


def nsa_attention(qT, k_tok, plainT, k_cmp, v_cmpT, gatesT, zT):
    tq = NSA_TQ
    g = NSA_KV_GROUPS
    qrows = NSA_REP * HEAD_DIM
    onehot = ((jnp.arange(SEQ) // NSA_SLC_BLOCK)[:, None] == jnp.arange(NSA_NS)[None, :]).astype(bf16)
    keys = lambda first: pl.BlockSpec((1, SEQ, HEAD_DIM), lambda gg, i: (first + gg, 0, 0))
    vals = lambda first: pl.BlockSpec((HEAD_DIM, SEQ), lambda gg, i: (first + gg, 0))
    return pl.pallas_call(
        _nsa_attn_kernel,
        grid=(g, SEQ // tq),
        in_specs=[pl.BlockSpec((qrows, tq), lambda gg, i: (gg, i)),
                  pl.BlockSpec((1, NSA_NC_PAD, HEAD_DIM), lambda gg, i: (gg, 0, 0)),
                  pl.BlockSpec((1, HEAD_DIM, NSA_NC_PAD), lambda gg, i: (gg, 0, 0)),
                  pl.BlockSpec((NSA_NS, NSA_NC_PAD), lambda gg, i: (0, 0)),
                  keys(0), vals(0), keys(g), vals(g),
                  pl.BlockSpec((SEQ, NSA_NS), lambda gg, i: (0, 0)),
                  pl.BlockSpec((1, NSA_GATE_ROWS, tq), lambda gg, i: (gg, 0, i)),
                  pl.BlockSpec((qrows, tq), lambda gg, i: (gg, i))],
        out_specs=pl.BlockSpec((qrows, tq), lambda gg, i: (gg, i)),
        out_shape=jax.ShapeDtypeStruct((NSA_HEADS * HEAD_DIM, SEQ), bf16),
        scratch_shapes=[pltpu.VMEM((HEAD_DIM + ONES_ROWS, SEQ), bf16),
                        pltpu.VMEM((HEAD_DIM + ONES_ROWS, SEQ), bf16),
                        pltpu.VMEM((HEAD_DIM + NSA_NS, NSA_REP * tq), bf16),
                        ] + [pltpu.VMEM((NSA_SPAN // RING, NSA_REP * tq), f32)] * RING,
        compiler_params=_params("arbitrary", "arbitrary"),
        name="nsa_attention",
    )(qT, k_cmp, v_cmpT, _nsa_overlap_t(), k_tok, plainT, k_tok, plainT, onehot, gatesT, zT)


def _ret_kernel(q_ref, k_ref, v_ref, g_ref, dt_ref, rd_ref, wd_ref, cd_ref, o_ref, st_ref):
    @pl.when(pl.program_id(1) == 0)
    def _():
        st_ref[...] = jnp.zeros(st_ref.shape, f32)

    heads = range(RET_HEADS_PER_STEP)
    rows = lambda ref, p, n: ref[p * n:(p + 1) * n, :]
    qT = [rows(q_ref, p, RET_QK_DIM) for p in heads]
    kT = [rows(k_ref, p, RET_QK_DIM) for p in heads]
    vT = [rows(v_ref, p, RET_V_DIM) for p in heads]
    innerT = [lax.dot_general(kT[p], qT[p], TN_DIMS, preferred_element_type=f32) * dt_ref[p]
              for p in heads]
    st = [st_ref[p] for p in heads]
    cross = [jnp.dot(st[p].astype(bf16), qT[p], preferred_element_type=f32) * rd_ref[p]
             for p in heads]
    o = [jnp.dot(vT[p], innerT[p].astype(bf16), preferred_element_type=f32) + cross[p]
         for p in heads]
    kw = [(kT[p].astype(f32) * wd_ref[p]).astype(bf16) for p in heads]
    for p in heads:
        st_ref[p] = cd_ref[p] * st[p] + lax.dot_general(vT[p], kw[p], NT_DIMS,
                                                        preferred_element_type=f32)
    for p in heads:
        mu = jnp.mean(o[p], axis=0, keepdims=True)
        d = o[p] - mu
        var = jnp.mean(d * d, axis=0, keepdims=True)
        gate = rows(g_ref, p, RET_V_DIM).astype(f32)
        o_ref[p * RET_V_DIM:(p + 1) * RET_V_DIM, :] = (gate * (d * lax.rsqrt(var + EPS))).astype(bf16)


def retention(qkT, vT, gT):
    c = RET_CHUNK
    h = RET_HEADS
    log_gamma = jnp.log(1.0 - 2.0 ** (-5.0 - jnp.arange(h, dtype=f32)))
    i = jnp.arange(c, dtype=f32)
    diff = i[None, :] - i[:, None]
    dt = jnp.where(diff >= 0, jnp.exp(log_gamma[:, None, None] * jnp.maximum(diff, 0.0)), 0.0)
    rd = jnp.exp(log_gamma[:, None] * (i + 1.0))[:, None, :]
    wd = jnp.exp(log_gamma[:, None] * (c - 1.0 - i))[:, None, :]
    cd = jnp.broadcast_to(jnp.exp(log_gamma * c)[:, None, None], (h, 1, RET_QK_DIM))
    per = RET_HEADS_PER_STEP
    steps = h // per
    vec = lambda n: pl.BlockSpec((per, 1, n), lambda hh, t: (hh, 0, 0))
    return pl.pallas_call(
        _ret_kernel,
        grid=(steps, SEQ // c),
        in_specs=[pl.BlockSpec((per * RET_QK_DIM, c), lambda hh, t: (hh, t)),
                  pl.BlockSpec((per * RET_QK_DIM, c), lambda hh, t: (steps + hh, t)),
                  pl.BlockSpec((per * RET_V_DIM, c), lambda hh, t: (hh, t)),
                  pl.BlockSpec((per * RET_V_DIM, c), lambda hh, t: (hh, t)),
                  pl.BlockSpec((per, c, c), lambda hh, t: (hh, 0, 0)),
                  vec(c), vec(c), vec(RET_QK_DIM)],
        out_specs=pl.BlockSpec((per * RET_V_DIM, c), lambda hh, t: (hh, t)),
        out_shape=jax.ShapeDtypeStruct((h * RET_V_DIM, SEQ), bf16),
        scratch_shapes=[pltpu.VMEM((per, RET_V_DIM, RET_QK_DIM), f32)],
        compiler_params=_params("arbitrary", "arbitrary"),
        name="retention",
    )(qkT, qkT, vT, gT, dt, rd, wd, cd)


def moba_mixer(hT, tabs16, w_in, layer):
    hd = MOBA_HEADS * HEAD_DIM
    proj = functools.partial(project, hT, w_in, layer=layer)
    qT = proj((0, hd), "rope", tables=tabs16, scale=HEAD_DIM ** -0.5, scale_rows=(0, hd),
              name="moba_proj_q")
    k_tok = proj((hd, hd), "rope", tables=tabs16, token_major=True, name="moba_proj_k")
    vT = proj((2 * hd, hd), "plain", name="moba_proj_v")
    zT = proj((3 * hd, hd), "silu", name="moba_proj_z")
    return moba_attention(qT, k_tok, vT, zT)


def nsa_mixer(hT, tabs16, w_in, layer, pe, wk1, wk2, wv1, wv2):
    hd = NSA_HEADS * HEAD_DIM
    gd = NSA_KV_GROUPS * HEAD_DIM
    proj = functools.partial(project, hT, w_in, layer=layer)
    qT = proj((0, hd), "rope", tables=tabs16, scale=HEAD_DIM ** -0.5, scale_rows=(0, hd),
              name="nsa_proj_q")
    k_tok = proj((hd + 2 * gd, 2 * gd), "rope", col_step=2, tables=tabs16, token_major=True,
                 name="nsa_proj_k")
    plainT = proj((hd + 3 * gd, 2 * gd), "plain", col_step=2, name="nsa_proj_v")
    kc_tok = proj((hd, gd), "rope", tables=tabs16, token_major=True, name="nsa_proj_kc")
    vc_tok = proj((hd + gd, gd), "plain", token_major=True, name="nsa_proj_vc")
    g0 = hd + 6 * gd
    per = NSA_REP * 3
    wg = w_in[layer, :, g0:g0 + NSA_KV_GROUPS * per].reshape(D_MODEL, NSA_KV_GROUPS, per)
    wg = jnp.pad(wg, ((0, 0), (0, 0), (0, NSA_GATE_ROWS - per))).reshape(D_MODEL, -1)
    gatesT = project(hT, wg, (0, wg.shape[1]), "sigmoid", out_dtype=f32, name="nsa_proj_gate")
    gatesT = gatesT.reshape(NSA_KV_GROUPS, NSA_GATE_ROWS, SEQ)
    wz = w_in[layer, :, g0 + NSA_HEADS * 3:]
    zT = project(hT, wz, (0, hd), "silu", name="nsa_proj_z")
    k_cmp = nsa_compress(kc_tok, pe, wk1, wk2, transpose_out=False)
    v_cmpT = nsa_compress(vc_tok, pe, wv1, wv2, transpose_out=True)
    return nsa_attention(qT, k_tok, plainT, k_cmp, v_cmpT, gatesT, zT)


def retention_mixer(hT, tabs128, w_in, layer):
    qk = 2 * RET_HEADS * RET_QK_DIM
    vd = RET_HEADS * RET_V_DIM
    proj = functools.partial(project, hT, w_in, layer=layer)
    qkT = proj((0, qk), "rope", tables=tabs128, slab=RET_QK_DIM, half=RET_QK_DIM // 2,
               scale=RET_QK_DIM ** -0.5, scale_rows=(qk // 2, qk), name="ret_proj_qk")
    vT = proj((qk, vd), "plain", name="ret_proj_v")
    gT = proj((qk + vd, vd), "silu", name="ret_proj_g")
    return retention(qkT, vT, gT)


def kernel(x, c, positions, ada_w, ada_b, norm_pre, norm_post, moba_w_in, moba_w_out, nsa_w_in, nsa_w_out, nsa_cmp_pe, nsa_cmp_wk1, nsa_cmp_wk2, nsa_cmp_wv1, nsa_cmp_wv2, ret_w_in, ret_w_out):
    assert x.shape == (1, SEQ, D_MODEL)
    mod = ada_modulation(c, ada_w, ada_b)
    tabs16 = rope_tables(positions, ROPE_DIM // 2, ROPE_THETA)
    tabs128 = rope_tables(positions, RET_QK_DIM // 2, RET_ROT_THETA)
    pre_cols = norm_pre.reshape(DEPTH * D_MODEL, 1)
    post_cols = norm_post.reshape(DEPTH * D_MODEL, 1)
    shift = lambda i: (mod, 3 * i)
    scale = lambda i: (mod, 3 * i + 1)
    gate = lambda i: (mod, 3 * i + 2)
    xT, hT = prenorm(x.reshape(SEQ, D_MODEL), (pre_cols, 0), scale(0), shift(0), token_major_in=True)
    for i in range(DEPTH):
        kind, j = i % N_MIXERS, i // N_MIXERS
        if kind == 0:
            aT, w_out = moba_mixer(hT, tabs16, moba_w_in, j), moba_w_out
        elif kind == 1:
            aT = nsa_mixer(hT, tabs16, nsa_w_in, j, nsa_cmp_pe[j], nsa_cmp_wk1[j], nsa_cmp_wk2[j],
                           nsa_cmp_wv1[j], nsa_cmp_wv2[j])
            w_out = nsa_w_out
        else:
            aT, w_out = retention_mixer(hT, tabs128, ret_w_in, j), ret_w_out
        if i + 1 < DEPTH:
            xT = outproj_residual(aT, w_out, j, xT, gate(i), (post_cols, i))
            hT = prenorm(xT, (pre_cols, i + 1), scale(i + 1), shift(i + 1))
        else:
            xT = outproj_residual(aT, w_out, j, xT, gate(i), (post_cols, i), token_major_out=True)
    return xT.reshape(1, SEQ, D_MODEL)
```

```python
import functools

import numpy as np
import jax
import jax.numpy as jnp
from jax import lax
from jax.experimental import pallas as pl
from jax.experimental.pallas import tpu as pltpu

f32 = jnp.float32
bf16 = jnp.bfloat16
i32 = jnp.int32

D_MODEL = 2048
SEQ = 8192
DEPTH = 4
N_MIXERS = 3
HEAD_DIM = 128
ROPE_THETA = 500000.0
ROPE_DIM = HEAD_DIM // 4
EPS = 1e-6
NEG = -1e30
POS_BIG = 1e30

MOBA_HEADS = D_MODEL // HEAD_DIM
MOBA_BLOCK = 256
MOBA_TOPK = 3
MOBA_NB = SEQ // MOBA_BLOCK
MOBA_TQ = 1024
ONES_ROWS = 16
OUT_T_ROWS = 512
PROJ_TN = 512
PROJ_TM = 2048
Q_SCALE = HEAD_DIM ** -0.5 * float(np.log2(np.e))
RING = 4
RING_AHEAD = 2
COL_GROUP = 256

NSA_HEADS = D_MODEL // HEAD_DIM
NSA_KV_GROUPS = NSA_HEADS // 4
NSA_REP = NSA_HEADS // NSA_KV_GROUPS
NSA_CMP_LEN = 32
NSA_CMP_STRIDE = 16
NSA_CMP_HIDDEN = 256
NSA_SLC_BLOCK = 64
NSA_SLC_TOPK = 16
NSA_WINDOW = 512
NSA_NC_PAD = SEQ // NSA_CMP_STRIDE
NSA_NS = SEQ // NSA_SLC_BLOCK
NSA_GATE_ROWS = 16
NSA_TQ = 256
NSA_SPAN = 1024

RET_HEADS = 8
RET_QK_DIM = D_MODEL // RET_HEADS
RET_V_DIM = 2 * D_MODEL // RET_HEADS
RET_CHUNK = 256
RET_ROT_THETA = 10000.0
RET_HEADS_PER_STEP = 2

VMEM_LIMIT = 56 * 1024 * 1024

TN_DIMS = (((0,), (0,)), ((), ()))
NT_DIMS = (((1,), (1,)), ((), ()))


def _params(*sem):
    return pltpu.CompilerParams(dimension_semantics=sem, vmem_limit_bytes=VMEM_LIMIT)


def _rope_table_kernel(pos_ref, inv_ref, cos_ref, sin_ref):
    ang = pos_ref[...].astype(f32) * inv_ref[...]
    cos_ref[...] = jnp.cos(ang)
    sin_ref[...] = jnp.sin(ang)


def rope_tables(positions, half, theta):
    inv = (theta ** (-jnp.arange(half, dtype=f32) / half)).reshape(half, 1)
    ts = 1024
    return pl.pallas_call(
        _rope_table_kernel,
        grid=(SEQ // ts,),
        in_specs=[pl.BlockSpec((1, ts), lambda i: (0, i)),
                  pl.BlockSpec((half, 1), lambda i: (0, 0))],
        out_specs=[pl.BlockSpec((half, ts), lambda i: (0, i))] * 2,
        out_shape=[jax.ShapeDtypeStruct((half, SEQ), f32)] * 2,
        name=f"rope_tables_{half}",
    )(positions, inv)


def _ada_kernel(c_ref, w_ref, b_ref, o_ref):
    sc = jax.nn.silu(c_ref[...])
    sc8 = jnp.broadcast_to(sc, (8, D_MODEL)).astype(bf16)
    acc = jnp.dot(sc8, w_ref[0].astype(bf16), preferred_element_type=f32)
    o_ref[...] = (acc + b_ref[0]).T[:, 0:1]


def ada_modulation(c, ada_w, ada_b):
    tn = 1024
    n3 = 3 * D_MODEL
    per = n3 // tn
    return pl.pallas_call(
        _ada_kernel,
        grid=(DEPTH, per),
        in_specs=[pl.BlockSpec((1, D_MODEL), lambda l, j: (0, 0)),
                  pl.BlockSpec((1, D_MODEL, tn), lambda l, j: (l, 0, j)),
                  pl.BlockSpec((1, 1, tn), lambda l, j: (l, 0, j))],
        out_specs=pl.BlockSpec((tn, 1), lambda l, j: (l * per + j, 0)),
        out_shape=jax.ShapeDtypeStruct((DEPTH * n3, 1), f32),
        compiler_params=_params("arbitrary", "arbitrary"),
        name="ada_modulation",
    )(c, ada_w, ada_b.reshape(DEPTH, 1, n3))


def _col_spec(block):
    return pl.BlockSpec((D_MODEL, 1), lambda *_: (block, 0))


def _prenorm_kernel(x_ref, g_ref, sc_ref, sh_ref, *o_refs, token_major_in):
    if token_major_in:
        xT_ref, o_ref = o_refs
        x = x_ref[...].T
        xT_ref[...] = x
    else:
        (o_ref,) = o_refs
        x = x_ref[...]
    ms = jnp.mean(x * x, axis=0, keepdims=True)
    y = (x * lax.rsqrt(ms + EPS)) * g_ref[...]
    o_ref[...] = (y * (1.0 + sc_ref[...]) + sh_ref[...]).astype(bf16)


def prenorm(x, g_col, scale_col, shift_col, token_major_in=False):
    tm = 512
    cols = [g_col, scale_col, shift_col]
    fm = pl.BlockSpec((D_MODEL, tm), lambda i: (0, i))
    x_spec = pl.BlockSpec((tm, D_MODEL), lambda i: (i, 0)) if token_major_in else fm
    h_shape = jax.ShapeDtypeStruct((D_MODEL, SEQ), bf16)
    return pl.pallas_call(
        functools.partial(_prenorm_kernel, token_major_in=token_major_in),
        grid=(SEQ // tm,),
        in_specs=[x_spec] + [_col_spec(b) for _, b in cols],
        out_specs=[fm, fm] if token_major_in else fm,
        out_shape=[jax.ShapeDtypeStruct((D_MODEL, SEQ), f32), h_shape] if token_major_in else h_shape,
        compiler_params=_params("arbitrary"),
        name="prenorm",
    )(x, *[a for a, _ in cols])


def _proj_kernel(*refs, kind, tn, slab, half, scale, scale_lo, scale_hi, token_major):
    if kind == "rope":
        h_ref, w_ref, cos_ref, sin_ref, o_ref = refs
    else:
        h_ref, w_ref, o_ref = refs
    acc = lax.dot_general(w_ref[...].astype(bf16), h_ref[...], TN_DIMS,
                          preferred_element_type=f32)
    if scale is not None:
        j = pl.program_id(1)
        acc = acc * jnp.where((j >= scale_lo) & (j < scale_hi), scale, 1.0).astype(f32)
    if kind == "silu":
        acc = acc * jax.nn.sigmoid(acc)
    elif kind == "sigmoid":
        acc = jax.nn.sigmoid(acc)

    if kind == "rope":
        cos = cos_ref[...]
        sin = sin_ref[...]
        pieces = []
        for s in range(tn // slab):
            b = s * slab
            x1 = acc[b:b + half]
            x2 = acc[b + half:b + 2 * half]
            pieces.append(x1 * cos - x2 * sin)
            pieces.append(x1 * sin + x2 * cos)
            if 2 * half < slab:
                pieces.append(acc[b + 2 * half:b + slab])
        acc = jnp.concatenate(pieces, axis=0)
    if token_major:
        for s in range(tn // HEAD_DIM):
            o_ref[s] = acc[s * HEAD_DIM:(s + 1) * HEAD_DIM].T.astype(o_ref.dtype)
    else:
        o_ref[...] = acc.astype(o_ref.dtype)


def project(hT, w, cols, kind, *, layer=None, col_step=1, tables=None, slab=HEAD_DIM,
            half=ROPE_DIM // 2, scale=None, scale_rows=(0, 0), token_major=False,
            out_dtype=bf16, tn=PROJ_TN, name="proj"):
    start, n = cols
    tm = PROJ_TM
    tn = min(tn, n)
    first = start // tn
    assert n % tn == 0 and start % tn == 0
    assert (first + col_step * (n // tn - 1) + 1) * tn <= w.shape[-1]
    assert scale_rows[0] % tn == 0 and scale_rows[1] % tn == 0
    assert kind != "rope" or tn % slab == 0
    if layer is None:
        w_spec = pl.BlockSpec((D_MODEL, tn), lambda i, j: (0, first + col_step * j))
    else:
        w_spec = pl.BlockSpec((None, D_MODEL, tn), lambda i, j: (layer, 0, first + col_step * j))
    in_specs = [pl.BlockSpec((D_MODEL, tm), lambda i, j: (0, i)), w_spec]
    args = [hT, w]
    if kind == "rope":
        in_specs += [pl.BlockSpec((half, tm), lambda i, j: (0, i))] * 2
        args += list(tables)
    if token_major:
        out_spec = pl.BlockSpec((tn // HEAD_DIM, tm, HEAD_DIM), lambda i, j: (j, i, 0))
        out_shape = jax.ShapeDtypeStruct((n // HEAD_DIM, SEQ, HEAD_DIM), out_dtype)
    else:
        out_spec = pl.BlockSpec((tn, tm), lambda i, j: (j, i))
        out_shape = jax.ShapeDtypeStruct((n, SEQ), out_dtype)
    body = functools.partial(
        _proj_kernel, kind=kind, tn=tn, slab=slab, half=half, scale=scale,
        scale_lo=scale_rows[0] // tn, scale_hi=scale_rows[1] // tn, token_major=token_major)
    return pl.pallas_call(
        body, grid=(SEQ // tm, n // tn), in_specs=in_specs, out_specs=out_spec,
        out_shape=out_shape, compiler_params=_params("arbitrary", "arbitrary"), name=name,
    )(*args)


def _outproj_kernel(a_ref, w_ref, x_ref, gate_ref, g_ref, o_ref, wt_ref, *, n_load, token_major_out):
    s = pl.program_id(0)

    @pl.when(s < n_load)
    def _():
        off = pl.multiple_of(s * OUT_T_ROWS, OUT_T_ROWS)
        wt_ref[:, pl.ds(off, OUT_T_ROWS)] = w_ref[...].T.astype(bf16)

    @pl.when(s >= n_load)
    def _():
        y = jnp.dot(wt_ref[...], a_ref[...], preferred_element_type=f32)
        ms = jnp.mean(y * y, axis=0, keepdims=True)
        yn = (y * lax.rsqrt(ms + EPS)) * g_ref[...]
        out = x_ref[...] + gate_ref[...] * yn
        o_ref[...] = out.T if token_major_out else out


def outproj_residual(aT, w_out, layer, xT, gate_col, g_col, token_major_out=False):
    k = aT.shape[0]
    tm = 256
    n_load = k // OUT_T_ROWS
    tile = lambda s: jnp.maximum(s - n_load, 0)
    if token_major_out:
        out_spec = pl.BlockSpec((tm, D_MODEL), lambda s: (tile(s), 0))
        out_shape = jax.ShapeDtypeStruct((SEQ, D_MODEL), f32)
    else:
        out_spec = pl.BlockSpec((D_MODEL, tm), lambda s: (0, tile(s)))
        out_shape = jax.ShapeDtypeStruct((D_MODEL, SEQ), f32)
    return pl.pallas_call(
        functools.partial(_outproj_kernel, n_load=n_load, token_major_out=token_major_out),
        grid=(n_load + SEQ // tm,),
        in_specs=[pl.BlockSpec((k, tm), lambda s: (0, tile(s))),
                  pl.BlockSpec((None, OUT_T_ROWS, D_MODEL),
                               lambda s: (layer, jnp.minimum(s, n_load - 1), 0)),
                  pl.BlockSpec((D_MODEL, tm), lambda s: (0, tile(s))),
                  _col_spec(gate_col[1]), _col_spec(g_col[1])],
        out_specs=out_spec,
        out_shape=out_shape,
        scratch_shapes=[pltpu.VMEM((D_MODEL, k), bf16)],
        compiler_params=_params("arbitrary"),
        name="outproj_residual",
    )(aT, w_out, xT, gate_col[0], g_col[0])


def _first_k_select(val, idx, n_rows, k, slot_ok):
    sel = jnp.zeros(val.shape, f32)
    for t in range(k):
        m = jnp.max(val, axis=0, keepdims=True)
        first = jnp.min(jnp.where(val == m, idx, n_rows), axis=0, keepdims=True)
        pick = idx == first
        sel = jnp.where(pick, slot_ok(t, m), sel)
        val = jnp.where(pick, -jnp.inf, val)
    return sel


def _probs(s, m):
    return jnp.exp2((s - m).astype(bf16))


def _flash_step(s, va, m, acc):
    mn = jnp.maximum(m, jnp.max(s, axis=0, keepdims=True))
    return mn, jnp.exp2(m - mn) * acc + jnp.dot(va, _probs(s, mn), preferred_element_type=f32)


def _flash_step_ref(s_ref, va, m, acc, group=COL_GROUP):
    mns, ps = [], []
    for c in range(s_ref.shape[1] // group):
        cs = slice(c * group, (c + 1) * group)
        mn = jnp.maximum(m[:, cs], jnp.max(s_ref[:, cs], axis=0, keepdims=True))
        mns.append(mn)
        ps.append(_probs(s_ref[:, cs], mn))
    mn = jnp.concatenate(mns, axis=1)
    p = jnp.concatenate(ps, axis=1)
    return mn, jnp.exp2(m - mn) * acc + jnp.dot(va, p, preferred_element_type=f32)


def _flash_ring(k_ref, e_ref, vaug_ref, qaug_ref, bufs, rounds, carry):
    ring = len(bufs)
    span = bufs[0].shape[0]

    def scores(n):
        off = pl.multiple_of(n * span, span)
        kaug = jnp.concatenate([k_ref[0, pl.ds(off, span), :], e_ref[pl.ds(off, span), :]], axis=1)
        return jnp.dot(kaug, qaug_ref[...], preferred_element_type=f32)

    def values(n):
        return vaug_ref[:, pl.ds(pl.multiple_of(n * span, span), span)]

    def one_round(j, carry, prefetch_next):
        n = ring * j
        for t in range(ring):
            if t + RING_AHEAD < ring or prefetch_next:
                bufs[(t + RING_AHEAD) % ring][...] = scores(n + t + RING_AHEAD)
            carry = _flash_step_ref(bufs[t], values(n + t), *carry)
        return carry

    def run(carry):
        for t in range(RING_AHEAD):
            bufs[t][...] = scores(t)
        carry = lax.fori_loop(0, rounds - 1, lambda j, c: one_round(j, c, True), carry)
        return one_round(rounds - 1, carry, False)

    return lax.cond(rounds > 0, run, lambda c: c, carry)


def _moba_attn_kernel(q_ref, k_ref, v_ref, z_ref, avg_ref, e_ref, o_ref,
                      kmean_ref, vaug_ref, qaug_ref, *score_bufs):
    blk = MOBA_BLOCK
    tq = MOBA_TQ
    sub = tq // blk
    i = pl.program_id(1)
    b0 = i * sub

    @pl.when(i == 0)
    def _():
        kmean_ref[...] = jnp.dot(avg_ref[...], k_ref[0], preferred_element_type=f32)
        vaug_ref[0:HEAD_DIM, :] = v_ref[...]
        vaug_ref[HEAD_DIM:, :] = jnp.ones((ONES_ROWS, SEQ), bf16)

    qT = q_ref[...]
    gate = jnp.dot(kmean_ref[...], qT.astype(f32), precision=lax.Precision.HIGHEST,
                   preferred_element_type=f32)
    nidx = lax.broadcasted_iota(i32, (MOBA_NB, tq), 0)
    own = b0 + jnp.right_shift(lax.broadcasted_iota(i32, (1, tq), 1), blk.bit_length() - 1)
    gate = jnp.where(nidx < own, gate, NEG)
    sel = _first_k_select(gate, nidx, MOBA_NB, MOBA_TOPK, lambda t, m: jnp.where(t < own, 1.0, 0.0))
    qaug_ref[0:HEAD_DIM, :] = qT
    qaug_ref[HEAD_DIM:HEAD_DIM + MOBA_NB, :] = jnp.where(sel > 0.5, 0.0, NEG).astype(bf16)
    qaug_ref[HEAD_DIM + MOBA_NB:, :] = jnp.zeros((HEAD_DIM - MOBA_NB, tq), bf16)

    offs = [pl.multiple_of((b0 + c) * blk, blk) for c in range(sub)]
    s_own = jnp.concatenate(
        [jnp.dot(k_ref[0, pl.ds(offs[c], blk), :], qT[:, c * blk:(c + 1) * blk],
                 preferred_element_type=f32) for c in range(sub)], axis=1)
    krow = lax.broadcasted_iota(i32, (blk, tq), 0)
    qcol = lax.broadcasted_iota(i32, (blk, tq), 1) & (blk - 1)
    s_own = jnp.where(krow <= qcol, s_own, NEG)
    m = jnp.max(s_own, axis=0, keepdims=True)
    p = _probs(s_own, m)
    acc = jnp.concatenate(
        [jnp.dot(vaug_ref[:, pl.ds(offs[c], blk)], p[:, c * blk:(c + 1) * blk],
                 preferred_element_type=f32) for c in range(sub)], axis=1)

    inner = (sub - 1) * blk
    kaug = jnp.concatenate([k_ref[0, pl.ds(offs[0], inner), :], e_ref[pl.ds(offs[0], inner), :]],
                           axis=1)
    s_in = jnp.dot(kaug, qaug_ref[:, blk:], preferred_element_type=f32)
    m_hi, acc_hi = _flash_step(s_in, vaug_ref[:, pl.ds(offs[0], inner)], m[:, blk:], acc[:, blk:])
    m = jnp.concatenate([m[:, :blk], m_hi], axis=1)
    acc = jnp.concatenate([acc[:, :blk], acc_hi], axis=1)

    _, acc = _flash_ring(k_ref, e_ref, vaug_ref, qaug_ref, score_bufs, i, (m, acc))
    o = acc[0:HEAD_DIM] / jnp.maximum(acc[HEAD_DIM:HEAD_DIM + 1], 1e-30)
    o_ref[...] = (o * z_ref[...].astype(f32)).astype(bf16)


def moba_attention(qT, k_tok, vT, zT):
    tq = MOBA_TQ
    h = MOBA_HEADS
    blk_of = jnp.arange(SEQ) // MOBA_BLOCK
    avg = ((blk_of[None, :] == jnp.arange(MOBA_NB)[:, None]).astype(f32) / MOBA_BLOCK).astype(bf16)
    onehot = (blk_of[:, None] == jnp.arange(HEAD_DIM)[None, :]).astype(bf16)
    return pl.pallas_call(
        _moba_attn_kernel,
        grid=(h, SEQ // tq),
        in_specs=[pl.BlockSpec((HEAD_DIM, tq), lambda hh, i: (hh, i)),
                  pl.BlockSpec((1, SEQ, HEAD_DIM), lambda hh, i: (hh, 0, 0)),
                  pl.BlockSpec((HEAD_DIM, SEQ), lambda hh, i: (hh, 0)),
                  pl.BlockSpec((HEAD_DIM, tq), lambda hh, i: (hh, i)),
                  pl.BlockSpec((MOBA_NB, SEQ), lambda hh, i: (0, 0)),
                  pl.BlockSpec((SEQ, HEAD_DIM), lambda hh, i: (0, 0))],
        out_specs=pl.BlockSpec((HEAD_DIM, tq), lambda hh, i: (hh, i)),
        out_shape=jax.ShapeDtypeStruct((h * HEAD_DIM, SEQ), bf16),
        scratch_shapes=[pltpu.VMEM((MOBA_NB, HEAD_DIM), f32),
                        pltpu.VMEM((HEAD_DIM + ONES_ROWS, SEQ), bf16),
                        pltpu.VMEM((2 * HEAD_DIM, tq), bf16),
                        ] + [pltpu.VMEM((tq // RING, tq), f32)] * RING,
        compiler_params=_params("arbitrary", "arbitrary"),
        name="moba_attention",
    )(qT, k_tok, vT, zT, avg, onehot)


def _nsa_cmp_kernel(x_ref, wab_ref, pe_ref, w1_ref, w2_ref, o_ref, *, transpose_out):
    hid = NSA_CMP_HIDDEN
    ab = jnp.dot(x_ref[0], wab_ref[...], preferred_element_type=f32)
    first = ab[:, :hid]
    second = pltpu.roll(ab[:, hid:], NSA_NC_PAD - 1, 0)
    pe8 = jnp.broadcast_to(pe_ref[...], (8, NSA_CMP_LEN * HEAD_DIM)).astype(bf16)
    bias = jnp.dot(pe8, w1_ref[...], preferred_element_type=f32)[0:1]
    hpre = first + second + bias
    out = jnp.dot((hpre * jax.nn.sigmoid(hpre)).astype(bf16), w2_ref[...],
                  preferred_element_type=f32)
    o_ref[0] = (out.T if transpose_out else out).astype(o_ref.dtype)


def nsa_compress(t_tok, pe, w1, w2, transpose_out):
    g = NSA_KV_GROUPS
    half_k = NSA_CMP_STRIDE * HEAD_DIM
    x = t_tok.reshape(g, NSA_NC_PAD, half_k)
    w1b = w1.astype(bf16)
    wab = jnp.concatenate([w1b[:half_k], w1b[half_k:]], axis=1)
    oshape = (g, HEAD_DIM, NSA_NC_PAD) if transpose_out else (g, NSA_NC_PAD, HEAD_DIM)
    return pl.pallas_call(
        functools.partial(_nsa_cmp_kernel, transpose_out=transpose_out),
        grid=(g,),
        in_specs=[pl.BlockSpec((1, NSA_NC_PAD, half_k), lambda gg: (gg, 0, 0)),
                  pl.BlockSpec((half_k, 2 * NSA_CMP_HIDDEN), lambda gg: (0, 0)),
                  pl.BlockSpec((1, NSA_CMP_LEN * HEAD_DIM), lambda gg: (0, 0)),
                  pl.BlockSpec((NSA_CMP_LEN * HEAD_DIM, NSA_CMP_HIDDEN), lambda gg: (0, 0)),
                  pl.BlockSpec((NSA_CMP_HIDDEN, HEAD_DIM), lambda gg: (0, 0))],
        out_specs=pl.BlockSpec((1,) + oshape[1:], lambda gg: (gg, 0, 0)),
        out_shape=jax.ShapeDtypeStruct(oshape, bf16),
        compiler_params=_params("arbitrary"),
        name="nsa_compress",
    )(x, wab, pe.reshape(1, NSA_CMP_LEN * HEAD_DIM), w1b, w2.astype(bf16))


def _nsa_attn_kernel(q_ref, kc_ref, vc_ref, ov_ref, ks_ref, vs_ref, kw_ref, vw_ref, e_ref,
                     gt_ref, z_ref, o_ref, vsaug_ref, vwaug_ref, qaug_ref, *score_bufs):
    tq = NSA_TQ
    rep = NSA_REP
    nq = rep * tq
    i = pl.program_id(1)
    q0 = i * tq

    @pl.when(i == 0)
    def _():
        for src, dst in ((vs_ref, vsaug_ref), (vw_ref, vwaug_ref)):
            dst[0:HEAD_DIM, :] = src[...]
            dst[HEAD_DIM:, :] = jnp.ones((ONES_ROWS, SEQ), bf16)

    qT = jnp.concatenate([q_ref[r * HEAD_DIM:(r + 1) * HEAD_DIM, :] for r in range(rep)],
                         axis=1)
    lane = lax.broadcasted_iota(i32, (1, nq), 1)
    qpos = q0 + (lane & (tq - 1))
    qpos1 = q0 + lax.broadcasted_iota(i32, (1, tq), 1)

    def compressed(nc):
        sc = jnp.dot(kc_ref[0, 0:nc, :], qT, preferred_element_type=f32)
        cend = lax.broadcasted_iota(i32, (nc, nq), 0) * NSA_CMP_STRIDE + (NSA_CMP_LEN - 1)
        ok_c = cend <= qpos
        sc = jnp.where(ok_c, sc, NEG)
        pc = jnp.where(ok_c, jnp.exp2(sc - jnp.max(sc, axis=0, keepdims=True)), 0.0)
        pc = pc / jnp.maximum(jnp.sum(pc, axis=0, keepdims=True), 1e-30)
        ocT = jnp.dot(vc_ref[0, :, 0:nc], pc.astype(bf16), preferred_element_type=f32)
        ps = pc[:, 0:tq]
        for r in range(1, rep):
            ps = ps + pc[:, r * tq:(r + 1) * tq]
        ps_hi = ps.astype(bf16)
        ps_lo = (ps - ps_hi.astype(f32)).astype(bf16)
        imp = (jnp.dot(ov_ref[:, 0:nc], ps_hi, preferred_element_type=f32)
               + jnp.dot(ov_ref[:, 0:nc], ps_lo, preferred_element_type=f32))
        return ocT, imp

    half_nc = NSA_NC_PAD // 2
    ocT, imp = lax.cond((q0 + tq) <= half_nc * NSA_CMP_STRIDE,
                        lambda: compressed(half_nc), lambda: compressed(NSA_NC_PAD))
    sidx = lax.broadcasted_iota(i32, (NSA_NS, tq), 0)
    cur = jnp.right_shift(qpos1, 6)
    valid = sidx * NSA_SLC_BLOCK <= qpos1
    forced = (sidx == 0) | (sidx == cur) | (sidx == cur - 1)
    val = jnp.where(valid, jnp.where(forced, POS_BIG, imp), NEG)
    sel = _first_k_select(val, sidx, NSA_NS, NSA_SLC_TOPK, lambda t, m: jnp.where(m > NEG / 2, 1.0, 0.0))

    bias = jnp.where(sel > 0.5, 0.0, NEG)
    qaug_ref[0:HEAD_DIM, :] = qT
    qaug_ref[HEAD_DIM:, :] = jnp.concatenate([bias] * rep, axis=1).astype(bf16)
    offd = pl.multiple_of(q0, tq)
    kaug = jnp.concatenate([ks_ref[0, pl.ds(offd, tq), :], e_ref[pl.ds(offd, tq), :]], axis=1)
    s = jnp.dot(kaug, qaug_ref[...], preferred_element_type=f32)
    s = jnp.where(q0 + lax.broadcasted_iota(i32, (tq, nq), 0) <= qpos, s, NEG)
    m = jnp.max(s, axis=0, keepdims=True)
    acc = jnp.dot(vsaug_ref[:, pl.ds(offd, tq)], _probs(s, m),
                  preferred_element_type=f32)

    own_first = i * (tq // NSA_SLC_BLOCK)
    bias_past = jnp.where(sidx >= own_first, NEG, bias)
    qaug_ref[HEAD_DIM:, :] = jnp.concatenate([bias_past] * rep, axis=1).astype(bf16)

    per_span = NSA_SPAN // tq
    rounds = jnp.right_shift(i + per_span - 1, per_span.bit_length() - 1)
    _, acc = _flash_ring(ks_ref, e_ref, vsaug_ref, qaug_ref, score_bufs, rounds, (m, acc))
    osT = acc[0:HEAD_DIM] / jnp.maximum(acc[HEAD_DIM:HEAD_DIM + 1], 1e-30)

    wspan = NSA_WINDOW + tq
    start = pl.multiple_of(jnp.maximum(q0 - NSA_WINDOW, 0), tq)
    sw = jnp.dot(kw_ref[0, pl.ds(start, wspan), :], qT, preferred_element_type=f32)
    kpos = start + lax.broadcasted_iota(i32, (wspan, nq), 0)
    sw = jnp.where((kpos <= qpos) & (kpos > qpos - NSA_WINDOW), sw, NEG)
    pw = _probs(sw, jnp.max(sw, axis=0, keepdims=True))
    accw = jnp.dot(vwaug_ref[:, pl.ds(start, wspan)], pw, preferred_element_type=f32)
    owT = accw[0:HEAD_DIM] / jnp.maximum(accw[HEAD_DIM:HEAD_DIM + 1], 1e-30)

    gt = gt_ref[0]
    for r in range(rep):
        cs = slice(r * tq, (r + 1) * tq)
        rs = slice(r * HEAD_DIM, (r + 1) * HEAD_DIM)
        o = (gt[3 * r:3 * r + 1] * ocT[:, cs] + gt[3 * r + 1:3 * r + 2] * osT[:, cs]
             + gt[3 * r + 2:3 * r + 3] * owT[:, cs])
        o_ref[rs, :] = (o * z_ref[rs, :].astype(f32)).astype(bf16)


def _nsa_overlap_t():
    nc = NSA_NC_PAD - 1
    cs = np.arange(nc)[:, None] * NSA_CMP_STRIDE
    ss = np.arange(NSA_NS)[None, :] * NSA_SLC_BLOCK
    ov = np.clip(np.minimum(cs + NSA_CMP_LEN, ss + NSA_SLC_BLOCK) - np.maximum(cs, ss), 0, None)
    ov = np.concatenate([ov / NSA_CMP_LEN, np.zeros((1, NSA_NS))], axis=0)
    return jnp.asarray(ov.T, dtype=bf16)


def nsa_attention(qT, k_tok, plainT, k_cmp, v_cmpT, gatesT, zT):
    tq = NSA_TQ
    g = NSA_KV_GROUPS
    qrows = NSA_REP * HEAD_DIM
    onehot = ((jnp.arange(SEQ) // NSA_SLC_BLOCK)[:, None] == jnp.arange(NSA_NS)[None, :]).astype(bf16)
    keys = lambda first: pl.BlockSpec((1, SEQ, HEAD_DIM), lambda gg, i: (first + gg, 0, 0))
    vals = lambda first: pl.BlockSpec((HEAD_DIM, SEQ), lambda gg, i: (first + gg, 0))
    return pl.pallas_call(
        _nsa_attn_kernel,
        grid=(g, SEQ // tq),
        in_specs=[pl.BlockSpec((qrows, tq), lambda gg, i: (gg, i)),
                  pl.BlockSpec((1, NSA_NC_PAD, HEAD_DIM), lambda gg, i: (gg, 0, 0)),
                  pl.BlockSpec((1, HEAD_DIM, NSA_NC_PAD), lambda gg, i: (gg, 0, 0)),
                  pl.BlockSpec((NSA_NS, NSA_NC_PAD), lambda gg, i: (0, 0)),
                  keys(0), vals(0), keys(g), vals(g),
                  pl.BlockSpec((SEQ, NSA_NS), lambda gg, i: (0, 0)),
                  pl.BlockSpec((1, NSA_GATE_ROWS, tq), lambda gg, i: (gg, 0, i)),
                  pl.BlockSpec((qrows, tq), lambda gg, i: (gg, i))],
        out_specs=pl.BlockSpec((qrows, tq), lambda gg, i: (gg, i)),
        out_shape=jax.ShapeDtypeStruct((NSA_HEADS * HEAD_DIM, SEQ), bf16),
        scratch_shapes=[pltpu.VMEM((HEAD_DIM + ONES_ROWS, SEQ), bf16),
                        pltpu.VMEM((HEAD_DIM + ONES_ROWS, SEQ), bf16),
                        pltpu.VMEM((HEAD_DIM + NSA_NS, NSA_REP * tq), bf16),
                        ] + [pltpu.VMEM((NSA_SPAN // RING, NSA_REP * tq), f32)] * RING,
        compiler_params=_params("arbitrary", "arbitrary"),
        name="nsa_attention",
    )(qT, k_cmp, v_cmpT, _nsa_overlap_t(), k_tok, plainT, k_tok, plainT, onehot, gatesT, zT)


def _ret_kernel(q_ref, k_ref, v_ref, g_ref, dt_ref, rd_ref, wd_ref, cd_ref, o_ref, st_ref):
    @pl.when(pl.program_id(1) == 0)
    def _():
        st_ref[...] = jnp.zeros(st_ref.shape, f32)

    heads = range(RET_HEADS_PER_STEP)
    rows = lambda ref, p, n: ref[p * n:(p + 1) * n, :]
    qT = [rows(q_ref, p, RET_QK_DIM) for p in heads]
    kT = [rows(k_ref, p, RET_QK_DIM) for p in heads]
    vT = [rows(v_ref, p, RET_V_DIM) for p in heads]
    innerT = [lax.dot_general(kT[p], qT[p], TN_DIMS, preferred_element_type=f32) * dt_ref[p]
              for p in heads]
    st = [st_ref[p] for p in heads]
    cross = [jnp.dot(st[p].astype(bf16), qT[p], preferred_element_type=f32) * rd_ref[p]
             for p in heads]
    o = [jnp.dot(vT[p], innerT[p].astype(bf16), preferred_element_type=f32) + cross[p]
         for p in heads]
    kw = [(kT[p].astype(f32) * wd_ref[p]).astype(bf16) for p in heads]
    for p in heads:
        st_ref[p] = cd_ref[p] * st[p] + lax.dot_general(vT[p], kw[p], NT_DIMS,
                                                        preferred_element_type=f32)
    for p in heads:
        mu = jnp.mean(o[p], axis=0, keepdims=True)
        d = o[p] - mu
        var = jnp.mean(d * d, axis=0, keepdims=True)
        gate = rows(g_ref, p, RET_V_DIM).astype(f32)
        o_ref[p * RET_V_DIM:(p + 1) * RET_V_DIM, :] = (gate * (d * lax.rsqrt(var + EPS))).astype(bf16)


def retention(qkT, vT, gT):
    c = RET_CHUNK
    h = RET_HEADS
    log_gamma = jnp.log(1.0 - 2.0 ** (-5.0 - jnp.arange(h, dtype=f32)))
    i = jnp.arange(c, dtype=f32)
    diff = i[None, :] - i[:, None]
    dt = jnp.where(diff >= 0, jnp.exp(log_gamma[:, None, None] * jnp.maximum(diff, 0.0)), 0.0)
    rd = jnp.exp(log_gamma[:, None] * (i + 1.0))[:, None, :]
    wd = jnp.exp(log_gamma[:, None] * (c - 1.0 - i))[:, None, :]
    cd = jnp.broadcast_to(jnp.exp(log_gamma * c)[:, None, None], (h, 1, RET_QK_DIM))
    per = RET_HEADS_PER_STEP
    steps = h // per
    vec = lambda n: pl.BlockSpec((per, 1, n), lambda hh, t: (hh, 0, 0))
    return pl.pallas_call(
        _ret_kernel,
        grid=(steps, SEQ // c),
        in_specs=[pl.BlockSpec((per * RET_QK_DIM, c), lambda hh, t: (hh, t)),
                  pl.BlockSpec((per * RET_QK_DIM, c), lambda hh, t: (steps + hh, t)),
                  pl.BlockSpec((per * RET_V_DIM, c), lambda hh, t: (hh, t)),
                  pl.BlockSpec((per * RET_V_DIM, c), lambda hh, t: (hh, t)),
                  pl.BlockSpec((per, c, c), lambda hh, t: (hh, 0, 0)),
                  vec(c), vec(c), vec(RET_QK_DIM)],
        out_specs=pl.BlockSpec((per * RET_V_DIM, c), lambda hh, t: (hh, t)),
        out_shape=jax.ShapeDtypeStruct((h * RET_V_DIM, SEQ), bf16),
        scratch_shapes=[pltpu.VMEM((per, RET_V_DIM, RET_QK_DIM), f32)],
        compiler_params=_params("arbitrary", "arbitrary"),
        name="retention",
    )(qkT, qkT, vT, gT, dt, rd, wd, cd)


def moba_mixer(hT, tabs16, w_in, layer):
    hd = MOBA_HEADS * HEAD_DIM
    proj = functools.partial(project, hT, w_in, layer=layer)
    qT = proj((0, hd), "rope", tables=tabs16, scale=Q_SCALE, scale_rows=(0, hd),
              name="moba_proj_q")
    k_tok = proj((hd, hd), "rope", tables=tabs16, token_major=True, name="moba_proj_k")
    vT = proj((2 * hd, hd), "plain", name="moba_proj_v")
    zT = proj((3 * hd, hd), "silu", name="moba_proj_z")
    return moba_attention(qT, k_tok, vT, zT)


def nsa_mixer(hT, tabs16, w_in, layer, pe, wk1, wk2, wv1, wv2):
    hd = NSA_HEADS * HEAD_DIM
    gd = NSA_KV_GROUPS * HEAD_DIM
    proj = functools.partial(project, hT, w_in, layer=layer)
    qT = proj((0, hd), "rope", tables=tabs16, scale=Q_SCALE, scale_rows=(0, hd),
              name="nsa_proj_q")
    k_tok = proj((hd + 2 * gd, 2 * gd), "rope", col_step=2, tables=tabs16, token_major=True,
                 name="nsa_proj_k")
    plainT = proj((hd + 3 * gd, 2 * gd), "plain", col_step=2, name="nsa_proj_v")
    kc_tok = proj((hd, gd), "rope", tables=tabs16, token_major=True, name="nsa_proj_kc")
    vc_tok = proj((hd + gd, gd), "plain", token_major=True, name="nsa_proj_vc")
    g0 = hd + 6 * gd
    per = NSA_REP * 3
    wg = w_in[layer, :, g0:g0 + NSA_KV_GROUPS * per].reshape(D_MODEL, NSA_KV_GROUPS, per)
    wg = jnp.pad(wg, ((0, 0), (0, 0), (0, NSA_GATE_ROWS - per))).reshape(D_MODEL, -1)
    gatesT = project(hT, wg, (0, wg.shape[1]), "sigmoid", out_dtype=f32, name="nsa_proj_gate")
    gatesT = gatesT.reshape(NSA_KV_GROUPS, NSA_GATE_ROWS, SEQ)
    wz = w_in[layer, :, g0 + NSA_HEADS * 3:]
    zT = project(hT, wz, (0, hd), "silu", name="nsa_proj_z")
    k_cmp = nsa_compress(kc_tok, pe, wk1, wk2, transpose_out=False)
    v_cmpT = nsa_compress(vc_tok, pe, wv1, wv2, transpose_out=True)
    return nsa_attention(qT, k_tok, plainT, k_cmp, v_cmpT, gatesT, zT)


def retention_mixer(hT, tabs128, w_in, layer):
    qk = 2 * RET_HEADS * RET_QK_DIM
    vd = RET_HEADS * RET_V_DIM
    proj = functools.partial(project, hT, w_in, layer=layer)
    qkT = proj((0, qk), "rope", tables=tabs128, slab=RET_QK_DIM, half=RET_QK_DIM // 2,
               scale=RET_QK_DIM ** -0.5, scale_rows=(qk // 2, qk), name="ret_proj_qk")
    vT = proj((qk, vd), "plain", name="ret_proj_v")
    gT = proj((qk + vd, vd), "silu", name="ret_proj_g")
    return retention(qkT, vT, gT)


def kernel(x, c, positions, ada_w, ada_b, norm_pre, norm_post, moba_w_in, moba_w_out, nsa_w_in, nsa_w_out, nsa_cmp_pe, nsa_cmp_wk1, nsa_cmp_wk2, nsa_cmp_wv1, nsa_cmp_wv2, ret_w_in, ret_w_out):
    assert x.shape == (1, SEQ, D_MODEL)
    mod = ada_modulation(c, ada_w, ada_b)
    tabs16 = rope_tables(positions, ROPE_DIM // 2, ROPE_THETA)
    tabs128 = rope_tables(positions, RET_QK_DIM // 2, RET_ROT_THETA)
    pre_cols = norm_pre.reshape(DEPTH * D_MODEL, 1)
    post_cols = norm_post.reshape(DEPTH * D_MODEL, 1)
    shift = lambda i: (mod, 3 * i)
    scale = lambda i: (mod, 3 * i + 1)
    gate = lambda i: (mod, 3 * i + 2)
    xT, hT = prenorm(x.reshape(SEQ, D_MODEL), (pre_cols, 0), scale(0), shift(0), token_major_in=True)
    for i in range(DEPTH):
        kind, j = i % N_MIXERS, i // N_MIXERS
        if kind == 0:
            aT, w_out = moba_mixer(hT, tabs16, moba_w_in, j), moba_w_out
        elif kind == 1:
            aT = nsa_mixer(hT, tabs16, nsa_w_in, j, nsa_cmp_pe[j], nsa_cmp_wk1[j], nsa_cmp_wk2[j],
                           nsa_cmp_wv1[j], nsa_cmp_wv2[j])
            w_out = nsa_w_out
        else:
            aT, w_out = retention_mixer(hT, tabs128, ret_w_in, j), ret_w_out
        if i + 1 < DEPTH:
            xT = outproj_residual(aT, w_out, j, xT, gate(i), (post_cols, i))
            hT = prenorm(xT, (pre_cols, i + 1), scale(i + 1), shift(i + 1))
        else:
            xT = outproj_residual(aT, w_out, j, xT, gate(i), (post_cols, i), token_major_out=True)
    return xT.reshape(1, SEQ, D_MODEL)
```

```python
import functools

import numpy as np
import jax
import jax.numpy as jnp
from jax import lax
from jax.experimental import pallas as pl
from jax.experimental.pallas import tpu as pltpu

f32 = jnp.float32
bf16 = jnp.bfloat16
i32 = jnp.int32

D_MODEL = 2048
SEQ = 8192
DEPTH = 4
N_MIXERS = 3
HEAD_DIM = 128
ROPE_THETA = 500000.0
ROPE_DIM = HEAD_DIM // 4
EPS = 1e-6
NEG = -1e30
POS_BIG = 1e30

MOBA_HEADS = D_MODEL // HEAD_DIM
MOBA_BLOCK = 256
MOBA_TOPK = 3
MOBA_NB = SEQ // MOBA_BLOCK
MOBA_TQ = 1024
ONES_ROWS = 16
OUT_T_ROWS = 512
PROJ_TN = 512
PROJ_TM = 2048
Q_SCALE = HEAD_DIM ** -0.5 * float(np.log2(np.e))
RING = 4
RING_AHEAD = 2
COL_GROUP = 512

NSA_HEADS = D_MODEL // HEAD_DIM
NSA_KV_GROUPS = NSA_HEADS // 4
NSA_REP = NSA_HEADS // NSA_KV_GROUPS
NSA_CMP_LEN = 32
NSA_CMP_STRIDE = 16
NSA_CMP_HIDDEN = 256
NSA_SLC_BLOCK = 64
NSA_SLC_TOPK = 16
NSA_WINDOW = 512
NSA_NC_PAD = SEQ // NSA_CMP_STRIDE
NSA_NS = SEQ // NSA_SLC_BLOCK
NSA_GATE_ROWS = 16
NSA_TQ = 256
NSA_SPAN = 1024

RET_HEADS = 8
RET_QK_DIM = D_MODEL // RET_HEADS
RET_V_DIM = 2 * D_MODEL // RET_HEADS
RET_CHUNK = 256
RET_ROT_THETA = 10000.0
RET_HEADS_PER_STEP = 2

VMEM_LIMIT = 56 * 1024 * 1024

TN_DIMS = (((0,), (0,)), ((), ()))
NT_DIMS = (((1,), (1,)), ((), ()))


def _params(*sem):
    return pltpu.CompilerParams(dimension_semantics=sem, vmem_limit_bytes=VMEM_LIMIT)


def _rope_table_kernel(pos_ref, inv_ref, cos_ref, sin_ref):
    ang = pos_ref[...].astype(f32) * inv_ref[...]
    cos_ref[...] = jnp.cos(ang)
    sin_ref[...] = jnp.sin(ang)


def rope_tables(positions, half, theta):
    inv = (theta ** (-jnp.arange(half, dtype=f32) / half)).reshape(half, 1)
    ts = 1024
    return pl.pallas_call(
        _rope_table_kernel,
        grid=(SEQ // ts,),
        in_specs=[pl.BlockSpec((1, ts), lambda i: (0, i)),
                  pl.BlockSpec((half, 1), lambda i: (0, 0))],
        out_specs=[pl.BlockSpec((half, ts), lambda i: (0, i))] * 2,
        out_shape=[jax.ShapeDtypeStruct((half, SEQ), f32)] * 2,
        name=f"rope_tables_{half}",
    )(positions, inv)


def _ada_kernel(c_ref, w_ref, b_ref, o_ref):
    sc = jax.nn.silu(c_ref[...])
    sc8 = jnp.broadcast_to(sc, (8, D_MODEL)).astype(bf16)
    acc = jnp.dot(sc8, w_ref[0].astype(bf16), preferred_element_type=f32)
    o_ref[...] = (acc + b_ref[0]).T[:, 0:1]


def ada_modulation(c, ada_w, ada_b):
    tn = 1024
    n3 = 3 * D_MODEL
    per = n3 // tn
    return pl.pallas_call(
        _ada_kernel,
        grid=(DEPTH, per),
        in_specs=[pl.BlockSpec((1, D_MODEL), lambda l, j: (0, 0)),
                  pl.BlockSpec((1, D_MODEL, tn), lambda l, j: (l, 0, j)),
                  pl.BlockSpec((1, 1, tn), lambda l, j: (l, 0, j))],
        out_specs=pl.BlockSpec((tn, 1), lambda l, j: (l * per + j, 0)),
        out_shape=jax.ShapeDtypeStruct((DEPTH * n3, 1), f32),
        compiler_params=_params("arbitrary", "arbitrary"),
        name="ada_modulation",
    )(c, ada_w, ada_b.reshape(DEPTH, 1, n3))


def _col_spec(block):
    return pl.BlockSpec((D_MODEL, 1), lambda *_: (block, 0))


def _prenorm_kernel(x_ref, g_ref, sc_ref, sh_ref, *o_refs, token_major_in):
    if token_major_in:
        xT_ref, o_ref = o_refs
        x = x_ref[...].T
        xT_ref[...] = x
    else:
        (o_ref,) = o_refs
        x = x_ref[...]
    ms = jnp.mean(x * x, axis=0, keepdims=True)
    y = (x * lax.rsqrt(ms + EPS)) * g_ref[...]
    o_ref[...] = (y * (1.0 + sc_ref[...]) + sh_ref[...]).astype(bf16)


def prenorm(x, g_col, scale_col, shift_col, token_major_in=False):
    tm = 512
    cols = [g_col, scale_col, shift_col]
    fm = pl.BlockSpec((D_MODEL, tm), lambda i: (0, i))
    x_spec = pl.BlockSpec((tm, D_MODEL), lambda i: (i, 0)) if token_major_in else fm
    h_shape = jax.ShapeDtypeStruct((D_MODEL, SEQ), bf16)
    return pl.pallas_call(
        functools.partial(_prenorm_kernel, token_major_in=token_major_in),
        grid=(SEQ // tm,),
        in_specs=[x_spec] + [_col_spec(b) for _, b in cols],
        out_specs=[fm, fm] if token_major_in else fm,
        out_shape=[jax.ShapeDtypeStruct((D_MODEL, SEQ), f32), h_shape] if token_major_in else h_shape,
        compiler_params=_params("arbitrary"),
        name="prenorm",
    )(x, *[a for a, _ in cols])


def _proj_kernel(*refs, kind, tn, slab, half, scale, scale_lo, scale_hi, token_major):
    if kind == "rope":
        h_ref, w_ref, cos_ref, sin_ref, o_ref = refs
    else:
        h_ref, w_ref, o_ref = refs
    acc = lax.dot_general(w_ref[...].astype(bf16), h_ref[...], TN_DIMS,
                          preferred_element_type=f32)
    if scale is not None:
        j = pl.program_id(1)
        acc = acc * jnp.where((j >= scale_lo) & (j < scale_hi), scale, 1.0).astype(f32)
    if kind == "silu":
        acc = acc * jax.nn.sigmoid(acc)
    elif kind == "sigmoid":
        acc = jax.nn.sigmoid(acc)

    if kind == "rope":
        cos = cos_ref[...]
        sin = sin_ref[...]
        pieces = []
        for s in range(tn // slab):
            b = s * slab
            x1 = acc[b:b + half]
            x2 = acc[b + half:b + 2 * half]
            pieces.append(x1 * cos - x2 * sin)
            pieces.append(x1 * sin + x2 * cos)
            if 2 * half < slab:
                pieces.append(acc[b + 2 * half:b + slab])
        acc = jnp.concatenate(pieces, axis=0)
    if token_major:
        for s in range(tn // HEAD_DIM):
            o_ref[s] = acc[s * HEAD_DIM:(s + 1) * HEAD_DIM].T.astype(o_ref.dtype)
    else:
        o_ref[...] = acc.astype(o_ref.dtype)


def project(hT, w, cols, kind, *, layer=None, col_step=1, tables=None, slab=HEAD_DIM,
            half=ROPE_DIM // 2, scale=None, scale_rows=(0, 0), token_major=False,
            out_dtype=bf16, tn=PROJ_TN, name="proj"):
    start, n = cols
    tm = PROJ_TM
    tn = min(tn, n)
    first = start // tn
    assert n % tn == 0 and start % tn == 0
    assert (first + col_step * (n // tn - 1) + 1) * tn <= w.shape[-1]
    assert scale_rows[0] % tn == 0 and scale_rows[1] % tn == 0
    assert kind != "rope" or tn % slab == 0
    if layer is None:
        w_spec = pl.BlockSpec((D_MODEL, tn), lambda i, j: (0, first + col_step * j))
    else:
        w_spec = pl.BlockSpec((None, D_MODEL, tn), lambda i, j: (layer, 0, first + col_step * j))
    in_specs = [pl.BlockSpec((D_MODEL, tm), lambda i, j: (0, i)), w_spec]
    args = [hT, w]
    if kind == "rope":
        in_specs += [pl.BlockSpec((half, tm), lambda i, j: (0, i))] * 2
        args += list(tables)
    if token_major:
        out_spec = pl.BlockSpec((tn // HEAD_DIM, tm, HEAD_DIM), lambda i, j: (j, i, 0))
        out_shape = jax.ShapeDtypeStruct((n // HEAD_DIM, SEQ, HEAD_DIM), out_dtype)
    else:
        out_spec = pl.BlockSpec((tn, tm), lambda i, j: (j, i))
        out_shape = jax.ShapeDtypeStruct((n, SEQ), out_dtype)
    body = functools.partial(
        _proj_kernel, kind=kind, tn=tn, slab=slab, half=half, scale=scale,
        scale_lo=scale_rows[0] // tn, scale_hi=scale_rows[1] // tn, token_major=token_major)
    return pl.pallas_call(
        body, grid=(SEQ // tm, n // tn), in_specs=in_specs, out_specs=out_spec,
        out_shape=out_shape, compiler_params=_params("arbitrary", "arbitrary"), name=name,
    )(*args)


def _outproj_kernel(a_ref, w_ref, x_ref, gate_ref, g_ref, o_ref, wt_ref, *, n_load, token_major_out):
    s = pl.program_id(0)

    @pl.when(s < n_load)
    def _():
        off = pl.multiple_of(s * OUT_T_ROWS, OUT_T_ROWS)
        wt_ref[:, pl.ds(off, OUT_T_ROWS)] = w_ref[...].T.astype(bf16)

    @pl.when(s >= n_load)
    def _():
        y = jnp.dot(wt_ref[...], a_ref[...], preferred_element_type=f32)
        ms = jnp.mean(y * y, axis=0, keepdims=True)
        yn = (y * lax.rsqrt(ms + EPS)) * g_ref[...]
        out = x_ref[...] + gate_ref[...] * yn
        o_ref[...] = out.T if token_major_out else out


def outproj_residual(aT, w_out, layer, xT, gate_col, g_col, token_major_out=False):
    k = aT.shape[0]
    tm = 256
    n_load = k // OUT_T_ROWS
    tile = lambda s: jnp.maximum(s - n_load, 0)
    if token_major_out:
        out_spec = pl.BlockSpec((tm, D_MODEL), lambda s: (tile(s), 0))
        out_shape = jax.ShapeDtypeStruct((SEQ, D_MODEL), f32)
    else:
        out_spec = pl.BlockSpec((D_MODEL, tm), lambda s: (0, tile(s)))
        out_shape = jax.ShapeDtypeStruct((D_MODEL, SEQ), f32)
    return pl.pallas_call(
        functools.partial(_outproj_kernel, n_load=n_load, token_major_out=token_major_out),
        grid=(n_load + SEQ // tm,),
        in_specs=[pl.BlockSpec((k, tm), lambda s: (0, tile(s))),
                  pl.BlockSpec((None, OUT_T_ROWS, D_MODEL),
                               lambda s: (layer, jnp.minimum(s, n_load - 1), 0)),
                  pl.BlockSpec((D_MODEL, tm), lambda s: (0, tile(s))),
                  _col_spec(gate_col[1]), _col_spec(g_col[1])],
        out_specs=out_spec,
        out_shape=out_shape,
        scratch_shapes=[pltpu.VMEM((D_MODEL, k), bf16)],
        compiler_params=_params("arbitrary"),
        name="outproj_residual",
    )(aT, w_out, xT, gate_col[0], g_col[0])


def _first_k_select(val, idx, n_rows, k, slot_ok):
    sel = jnp.zeros(val.shape, f32)
    for t in range(k):
        m = jnp.max(val, axis=0, keepdims=True)
        first = jnp.min(jnp.where(val == m, idx, n_rows), axis=0, keepdims=True)
        pick = idx == first
        sel = jnp.where(pick, slot_ok(t, m), sel)
        val = jnp.where(pick, -jnp.inf, val)
    return sel


def _probs(s, m):
    return jnp.exp2((s - m).astype(bf16))


def _flash_step(s, va, m, acc):
    mn = jnp.maximum(m, jnp.max(s, axis=0, keepdims=True))
    return mn, jnp.exp2(m - mn) * acc + jnp.dot(va, _probs(s, mn), preferred_element_type=f32)


def _flash_step_state(s_ref, va, m_ref, acc_ref, group=COL_GROUP):
    for c in range(s_ref.shape[1] // group):
        cs = slice(c * group, (c + 1) * group)
        m = m_ref[:, cs]
        mn = jnp.maximum(m, jnp.max(s_ref[:, cs], axis=0, keepdims=True))
        pv = jnp.dot(va, _probs(s_ref[:, cs], mn), preferred_element_type=f32)
        acc_ref[:, cs] = jnp.exp2(m - mn) * acc_ref[:, cs] + pv
        m_ref[:, cs] = mn


def _flash_ring(k_ref, e_ref, vaug_ref, qaug_ref, bufs, rounds, state, first_step=0):
    ring = len(bufs)
    ahead = min(RING_AHEAD, ring - 1)
    span = bufs[0].shape[0]

    def scores(n):
        off = pl.multiple_of((first_step + n) * span, span)
        kaug = jnp.concatenate([k_ref[0, pl.ds(off, span), :], e_ref[pl.ds(off, span), :]], axis=1)
        return jnp.dot(kaug, qaug_ref[...], preferred_element_type=f32)

    def values(n):
        return vaug_ref[:, pl.ds(pl.multiple_of((first_step + n) * span, span), span)]

    m_ref, acc_ref = state

    def one_round(j, prefetch_next):
        n = ring * j
        for t in range(ring):
            if t + ahead < ring or prefetch_next:
                bufs[(t + ahead) % ring][...] = scores(n + t + ahead)
            _flash_step_state(bufs[t], values(n + t), m_ref, acc_ref)

    @pl.when(rounds > 0)
    def _():
        for t in range(ahead):
            bufs[t][...] = scores(t)

        def body(j, c):
            one_round(j, True)
            return c

        lax.fori_loop(0, rounds - 1, body, 0)
        one_round(rounds - 1, False)


def _moba_attn_kernel(q_ref, k_ref, v_ref, z_ref, avg_ref, e_ref, o_ref,
                      kmean_ref, vaug_ref, qaug_ref, *score_bufs):
    blk = MOBA_BLOCK
    tq = MOBA_TQ
    sub = tq // blk
    i = pl.program_id(1)
    b0 = i * sub

    @pl.when(i == 0)
    def _():
        kmean_ref[...] = jnp.dot(avg_ref[...], k_ref[0], preferred_element_type=f32)
        vaug_ref[0:HEAD_DIM, :] = v_ref[...]
        vaug_ref[HEAD_DIM:, :] = jnp.ones((ONES_ROWS, SEQ), bf16)

    qT = q_ref[...]
    gate = jnp.dot(kmean_ref[...], qT.astype(f32), precision=lax.Precision.HIGHEST,
                   preferred_element_type=f32)
    nidx = lax.broadcasted_iota(i32, (MOBA_NB, tq), 0)
    own = b0 + jnp.right_shift(lax.broadcasted_iota(i32, (1, tq), 1), blk.bit_length() - 1)
    gate = jnp.where(nidx < own, gate, NEG)
    sel = _first_k_select(gate, nidx, MOBA_NB, MOBA_TOPK, lambda t, m: jnp.where(t < own, 1.0, 0.0))
    qaug_ref[0:HEAD_DIM, :] = qT
    qaug_ref[HEAD_DIM:HEAD_DIM + MOBA_NB, :] = jnp.where(sel > 0.5, 0.0, NEG).astype(bf16)
    qaug_ref[HEAD_DIM + MOBA_NB:, :] = jnp.zeros((HEAD_DIM - MOBA_NB, tq), bf16)

    offs = [pl.multiple_of((b0 + c) * blk, blk) for c in range(sub)]
    s_own = jnp.concatenate(
        [jnp.dot(k_ref[0, pl.ds(offs[c], blk), :], qT[:, c * blk:(c + 1) * blk],
                 preferred_element_type=f32) for c in range(sub)], axis=1)
    krow = lax.broadcasted_iota(i32, (blk, tq), 0)
    qcol = lax.broadcasted_iota(i32, (blk, tq), 1) & (blk - 1)
    s_own = jnp.where(krow <= qcol, s_own, NEG)
    m = jnp.max(s_own, axis=0, keepdims=True)
    p = _probs(s_own, m)
    acc = jnp.concatenate(
        [jnp.dot(vaug_ref[:, pl.ds(offs[c], blk)], p[:, c * blk:(c + 1) * blk],
                 preferred_element_type=f32) for c in range(sub)], axis=1)

    inner = (sub - 1) * blk
    kaug = jnp.concatenate([k_ref[0, pl.ds(offs[0], inner), :], e_ref[pl.ds(offs[0], inner), :]],
                           axis=1)
    s_in = jnp.dot(kaug, qaug_ref[:, blk:], preferred_element_type=f32)
    m_hi, acc_hi = _flash_step(s_in, vaug_ref[:, pl.ds(offs[0], inner)], m[:, blk:], acc[:, blk:])
    m = jnp.concatenate([m[:, :blk], m_hi], axis=1)
    acc = jnp.concatenate([acc[:, :blk], acc_hi], axis=1)

    m_ref, acc_ref = score_bufs[RING:]
    m_ref[...] = m
    acc_ref[...] = acc
    _flash_ring(k_ref, e_ref, vaug_ref, qaug_ref, score_bufs[:RING], i, (m_ref, acc_ref))
    acc = acc_ref[...]
    o = acc[0:HEAD_DIM] / jnp.maximum(acc[HEAD_DIM:HEAD_DIM + 1], 1e-30)
    o_ref[...] = (o * z_ref[...].astype(f32)).astype(bf16)


def moba_attention(qT, k_tok, vT, zT):
    tq = MOBA_TQ
    h = MOBA_HEADS
    blk_of = jnp.arange(SEQ) // MOBA_BLOCK
    avg = ((blk_of[None, :] == jnp.arange(MOBA_NB)[:, None]).astype(f32) / MOBA_BLOCK).astype(bf16)
    onehot = (blk_of[:, None] == jnp.arange(HEAD_DIM)[None, :]).astype(bf16)
    return pl.pallas_call(
        _moba_attn_kernel,
        grid=(h, SEQ // tq),
        in_specs=[pl.BlockSpec((HEAD_DIM, tq), lambda hh, i: (hh, i)),
                  pl.BlockSpec((1, SEQ, HEAD_DIM), lambda hh, i: (hh, 0, 0)),
                  pl.BlockSpec((HEAD_DIM, SEQ), lambda hh, i: (hh, 0)),
                  pl.BlockSpec((HEAD_DIM, tq), lambda hh, i: (hh, i)),
                  pl.BlockSpec((MOBA_NB, SEQ), lambda hh, i: (0, 0)),
                  pl.BlockSpec((SEQ, HEAD_DIM), lambda hh, i: (0, 0))],
        out_specs=pl.BlockSpec((HEAD_DIM, tq), lambda hh, i: (hh, i)),
        out_shape=jax.ShapeDtypeStruct((h * HEAD_DIM, SEQ), bf16),
        scratch_shapes=[pltpu.VMEM((MOBA_NB, HEAD_DIM), f32),
                        pltpu.VMEM((HEAD_DIM + ONES_ROWS, SEQ), bf16),
                        pltpu.VMEM((2 * HEAD_DIM, tq), bf16),
                        ] + [pltpu.VMEM((tq // RING, tq), f32)] * RING
        + [pltpu.VMEM((1, tq), f32), pltpu.VMEM((HEAD_DIM + ONES_ROWS, tq), f32)],
        compiler_params=_params("arbitrary", "arbitrary"),
        name="moba_attention",
    )(qT, k_tok, vT, zT, avg, onehot)


def _nsa_cmp_kernel(x_ref, wab_ref, pe_ref, w1_ref, w2_ref, o_ref, *, transpose_out):
    hid = NSA_CMP_HIDDEN
    ab = jnp.dot(x_ref[0], wab_ref[...], preferred_element_type=f32)
    first = ab[:, :hid]
    second = pltpu.roll(ab[:, hid:], NSA_NC_PAD - 1, 0)
    pe8 = jnp.broadcast_to(pe_ref[...], (8, NSA_CMP_LEN * HEAD_DIM)).astype(bf16)
    bias = jnp.dot(pe8, w1_ref[...], preferred_element_type=f32)[0:1]
    hpre = first + second + bias
    out = jnp.dot((hpre * jax.nn.sigmoid(hpre)).astype(bf16), w2_ref[...],
                  preferred_element_type=f32)
    o_ref[0] = (out.T if transpose_out else out).astype(o_ref.dtype)


def nsa_compress(t_tok, pe, w1, w2, transpose_out):
    g = NSA_KV_GROUPS
    half_k = NSA_CMP_STRIDE * HEAD_DIM
    x = t_tok.reshape(g, NSA_NC_PAD, half_k)
    w1b = w1.astype(bf16)
    wab = jnp.concatenate([w1b[:half_k], w1b[half_k:]], axis=1)
    oshape = (g, HEAD_DIM, NSA_NC_PAD) if transpose_out else (g, NSA_NC_PAD, HEAD_DIM)
    return pl.pallas_call(
        functools.partial(_nsa_cmp_kernel, transpose_out=transpose_out),
        grid=(g,),
        in_specs=[pl.BlockSpec((1, NSA_NC_PAD, half_k), lambda gg: (gg, 0, 0)),
                  pl.BlockSpec((half_k, 2 * NSA_CMP_HIDDEN), lambda gg: (0, 0)),
                  pl.BlockSpec((1, NSA_CMP_LEN * HEAD_DIM), lambda gg: (0, 0)),
                  pl.BlockSpec((NSA_CMP_LEN * HEAD_DIM, NSA_CMP_HIDDEN), lambda gg: (0, 0)),
                  pl.BlockSpec((NSA_CMP_HIDDEN, HEAD_DIM), lambda gg: (0, 0))],
        out_specs=pl.BlockSpec((1,) + oshape[1:], lambda gg: (gg, 0, 0)),
        out_shape=jax.ShapeDtypeStruct(oshape, bf16),
        compiler_params=_params("arbitrary"),
        name="nsa_compress",
    )(x, wab, pe.reshape(1, NSA_CMP_LEN * HEAD_DIM), w1b, w2.astype(bf16))


def _nsa_attn_kernel(q_ref, kc_ref, vc_ref, ov_ref, ks_ref, vs_ref, kw_ref, vw_ref, e_ref,
                     gt_ref, z_ref, o_ref, vsaug_ref, vwaug_ref, qaug_ref, *score_bufs):
    tq = NSA_TQ
    rep = NSA_REP
    nq = rep * tq
    i = pl.program_id(1)
    q0 = i * tq

    @pl.when(i == 0)
    def _():
        for src, dst in ((vs_ref, vsaug_ref), (vw_ref, vwaug_ref)):
            dst[0:HEAD_DIM, :] = src[...]
            dst[HEAD_DIM:, :] = jnp.ones((ONES_ROWS, SEQ), bf16)

    qT = jnp.concatenate([q_ref[r * HEAD_DIM:(r + 1) * HEAD_DIM, :] for r in range(rep)],
                         axis=1)
    lane = lax.broadcasted_iota(i32, (1, nq), 1)
    qpos = q0 + (lane & (tq - 1))
    qpos1 = q0 + lax.broadcasted_iota(i32, (1, tq), 1)

    def compressed(nc):
        sc = jnp.dot(kc_ref[0, 0:nc, :], qT, preferred_element_type=f32)
        cend = lax.broadcasted_iota(i32, (nc, nq), 0) * NSA_CMP_STRIDE + (NSA_CMP_LEN - 1)
        ok_c = cend <= qpos
        sc = jnp.where(ok_c, sc, NEG)
        pc = jnp.where(ok_c, jnp.exp2(sc - jnp.max(sc, axis=0, keepdims=True)), 0.0)
        pc = pc / jnp.maximum(jnp.sum(pc, axis=0, keepdims=True), 1e-30)
        ocT = jnp.dot(vc_ref[0, :, 0:nc], pc.astype(bf16), preferred_element_type=f32)
        ps = pc[:, 0:tq]
        for r in range(1, rep):
            ps = ps + pc[:, r * tq:(r + 1) * tq]
        ps_hi = ps.astype(bf16)
        ps_lo = (ps - ps_hi.astype(f32)).astype(bf16)
        imp = (jnp.dot(ov_ref[:, 0:nc], ps_hi, preferred_element_type=f32)
               + jnp.dot(ov_ref[:, 0:nc], ps_lo, preferred_element_type=f32))
        return ocT, imp

    half_nc = NSA_NC_PAD // 2
    ocT, imp = lax.cond((q0 + tq) <= half_nc * NSA_CMP_STRIDE,
                        lambda: compressed(half_nc), lambda: compressed(NSA_NC_PAD))
    sidx = lax.broadcasted_iota(i32, (NSA_NS, tq), 0)
    cur = jnp.right_shift(qpos1, 6)
    valid = sidx * NSA_SLC_BLOCK <= qpos1
    forced = (sidx == 0) | (sidx == cur) | (sidx == cur - 1)
    val = jnp.where(valid, jnp.where(forced, POS_BIG, imp), NEG)
    sel = _first_k_select(val, sidx, NSA_NS, NSA_SLC_TOPK, lambda t, m: jnp.where(m > NEG / 2, 1.0, 0.0))

    bias = jnp.where(sel > 0.5, 0.0, NEG)
    qaug_ref[0:HEAD_DIM, :] = qT
    qaug_ref[HEAD_DIM:, :] = jnp.concatenate([bias] * rep, axis=1).astype(bf16)
    offd = pl.multiple_of(q0, tq)
    kaug = jnp.concatenate([ks_ref[0, pl.ds(offd, tq), :], e_ref[pl.ds(offd, tq), :]], axis=1)
    s = jnp.dot(kaug, qaug_ref[...], preferred_element_type=f32)
    s = jnp.where(q0 + lax.broadcasted_iota(i32, (tq, nq), 0) <= qpos, s, NEG)
    m = jnp.max(s, axis=0, keepdims=True)
    acc = jnp.dot(vsaug_ref[:, pl.ds(offd, tq)], _probs(s, m),
                  preferred_element_type=f32)

    own_first = i * (tq // NSA_SLC_BLOCK)
    bias_past = jnp.where(sidx >= own_first, NEG, bias)
    qaug_ref[HEAD_DIM:, :] = jnp.concatenate([bias_past] * rep, axis=1).astype(bf16)

    per_span = NSA_SPAN // tq
    rounds = jnp.right_shift(i + per_span - 1, per_span.bit_length() - 1)
    m_ref, acc_ref = score_bufs[RING:]
    m_ref[...] = m
    acc_ref[...] = acc
    _flash_ring(ks_ref, e_ref, vsaug_ref, qaug_ref, score_bufs[:RING], rounds, (m_ref, acc_ref))
    acc = acc_ref[...]
    osT = acc[0:HEAD_DIM] / jnp.maximum(acc[HEAD_DIM:HEAD_DIM + 1], 1e-30)

    wspan = NSA_WINDOW + tq
    start = pl.multiple_of(jnp.maximum(q0 - NSA_WINDOW, 0), tq)
    sw = jnp.dot(kw_ref[0, pl.ds(start, wspan), :], qT, preferred_element_type=f32)
    kpos = start + lax.broadcasted_iota(i32, (wspan, nq), 0)
    sw = jnp.where((kpos <= qpos) & (kpos > qpos - NSA_WINDOW), sw, NEG)
    pw = _probs(sw, jnp.max(sw, axis=0, keepdims=True))
    accw = jnp.dot(vwaug_ref[:, pl.ds(start, wspan)], pw, preferred_element_type=f32)
    owT = accw[0:HEAD_DIM] / jnp.maximum(accw[HEAD_DIM:HEAD_DIM + 1], 1e-30)

    gt = gt_ref[0]
    for r in range(rep):
        cs = slice(r * tq, (r + 1) * tq)
        rs = slice(r * HEAD_DIM, (r + 1) * HEAD_DIM)
        o = (gt[3 * r:3 * r + 1] * ocT[:, cs] + gt[3 * r + 1:3 * r + 2] * osT[:, cs]
             + gt[3 * r + 2:3 * r + 3] * owT[:, cs])
        o_ref[rs, :] = (o * z_ref[rs, :].astype(f32)).astype(bf16)


def _nsa_overlap_t():
    nc = NSA_NC_PAD - 1
    cs = np.arange(nc)[:, None] * NSA_CMP_STRIDE
    ss = np.arange(NSA_NS)[None, :] * NSA_SLC_BLOCK
    ov = np.clip(np.minimum(cs + NSA_CMP_LEN, ss + NSA_SLC_BLOCK) - np.maximum(cs, ss), 0, None)
    ov = np.concatenate([ov / NSA_CMP_LEN, np.zeros((1, NSA_NS))], axis=0)
    return jnp.asarray(ov.T, dtype=bf16)


def nsa_attention(qT, k_tok, plainT, k_cmp, v_cmpT, gatesT, zT):
    tq = NSA_TQ
    g = NSA_KV_GROUPS
    qrows = NSA_REP * HEAD_DIM
    onehot = ((jnp.arange(SEQ) // NSA_SLC_BLOCK)[:, None] == jnp.arange(NSA_NS)[None, :]).astype(bf16)
    keys = lambda first: pl.BlockSpec((1, SEQ, HEAD_DIM), lambda gg, i: (first + gg, 0, 0))
    vals = lambda first: pl.BlockSpec((HEAD_DIM, SEQ), lambda gg, i: (first + gg, 0))
    return pl.pallas_call(
        _nsa_attn_kernel,
        grid=(g, SEQ // tq),
        in_specs=[pl.BlockSpec((qrows, tq), lambda gg, i: (gg, i)),
                  pl.BlockSpec((1, NSA_NC_PAD, HEAD_DIM), lambda gg, i: (gg, 0, 0)),
                  pl.BlockSpec((1, HEAD_DIM, NSA_NC_PAD), lambda gg, i: (gg, 0, 0)),
                  pl.BlockSpec((NSA_NS, NSA_NC_PAD), lambda gg, i: (0, 0)),
                  keys(0), vals(0), keys(g), vals(g),
                  pl.BlockSpec((SEQ, NSA_NS), lambda gg, i: (0, 0)),
                  pl.BlockSpec((1, NSA_GATE_ROWS, tq), lambda gg, i: (gg, 0, i)),
                  pl.BlockSpec((qrows, tq), lambda gg, i: (gg, i))],
        out_specs=pl.BlockSpec((qrows, tq), lambda gg, i: (gg, i)),
        out_shape=jax.ShapeDtypeStruct((NSA_HEADS * HEAD_DIM, SEQ), bf16),
        scratch_shapes=[pltpu.VMEM((HEAD_DIM + ONES_ROWS, SEQ), bf16),
                        pltpu.VMEM((HEAD_DIM + ONES_ROWS, SEQ), bf16),
                        pltpu.VMEM((HEAD_DIM + NSA_NS, NSA_REP * tq), bf16),
                        ] + [pltpu.VMEM((NSA_SPAN // RING, NSA_REP * tq), f32)] * RING
        + [pltpu.VMEM((1, NSA_REP * tq), f32), pltpu.VMEM((HEAD_DIM + ONES_ROWS, NSA_REP * tq), f32)],
        compiler_params=_params("arbitrary", "arbitrary"),
        name="nsa_attention",
    )(qT, k_cmp, v_cmpT, _nsa_overlap_t(), k_tok, plainT, k_tok, plainT, onehot, gatesT, zT)


def _ret_kernel(q_ref, k_ref, v_ref, g_ref, dt_ref, rd_ref, wd_ref, cd_ref, o_ref, st_ref):
    @pl.when(pl.program_id(1) == 0)
    def _():
        st_ref[...] = jnp.zeros(st_ref.shape, f32)

    heads = range(RET_HEADS_PER_STEP)
    rows = lambda ref, p, n: ref[p * n:(p + 1) * n, :]
    qT = [rows(q_ref, p, RET_QK_DIM) for p in heads]
    kT = [rows(k_ref, p, RET_QK_DIM) for p in heads]
    vT = [rows(v_ref, p, RET_V_DIM) for p in heads]
    innerT = [lax.dot_general(kT[p], qT[p], TN_DIMS, preferred_element_type=f32) * dt_ref[p]
              for p in heads]
    st = [st_ref[p] for p in heads]
    cross = [jnp.dot(st[p].astype(bf16), qT[p], preferred_element_type=f32) * rd_ref[p]
             for p in heads]
    o = [jnp.dot(vT[p], innerT[p].astype(bf16), preferred_element_type=f32) + cross[p]
         for p in heads]
    kw = [(kT[p].astype(f32) * wd_ref[p]).astype(bf16) for p in heads]
    for p in heads:
        st_ref[p] = cd_ref[p] * st[p] + lax.dot_general(vT[p], kw[p], NT_DIMS,
                                                        preferred_element_type=f32)
    for p in heads:
        mu = jnp.mean(o[p], axis=0, keepdims=True)
        d = o[p] - mu
        var = jnp.mean(d * d, axis=0, keepdims=True)
        gate = rows(g_ref, p, RET_V_DIM).astype(f32)
        o_ref[p * RET_V_DIM:(p + 1) * RET_V_DIM, :] = (gate * (d * lax.rsqrt(var + EPS))).astype(bf16)


def retention(qkT, vT, gT):
    c = RET_CHUNK
    h = RET_HEADS
    log_gamma = jnp.log(1.0 - 2.0 ** (-5.0 - jnp.arange(h, dtype=f32)))
    i = jnp.arange(c, dtype=f32)
    diff = i[None, :] - i[:, None]
    dt = jnp.where(diff >= 0, jnp.exp(log_gamma[:, None, None] * jnp.maximum(diff, 0.0)), 0.0)
    rd = jnp.exp(log_gamma[:, None] * (i + 1.0))[:, None, :]
    wd = jnp.exp(log_gamma[:, None] * (c - 1.0 - i))[:, None, :]
    cd = jnp.broadcast_to(jnp.exp(log_gamma * c)[:, None, None], (h, 1, RET_QK_DIM))
    per = RET_HEADS_PER_STEP
    steps = h // per
    vec = lambda n: pl.BlockSpec((per, 1, n), lambda hh, t: (hh, 0, 0))
    return pl.pallas_call(
        _ret_kernel,
        grid=(steps, SEQ // c),
        in_specs=[pl.BlockSpec((per * RET_QK_DIM, c), lambda hh, t: (hh, t)),
                  pl.BlockSpec((per * RET_QK_DIM, c), lambda hh, t: (steps + hh, t)),
                  pl.BlockSpec((per * RET_V_DIM, c), lambda hh, t: (hh, t)),
                  pl.BlockSpec((per * RET_V_DIM, c), lambda hh, t: (hh, t)),
                  pl.BlockSpec((per, c, c), lambda hh, t: (hh, 0, 0)),
                  vec(c), vec(c), vec(RET_QK_DIM)],
        out_specs=pl.BlockSpec((per * RET_V_DIM, c), lambda hh, t: (hh, t)),
        out_shape=jax.ShapeDtypeStruct((h * RET_V_DIM, SEQ), bf16),
        scratch_shapes=[pltpu.VMEM((per, RET_V_DIM, RET_QK_DIM), f32)],
        compiler_params=_params("arbitrary", "arbitrary"),
        name="retention",
    )(qkT, qkT, vT, gT, dt, rd, wd, cd)


def moba_mixer(hT, tabs16, w_in, layer):
    hd = MOBA_HEADS * HEAD_DIM
    proj = functools.partial(project, hT, w_in, layer=layer)
    qT = proj((0, hd), "rope", tables=tabs16, scale=Q_SCALE, scale_rows=(0, hd),
              name="moba_proj_q")
    k_tok = proj((hd, hd), "rope", tables=tabs16, token_major=True, name="moba_proj_k")
    vT = proj((2 * hd, hd), "plain", name="moba_proj_v")
    zT = proj((3 * hd, hd), "silu", name="moba_proj_z")
    return moba_attention(qT, k_tok, vT, zT)


def nsa_mixer(hT, tabs16, w_in, layer, pe, wk1, wk2, wv1, wv2):
    hd = NSA_HEADS * HEAD_DIM
    gd = NSA_KV_GROUPS * HEAD_DIM
    proj = functools.partial(project, hT, w_in, layer=layer)
    qT = proj((0, hd), "rope", tables=tabs16, scale=Q_SCALE, scale_rows=(0, hd),
              name="nsa_proj_q")
    k_tok = proj((hd + 2 * gd, 2 * gd), "rope", col_step=2, tables=tabs16, token_major=True,
                 name="nsa_proj_k")
    plainT = proj((hd + 3 * gd, 2 * gd), "plain", col_step=2, name="nsa_proj_v")
    kc_tok = proj((hd, gd), "rope", tables=tabs16, token_major=True, name="nsa_proj_kc")
    vc_tok = proj((hd + gd, gd), "plain", token_major=True, name="nsa_proj_vc")
    g0 = hd + 6 * gd
    per = NSA_REP * 3
    wg = w_in[layer, :, g0:g0 + NSA_KV_GROUPS * per].reshape(D_MODEL, NSA_KV_GROUPS, per)
    wg = jnp.pad(wg, ((0, 0), (0, 0), (0, NSA_GATE_ROWS - per))).reshape(D_MODEL, -1)
    gatesT = project(hT, wg, (0, wg.shape[1]), "sigmoid", out_dtype=f32, name="nsa_proj_gate")
    gatesT = gatesT.reshape(NSA_KV_GROUPS, NSA_GATE_ROWS, SEQ)
    wz = w_in[layer, :, g0 + NSA_HEADS * 3:]
    zT = project(hT, wz, (0, hd), "silu", name="nsa_proj_z")
    k_cmp = nsa_compress(kc_tok, pe, wk1, wk2, transpose_out=False)
    v_cmpT = nsa_compress(vc_tok, pe, wv1, wv2, transpose_out=True)
    return nsa_attention(qT, k_tok, plainT, k_cmp, v_cmpT, gatesT, zT)


def retention_mixer(hT, tabs128, w_in, layer):
    qk = 2 * RET_HEADS * RET_QK_DIM
    vd = RET_HEADS * RET_V_DIM
    proj = functools.partial(project, hT, w_in, layer=layer)
    qkT = proj((0, qk), "rope", tables=tabs128, slab=RET_QK_DIM, half=RET_QK_DIM // 2,
               scale=RET_QK_DIM ** -0.5, scale_rows=(qk // 2, qk), name="ret_proj_qk")
    vT = proj((qk, vd), "plain", name="ret_proj_v")
    gT = proj((qk + vd, vd), "silu", name="ret_proj_g")
    return retention(qkT, vT, gT)


def kernel(x, c, positions, ada_w, ada_b, norm_pre, norm_post, moba_w_in, moba_w_out, nsa_w_in, nsa_w_out, nsa_cmp_pe, nsa_cmp_wk1, nsa_cmp_wk2, nsa_cmp_wv1, nsa_cmp_wv2, ret_w_in, ret_w_out):
    assert x.shape == (1, SEQ, D_MODEL)
    mod = ada_modulation(c, ada_w, ada_b)
    tabs16 = rope_tables(positions, ROPE_DIM // 2, ROPE_THETA)
    tabs128 = rope_tables(positions, RET_QK_DIM // 2, RET_ROT_THETA)
    pre_cols = norm_pre.reshape(DEPTH * D_MODEL, 1)
    post_cols = norm_post.reshape(DEPTH * D_MODEL, 1)
    shift = lambda i: (mod, 3 * i)
    scale = lambda i: (mod, 3 * i + 1)
    gate = lambda i: (mod, 3 * i + 2)
    xT, hT = prenorm(x.reshape(SEQ, D_MODEL), (pre_cols, 0), scale(0), shift(0), token_major_in=True)
    for i in range(DEPTH):
        kind, j = i % N_MIXERS, i // N_MIXERS
        if kind == 0:
            aT, w_out = moba_mixer(hT, tabs16, moba_w_in, j), moba_w_out
        elif kind == 1:
            aT = nsa_mixer(hT, tabs16, nsa_w_in, j, nsa_cmp_pe[j], nsa_cmp_wk1[j], nsa_cmp_wk2[j],
                           nsa_cmp_wv1[j], nsa_cmp_wv2[j])
            w_out = nsa_w_out
        else:
            aT, w_out = retention_mixer(hT, tabs128, ret_w_in, j), ret_w_out
        if i + 1 < DEPTH:
            xT = outproj_residual(aT, w_out, j, xT, gate(i), (post_cols, i))
            hT = prenorm(xT, (pre_cols, i + 1), scale(i + 1), shift(i + 1))
        else:
            xT = outproj_residual(aT, w_out, j, xT, gate(i), (post_cols, i), token_major_out=True)
    return xT.reshape(1, SEQ, D_MODEL)
```
